```python
import jax, jax.numpy as jnp
from jax import lax
import numpy as np

D_MODEL = 2048
BATCH = 2
SEQ = 4096
DEPTH = 2
DEC_BATCH = 128
DEC_SEQ = 4
PAST_LEN = 8192
PAGE_SIZE = 128

A_HEADS = 16
A_KV_HEADS = 4
A_HEAD_DIM = 64
A_GROUP = A_HEADS // A_KV_HEADS
A_Q_WIDTH = A_HEADS * A_HEAD_DIM
A_KV_WIDTH = A_KV_HEADS * A_HEAD_DIM
WINDOW = 128
ROT_DIM = A_HEAD_DIM // 4
ROPE_THETA = 500000.0
B_HEADS = 8
B_HEAD_DIM = 128
B_WIDTH = B_HEADS * B_HEAD_DIM
CHUNK = 128
C_WIDTH = 1024
C_CONV_WIDTH = 31
POOL_SIZES = (2, 4, 8, 16)
D_GROUPS = len(POOL_SIZES)
POOL_MAX = max(POOL_SIZES)
D_WIDTH = 1024
D_GROUP_WIDTH = D_WIDTH // D_GROUPS
D_FF = 5632
FFN_CONV_WIDTH = 3
MIX_WIDTH = A_Q_WIDTH + B_WIDTH
EVEN_IN = A_Q_WIDTH + 2 * A_KV_WIDTH + 2 * B_WIDTH
ODD_IN = 2 * C_WIDTH + D_WIDTH
N_EVEN = (DEPTH + 1) // 2
N_ODD = DEPTH // 2
EPS = 1e-6
NEG_INF = -1e30

kernel_name = "hybrid_swa_gmlp_conformer_pool_decode_step"


def rms_norm(x, g):
    xf = x.astype(jnp.float32)
    y = xf * lax.rsqrt(jnp.mean(xf * xf, axis=-1, keepdims=True) + EPS)
    return (y * g.astype(jnp.float32)).astype(x.dtype)


def layer_norm(x, g, b):
    xf = x.astype(jnp.float32)
    mu = jnp.mean(xf, axis=-1, keepdims=True)
    xc = xf - mu
    var = jnp.mean(xc * xc, axis=-1, keepdims=True)
    y = xc * lax.rsqrt(var + EPS) * g.astype(jnp.float32) + b.astype(jnp.float32)
    return y.astype(x.dtype)


def ada_mod(c, w, b):
    m = jax.nn.silu(c) @ w + b
    shift, scale, gate = jnp.split(m[:, None, :], 3, axis=-1)
    return shift, scale, gate


def rope_partial(x, pos):
    half = ROT_DIM // 2
    inv = ROPE_THETA ** (-jnp.arange(half, dtype=jnp.float32) * 2.0 / ROT_DIM)
    ang = pos.astype(jnp.float32)[:, None] * inv[None, :]
    cos = jnp.cos(ang)[:, None, :]
    sin = jnp.sin(ang)[:, None, :]
    xr = x[..., :ROT_DIM].astype(jnp.float32)
    x1, x2 = xr[..., :half], xr[..., half:]
    rot = jnp.concatenate([x1 * cos - x2 * sin, x2 * cos + x1 * sin], axis=-1).astype(x.dtype)
    return jnp.concatenate([rot, x[..., ROT_DIM:]], axis=-1)


def sink_attention(q, k_new, v_new, k_prev, v_prev, pos0, sinks):
    n, t = q.shape[0], q.shape[1]
    l = k_prev.shape[1]
    qb = min(t, WINDOW)
    nb = t // qb
    k_ext = jnp.concatenate([k_prev, k_new], axis=1)
    v_ext = jnp.concatenate([v_prev, v_new], axis=1)
    blk = jnp.arange(nb)[:, None] * qb
    idx = blk + jnp.arange(qb + l)[None, :]
    kb = jnp.take(k_ext, idx, axis=1)
    vb = jnp.take(v_ext, idx, axis=1)
    qg = q.reshape(n, nb, qb, A_KV_HEADS, A_GROUP, A_HEAD_DIM)
    s = jnp.einsum('nbqkgd,nbskd->nbkgqs', qg, kb,
                   preferred_element_type=jnp.float32) * (A_HEAD_DIM ** -0.5)
    qpos = pos0 + blk + jnp.arange(qb)[None, :]
    kpos = pos0 - l + idx
    rel = qpos[:, :, None] - kpos[:, None, :]
    mask = (rel >= 0) & (rel < WINDOW) & (kpos[:, None, :] >= 0)
    s = jnp.where(mask[None, :, None, None], s, NEG_INF)
    sink = jnp.broadcast_to(
        sinks.astype(jnp.float32).reshape(1, 1, A_KV_HEADS, A_GROUP, 1, 1), s.shape[:-1] + (1,))
    p = jax.nn.softmax(jnp.concatenate([s, sink], axis=-1), axis=-1)[..., :-1]
    o = jnp.einsum('nbkgqs,nbskd->nbqkgd', p.astype(v_ext.dtype), vb)
    return o.reshape(n, t, A_Q_WIDTH), k_ext[:, -l:], v_ext[:, -l:]


def spatial_gating(u, v, ws, bias):
    n, t = u.shape[0], u.shape[1]
    nc = -(-t // CHUNK)
    vp = jnp.pad(v, ((0, 0), (0, nc * CHUNK - t), (0, 0))).reshape(n, nc, CHUNK, B_HEADS, B_HEAD_DIM)
    w = jnp.tril(ws)
    mixed = jnp.einsum('hts,ncshd->ncthd', w, vp) + jnp.transpose(bias)[:, :, None]
    mixed = mixed.reshape(n, nc * CHUNK, B_WIDTH)[:, :t]
    return u * mixed


def causal_dwconv(prev, x, w, b):
    xe = jnp.concatenate([prev, x], axis=1)
    y = lax.conv_general_dilated(xe, w[:, None, :].astype(xe.dtype), window_strides=(1,),
                                 padding='VALID', dimension_numbers=('NWC', 'WIO', 'NWC'),
                                 feature_group_count=x.shape[-1])
    return y + b, xe[:, -prev.shape[1]:]


def multiscale_pool(prev, z, pos0, w, b, scale):
    n, t = z.shape[0], z.shape[1]
    p = prev.shape[1]
    ze = jnp.concatenate([prev, z], axis=1)
    cs = jnp.pad(jnp.cumsum(ze.astype(jnp.float32), axis=1), ((0, 0), (1, 0), (0, 0)))
    pos = pos0 + jnp.arange(t)
    pooled = []
    for g, wsz in enumerate(POOL_SIZES):
        sl = slice(g * D_GROUP_WIDTH, (g + 1) * D_GROUP_WIDTH)
        win_sum = cs[:, p + 1:p + 1 + t, sl] - cs[:, p + 1 - wsz:p + 1 - wsz + t, sl]
        cnt = jnp.minimum(pos + 1, wsz).astype(jnp.float32)[None, :, None]
        pooled.append(win_sum / cnt)
    pooled = jnp.stack(pooled, axis=2)
    diff = (pooled - z.reshape(n, t, D_GROUPS, D_GROUP_WIDTH).astype(jnp.float32)).astype(z.dtype)
    out = jnp.einsum('ntgc,gcd->ntgd', diff, w) + b
    return out.reshape(n, t, D_WIDTH) * scale, ze[:, -p:]


def even_mixer(h, pos0, k_prev, v_prev, w_in, w_out, sinks, ln_g, ln_b, ws, wb):
    n, t = h.shape[0], h.shape[1]
    z = h @ w_in
    q, k, v, zb = jnp.split(z, [A_Q_WIDTH, A_Q_WIDTH + A_KV_WIDTH, A_Q_WIDTH + 2 * A_KV_WIDTH], axis=-1)
    pos = pos0 + jnp.arange(t)
    q = rope_partial(q.reshape(n, t, A_HEADS, A_HEAD_DIM), pos)
    k = rope_partial(k.reshape(n, t, A_KV_HEADS, A_HEAD_DIM), pos)
    v = v.reshape(n, t, A_KV_HEADS, A_HEAD_DIM)
    o_a, nk, nv = sink_attention(q, k, v, k_prev, v_prev, pos0, sinks)
    u, vb = jnp.split(jax.nn.gelu(zb, approximate=True), 2, axis=-1)
    vb = layer_norm(vb, ln_g, ln_b)
    o_b = spatial_gating(u, vb, ws, wb)
    return jnp.concatenate([o_a, o_b], axis=-1) @ w_out, nk, nv, vb


def odd_mixer(h, pos0, c_prev, d_prev, w_in, w_out, cw, cb, c_g, c_b, dw, db, dscale):
    z = h @ w_in
    za, zg, zd = jnp.split(z, [C_WIDTH, 2 * C_WIDTH], axis=-1)
    glu = za * jax.nn.sigmoid(zg)
    conv, new_c = causal_dwconv(c_prev, glu, cw, cb)
    o_c = jax.nn.silu(layer_norm(conv, c_g, c_b))
    o_d, new_d = multiscale_pool(d_prev, zd, pos0, dw, db, dscale)
    return jnp.concatenate([o_c, o_d], axis=-1) @ w_out, new_c, new_d


def conv_glu_ffn(prev, h, w_up, cw, cb, w_down):
    up = h @ w_up
    upc, new_f = causal_dwconv(prev, up, cw, cb)
    g, val = jnp.split(upc, 2, axis=-1)
    return (jax.nn.gelu(g, approximate=True) * val) @ w_down, new_f


def run_group(x, c, pos0, a_k, a_v, c_st, d_st, f_st,
              norm_g, ada_w, ada_b, w_in_e, w_out_e, a_sinks, b_ln_g, b_ln_b, b_ws, b_bias,
              w_in_o, w_out_o, c_conv_w, c_conv_b, c_ln_g, c_ln_b, d_w, d_b, d_scale,
              ffn_w_up, ffn_conv_w, ffn_conv_b, ffn_w_down):
    nak, nav, nbv, ncc, ndp, nff = [], [], [], [], [], []
    for i in range(DEPTH):
        j = i // 2
        shift, scale, gate = ada_mod(c, ada_w[i, 0], ada_b[i, 0])
        h = rms_norm(x, norm_g[i, 0]) * (1 + scale) + shift
        if i % 2 == 0:
            out, nk, nv, vb = even_mixer(h, pos0, a_k[j], a_v[j], w_in_e[j], w_out_e[j], a_sinks[j],
                                         b_ln_g[j], b_ln_b[j], b_ws[j], b_bias[j])
            nak.append(nk)
            nav.append(nv)
            nbv.append(vb)
        else:
            out, nc, nd = odd_mixer(h, pos0, c_st[j], d_st[j], w_in_o[j], w_out_o[j], c_conv_w[j],
                                    c_conv_b[j], c_ln_g[j], c_ln_b[j], d_w[j], d_b[j], d_scale[j])
            ncc.append(nc)
            ndp.append(nd)
        x = x + gate * rms_norm(out, norm_g[i, 1])
        shift, scale, gate = ada_mod(c, ada_w[i, 1], ada_b[i, 1])
        h = rms_norm(x, norm_g[i, 2]) * (1 + scale) + shift
        out, nf = conv_glu_ffn(f_st[i], h, ffn_w_up[i], ffn_conv_w[i], ffn_conv_b[i], ffn_w_down[i])
        nff.append(nf)
        x = x + gate * rms_norm(out, norm_g[i, 3])
    return (x, jnp.stack(nak), jnp.stack(nav), jnp.stack(nbv), jnp.stack(ncc),
            jnp.stack(ndp), jnp.stack(nff))


def setup_inputs(seed: int = 0) -> dict:
    key = jax.random.key(seed)
    ks = jax.random.split(key, 32)

    def nrm(k, shape, s):
        return jax.random.normal(k, shape, jnp.float32) * s

    win = min(WINDOW, PAST_LEN)
    return {
        'x_prompt': nrm(ks[0], (BATCH, SEQ, D_MODEL), 1.0),
        'x_sample': nrm(ks[1], (DEC_BATCH, DEC_SEQ, D_MODEL), 1.0),
        'cache_a_k': nrm(ks[2], (N_EVEN, DEC_BATCH, win, A_KV_HEADS, A_HEAD_DIM), 1.0),
        'cache_a_v': nrm(ks[3], (N_EVEN, DEC_BATCH, win, A_KV_HEADS, A_HEAD_DIM), 1.0),
        'state_c_conv': nrm(ks[4], (N_ODD, DEC_BATCH, C_CONV_WIDTH - 1, C_WIDTH), 0.5),
        'state_d_pool': nrm(ks[5], (N_ODD, DEC_BATCH, POOL_MAX - 1, D_WIDTH), 1.0),
        'state_ffn_conv': nrm(ks[6], (DEPTH, DEC_BATCH, FFN_CONV_WIDTH - 1, 2 * D_FF), 1.0),
        'c_prompt': nrm(ks[7], (BATCH, D_MODEL), 1.0),
        'c_sample': nrm(ks[8], (DEC_BATCH, D_MODEL), 1.0),
        'norm_g': 1.0 + nrm(ks[9], (DEPTH, 4, D_MODEL), 0.1),
        'ada_w': nrm(ks[10], (DEPTH, 2, D_MODEL, 3 * D_MODEL), 0.5 * D_MODEL ** -0.5),
        'ada_b': nrm(ks[11], (DEPTH, 2, 3 * D_MODEL), 0.02),
        'w_in_e': nrm(ks[12], (N_EVEN, D_MODEL, EVEN_IN), D_MODEL ** -0.5),
        'w_out_e': nrm(ks[13], (N_EVEN, MIX_WIDTH, D_MODEL), MIX_WIDTH ** -0.5),
        'a_sinks': nrm(ks[14], (N_EVEN, A_HEADS), 0.5),
        'b_ln_g': 1.0 + nrm(ks[15], (N_EVEN, B_WIDTH), 0.1),
        'b_ln_b': nrm(ks[16], (N_EVEN, B_WIDTH), 0.02),
        'b_ws': nrm(ks[17], (N_EVEN, B_HEADS, CHUNK, CHUNK), CHUNK ** -0.5),
        'b_bias': 1.0 + nrm(ks[18], (N_EVEN, B_HEADS, CHUNK), 0.1),
        'w_in_o': nrm(ks[19], (N_ODD, D_MODEL, ODD_IN), D_MODEL ** -0.5),
        'w_out_o': nrm(ks[20], (N_ODD, MIX_WIDTH, D_MODEL), MIX_WIDTH ** -0.5),
        'c_conv_w': nrm(ks[21], (N_ODD, C_CONV_WIDTH, C_WIDTH), C_CONV_WIDTH ** -0.5),
        'c_conv_b': nrm(ks[22], (N_ODD, C_WIDTH), 0.02),
        'c_ln_g': 1.0 + nrm(ks[23], (N_ODD, C_WIDTH), 0.1),
        'c_ln_b': nrm(ks[24], (N_ODD, C_WIDTH), 0.02),
        'd_w': nrm(ks[25], (N_ODD, D_GROUPS, D_GROUP_WIDTH, D_GROUP_WIDTH), D_GROUP_WIDTH ** -0.5),
        'd_b': nrm(ks[26], (N_ODD, D_GROUPS, D_GROUP_WIDTH), 0.02),
        'd_scale': 1.0 + nrm(ks[27], (N_ODD, D_WIDTH), 0.1),
        'ffn_w_up': nrm(ks[28], (DEPTH, D_MODEL, 2 * D_FF), D_MODEL ** -0.5),
        'ffn_conv_w': nrm(ks[29], (DEPTH, FFN_CONV_WIDTH, 2 * D_FF), FFN_CONV_WIDTH ** -0.5),
        'ffn_conv_b': nrm(ks[30], (DEPTH, 2 * D_FF), 0.02),
        'ffn_w_down': nrm(ks[31], (DEPTH, D_FF, D_MODEL), D_FF ** -0.5),
    }


def reference(x_prompt, x_sample, cache_a_k, cache_a_v, state_c_conv, state_d_pool, state_ffn_conv,
              c_prompt, c_sample, norm_g, ada_w, ada_b, w_in_e, w_out_e, a_sinks, b_ln_g, b_ln_b,
              b_ws, b_bias, w_in_o, w_out_o, c_conv_w, c_conv_b, c_ln_g, c_ln_b, d_w, d_b, d_scale,
              ffn_w_up, ffn_conv_w, ffn_conv_b, ffn_w_down):
    weights = (norm_g, ada_w, ada_b, w_in_e, w_out_e, a_sinks, b_ln_g, b_ln_b, b_ws, b_bias,
               w_in_o, w_out_o, c_conv_w, c_conv_b, c_ln_g, c_ln_b, d_w, d_b, d_scale,
               ffn_w_up, ffn_conv_w, ffn_conv_b, ffn_w_down)
    nbp = x_prompt.shape[0]
    dt = x_prompt.dtype
    zero_ak = jnp.zeros((N_EVEN, nbp, WINDOW, A_KV_HEADS, A_HEAD_DIM), dt)
    zero_c = jnp.zeros((N_ODD, nbp, C_CONV_WIDTH - 1, C_WIDTH), dt)
    zero_d = jnp.zeros((N_ODD, nbp, POOL_MAX - 1, D_WIDTH), dt)
    zero_f = jnp.zeros((DEPTH, nbp, FFN_CONV_WIDTH - 1, 2 * D_FF), dt)
    y_prompt, ak_p, av_p, _, cc_p, dp_p, ff_p = run_group(
        x_prompt, c_prompt, 0, zero_ak, zero_ak, zero_c, zero_d, zero_f, *weights)
    y_sample, ak_s, av_s, bv_s, cc_s, dp_s, ff_s = run_group(
        x_sample, c_sample, PAST_LEN, cache_a_k, cache_a_v, state_c_conv, state_d_pool,
        state_ffn_conv, *weights)
    return (y_prompt, y_sample, ak_p, ak_s, av_p, av_s, bv_s, cc_p, cc_s, dp_p, dp_s, ff_p, ff_s)
```

```python
import functools

import jax
import jax.numpy as jnp
from jax import lax
from jax.experimental import pallas as pl
from jax.experimental.pallas import tpu as pltpu

D_MODEL = 2048
BATCH = 2
SEQ = 4096
DEPTH = 2
DEC_BATCH = 128
DEC_SEQ = 4
PAST_LEN = 8192
A_HEADS = 16
A_KV_HEADS = 4
A_HEAD_DIM = 64
A_Q_WIDTH = A_HEADS * A_HEAD_DIM
A_KV_WIDTH = A_KV_HEADS * A_HEAD_DIM
WINDOW = 128
ROT_DIM = A_HEAD_DIM // 4
ROPE_THETA = 500000.0
B_HEADS = 8
B_HEAD_DIM = 128
B_WIDTH = B_HEADS * B_HEAD_DIM
CHUNK = 128
C_WIDTH = 1024
C_CONV_WIDTH = 31
POOL_SIZES = (2, 4, 8, 16)
D_GROUPS = len(POOL_SIZES)
POOL_MAX = max(POOL_SIZES)
D_WIDTH = 1024
D_GROUP_WIDTH = D_WIDTH // D_GROUPS
D_FF = 5632
FFN_CONV_WIDTH = 3
EVEN_IN = A_Q_WIDTH + 2 * A_KV_WIDTH + 2 * B_WIDTH
ODD_IN = 2 * C_WIDTH + D_WIDTH
EPS = 1e-6
NEG_INF = -1e30

LANES = 128
SUBLANES = 8
VMEM_LIMIT = 56 * 1024 * 1024

BF16 = jnp.bfloat16
F32 = jnp.float32

SEQ_BLOCK = 32
C_TAIL = 32
D_TAIL = 16


def _params(n_axes):
    return pltpu.CompilerParams(dimension_semantics=("arbitrary",) * n_axes,
                                vmem_limit_bytes=VMEM_LIMIT)


def _gelu_tanh(x):
    return 0.5 * x * (1.0 + jnp.tanh(0.7978845608028654 * (x + 0.044715 * (x * x * x))))


def _rms(x, g):
    return x * lax.rsqrt(jnp.mean(x * x, axis=-1, keepdims=True) + EPS) * g


def _layer_norm(x, g, b):
    mu = jnp.mean(x, axis=-1, keepdims=True)
    xc = x - mu
    var = jnp.mean(xc * xc, axis=-1, keepdims=True)
    return xc * lax.rsqrt(var + EPS) * g + b


def _rows(m, tm):
    r = m.shape[0]
    if r == 1 or r == tm:
        return m
    return jnp.concatenate([m] * (tm // r), axis=0)


def _rope(x, c, a, b):
    return x * c + pltpu.roll(x, LANES - ROT_DIM // 2, 1) * a + pltpu.roll(x, ROT_DIM // 2, 1) * b


def _ada_kernel(c_ref, w_ref, b_ref, o_ref):
    c = c_ref[...]
    s = (c * jax.nn.sigmoid(c)).astype(BF16)
    o_ref[...] = jnp.dot(s, w_ref[...].astype(BF16), preferred_element_type=F32) + b_ref[...]


def _ada(c_all, w, b):
    r = c_all.shape[0]
    tn = 1024
    n_sub, _, n = w.shape
    return pl.pallas_call(
        _ada_kernel,
        grid=(n_sub, n // tn),
        in_specs=[pl.BlockSpec((r, D_MODEL), lambda k, j: (0, 0)),
                  pl.BlockSpec((None, D_MODEL, tn), lambda k, j: (k, 0, j)),
                  pl.BlockSpec((None, 1, tn), lambda k, j: (k, 0, j))],
        out_specs=pl.BlockSpec((None, r, tn), lambda k, j: (k, 0, j)),
        out_shape=jax.ShapeDtypeStruct((n_sub, r, n), F32),
        compiler_params=_params(2),
    )(c_all, w, b)


def _resnorm_kernel(*refs, has_res, has_next):
    it = iter(refs)
    x_ref = next(it)
    if has_res:
        o_ref, gate_ref, gpost_ref = next(it), next(it), next(it)
    if has_next:
        gpre_ref, scale_ref, shift_ref = next(it), next(it), next(it)
    if has_res:
        xo_ref = next(it)
    if has_next:
        ho_ref = next(it)
    x = x_ref[...]
    tm = x.shape[0]
    if has_res:
        x = x + _rows(gate_ref[...], tm) * _rms(o_ref[...], gpost_ref[...])
        xo_ref[...] = x
    if has_next:
        h = _rms(x, gpre_ref[...]) * (1.0 + _rows(scale_ref[...], tm)) + _rows(shift_ref[...], tm)
        ho_ref[...] = h.astype(BF16)


class _Group:
    def __init__(self, m, tm, mods, per_batch_rows):
        self.m, self.tm, self.mods = m, tm, mods
        self.per_batch_rows = per_batch_rows

    def mod_spec(self, k, col):
        if self.per_batch_rows is None:
            return pl.BlockSpec((None, DEC_BATCH, D_MODEL), lambda i: (k, 0, col))
        tpb = self.per_batch_rows // self.tm
        return pl.BlockSpec((None, None, 1, D_MODEL), lambda i: (k, i // tpb, 0, col))


def _resnorm(grp, x, out, gains, res=None, nxt=None):
    tm = grp.tm
    row = pl.BlockSpec((tm, D_MODEL), lambda i: (i, 0))
    args, specs, out_shapes, out_specs = [x], [row], [], []

    def gain_spec(idx):
        return pl.BlockSpec((None, 1, D_MODEL), lambda i: (idx, 0, 0))

    if res is not None:
        args += [out, grp.mods, gains]
        specs += [row, grp.mod_spec(res[0], 2), gain_spec(res[1])]
        out_shapes.append(jax.ShapeDtypeStruct((grp.m, D_MODEL), F32))
        out_specs.append(row)
    if nxt is not None:
        args += [gains, grp.mods, grp.mods]
        specs += [gain_spec(nxt[1]), grp.mod_spec(nxt[0], 1), grp.mod_spec(nxt[0], 0)]
        out_shapes.append(jax.ShapeDtypeStruct((grp.m, D_MODEL), BF16))
        out_specs.append(row)
    res_out = pl.pallas_call(
        functools.partial(_resnorm_kernel, has_res=res is not None, has_next=nxt is not None),
        grid=(grp.m // tm,),
        in_specs=specs, out_specs=out_specs, out_shape=out_shapes,
        compiler_params=_params(1),
    )(*args)
    return res_out


def _mm_kernel(a_ref, w_ref, o_ref):
    o_ref[...] = jnp.dot(a_ref[...], w_ref[...], preferred_element_type=F32).astype(o_ref.dtype)


def _mm(a, w, tm, tn, out_dtype=F32):
    m, k = a.shape
    n = w.shape[1]
    return pl.pallas_call(
        _mm_kernel,
        grid=(n // tn, m // tm),
        in_specs=[pl.BlockSpec((tm, k), lambda j, i: (i, 0)),
                  pl.BlockSpec((k, tn), lambda j, i: (0, j))],
        out_specs=pl.BlockSpec((tm, tn), lambda j, i: (i, j)),
        out_shape=jax.ShapeDtypeStruct((m, n), out_dtype),
        compiler_params=_params(2),
    )(a, w)


def _even_p_kernel(sink_ref, z_ref, cos_ref, sa_ref, sb_ref, ws_ref, bexp_ref, lng_ref, lnb_ref,
                   mix_ref, nk_ref, nv_ref, kprev, vprev):
    b = pl.program_id(1)
    last = pl.num_programs(1) - 1
    qb = WINDOW
    cos_t, sa_t, sb_t = cos_ref[...], sa_ref[...], sb_ref[...]

    @pl.when(b == 0)
    def _():
        kprev[...] = jnp.zeros_like(kprev)
        vprev[...] = jnp.zeros_like(vprev)

    lane = lax.broadcasted_iota(jnp.int32, (qb, LANES), 1)
    lo = lane < A_HEAD_DIM
    lane2 = lax.broadcasted_iota(jnp.int32, (2 * qb, LANES), 1)
    lo2 = lane2 < A_HEAD_DIM
    row = lax.broadcasted_iota(jnp.int32, (qb, 2 * qb), 0)
    col = lax.broadcasted_iota(jnp.int32, (qb, 2 * qb), 1)
    in_window = ((col > row) & (col < qb)) | ((col >= qb) & (col - qb <= row))
    mask = in_window & (col >= jnp.where(b > 0, 0, qb))
    nt = (((1,), (1,)), ((), ()))

    for jk in range(A_KV_HEADS // 2):
        ksl = slice(A_Q_WIDTH + LANES * jk, A_Q_WIDTH + LANES * (jk + 1))
        vsl = slice(A_Q_WIDTH + A_KV_WIDTH + LANES * jk, A_Q_WIDTH + A_KV_WIDTH + LANES * (jk + 1))
        csl = slice(LANES * jk, LANES * (jk + 1))
        kc = _rope(z_ref[:, ksl], cos_t, sa_t, sb_t)
        vc = z_ref[:, vsl]
        kall = jnp.concatenate([kprev[:, csl], kc], axis=0)
        vall = jnp.concatenate([vprev[:, csl], vc], axis=0)
        krol = pltpu.roll(kall, A_HEAD_DIM, 1)
        vrol = pltpu.roll(vall, A_HEAD_DIM, 1)
        for sub in range(2):
            kvh = 2 * jk + sub
            kd = (jnp.where(lo2, kall, krol) if sub == 0 else jnp.where(lo2, krol, kall)).astype(BF16)
            vd = (jnp.where(lo2, vall, vrol) if sub == 0 else jnp.where(lo2, vrol, vall)).astype(BF16)
            for qs in range(2):
                js = 2 * kvh + qs
                qsl = _rope(z_ref[:, LANES * js:LANES * (js + 1)], cos_t, sa_t, sb_t) * (A_HEAD_DIM ** -0.5)
                outs = []
                for half in range(2):
                    h = 2 * js + half
                    qm = jnp.where(lo if half == 0 else jnp.logical_not(lo), qsl, 0.0).astype(BF16)
                    s = lax.dot_general(qm, kd, nt, preferred_element_type=F32)
                    s = jnp.where(mask, s, NEG_INF)
                    sink = sink_ref[h]
                    m = jnp.maximum(jnp.max(s, axis=-1, keepdims=True), sink)
                    p = jnp.exp(s - m)
                    den = jnp.sum(p, axis=-1, keepdims=True) + jnp.exp(sink - m)
                    o = jnp.dot(p.astype(BF16), vd, preferred_element_type=F32)
                    outs.append(o / den)
                mix_ref[:, LANES * js:LANES * (js + 1)] = jnp.where(lo, outs[0], outs[1]).astype(BF16)
        kprev[:, csl] = kc
        vprev[:, csl] = vc

        @pl.when(b == last)
        def _():
            nk_ref[:, csl] = kc
            nv_ref[:, csl] = vc

    zb0 = A_Q_WIDTH + 2 * A_KV_WIDTH
    u = _gelu_tanh(z_ref[:, zb0:zb0 + B_WIDTH])
    vb = _layer_norm(_gelu_tanh(z_ref[:, zb0 + B_WIDTH:zb0 + 2 * B_WIDTH]), lng_ref[...], lnb_ref[...])
    ri = lax.broadcasted_iota(jnp.int32, (CHUNK, CHUNK), 0)
    ci = lax.broadcasted_iota(jnp.int32, (CHUNK, CHUNK), 1)
    tri = ri >= ci
    for h in range(B_HEADS):
        hs = slice(B_HEAD_DIM * h, B_HEAD_DIM * (h + 1))
        w = jnp.where(tri, ws_ref[h], 0.0).astype(BF16)
        mixed = jnp.dot(w, vb[:, hs].astype(BF16), preferred_element_type=F32) + bexp_ref[:, hs]
        mix_ref[:, A_Q_WIDTH + B_HEAD_DIM * h:A_Q_WIDTH + B_HEAD_DIM * (h + 1)] = (u[:, hs] * mixed).astype(BF16)


def _even_p(z, sinks, tabs, ws, bexp, lng, lnb):
    n, t, _ = z.shape
    qb = WINDOW
    full = lambda shape: pl.BlockSpec(shape, lambda i, b: (0,) * len(shape))
    tab = pl.BlockSpec((qb, LANES), lambda i, b: (b, 0))
    return pl.pallas_call(
        _even_p_kernel,
        grid=(n, t // qb),
        in_specs=[pl.BlockSpec(memory_space=pltpu.SMEM),
                  pl.BlockSpec((None, qb, EVEN_IN), lambda i, b: (i, b, 0)),
                  tab, tab, tab,
                  full((B_HEADS, CHUNK, CHUNK)), full((CHUNK, B_WIDTH)),
                  full((1, B_WIDTH)), full((1, B_WIDTH))],
        out_specs=[pl.BlockSpec((None, qb, D_MODEL), lambda i, b: (i, b, 0)),
                   pl.BlockSpec((None, qb, A_KV_WIDTH), lambda i, b: (i, 0, 0)),
                   pl.BlockSpec((None, qb, A_KV_WIDTH), lambda i, b: (i, 0, 0))],
        out_shape=[jax.ShapeDtypeStruct((n, t, D_MODEL), BF16),
                   jax.ShapeDtypeStruct((n, qb, A_KV_WIDTH), F32),
                   jax.ShapeDtypeStruct((n, qb, A_KV_WIDTH), F32)],
        scratch_shapes=[pltpu.VMEM((qb, A_KV_WIDTH), F32), pltpu.VMEM((qb, A_KV_WIDTH), F32)],
        compiler_params=_params(2),
    )(sinks, z, *tabs, ws, bexp, lng, lnb)


def _even_s_kernel(sinkrow_ref, z_ref, ck_ref, cv_ref, cos_ref, sa_ref, sb_ref, wexp_ref, bexp_ref,
                   lng_ref, lnb_ref, mix_ref, ok_ref, ov_ref, vb_ref, qm, osc, kn, vn):
    s_blk = SEQ_BLOCK
    rows_per_seq = A_HEADS * DEC_SEQ
    lane256 = lax.broadcasted_iota(jnp.int32, (s_blk, A_KV_WIDTH), 1)
    kn[...] = jnp.zeros_like(kn)
    vn[...] = jnp.zeros_like(vn)
    halves = A_KV_WIDTH // LANES

    def put(ref, start, stride, val):
        for c in range(halves):
            ref[c, pl.ds(start, s_blk, stride=stride), :] = val[:, LANES * c:LANES * (c + 1)]

    def get_rows(ref, start, size):
        return jnp.concatenate([ref[c, pl.ds(start, size), :] for c in range(halves)], axis=1)

    def get_strided(ref, start, stride):
        return jnp.concatenate([ref[c, pl.ds(start, s_blk, stride=stride), :] for c in range(halves)], axis=1)

    for t in range(DEC_SEQ):
        c, a, bb = cos_ref[t:t + 1, :], sa_ref[t:t + 1, :], sb_ref[t:t + 1, :]

        def rope2(lo_lane):
            return jnp.concatenate([_rope(z_ref[t, :, lo_lane:lo_lane + LANES], c, a, bb),
                                    _rope(z_ref[t, :, lo_lane + LANES:lo_lane + 2 * LANES], c, a, bb)], axis=1)

        put(kn, t, SUBLANES, rope2(A_Q_WIDTH))
        put(vn, t, SUBLANES, z_ref[t, :, A_Q_WIDTH + A_KV_WIDTH:A_Q_WIDTH + 2 * A_KV_WIDTH])
        for j in range(A_KV_HEADS):
            qs = rope2(A_KV_WIDTH * j) * (A_HEAD_DIM ** -0.5)
            keep = (lane256 >= A_HEAD_DIM * j) & (lane256 < A_HEAD_DIM * (j + 1))
            for g in range(A_HEADS // A_KV_HEADS):
                sh = ((j - g) * A_HEAD_DIM) % A_KV_WIDTH
                qr = qs if sh == 0 else pltpu.roll(qs, sh, 1)
                r = (4 * j + g) * DEC_SEQ + t
                put(qm, r, rows_per_seq, jnp.where(keep, qr, 0.0))

    rr = lax.broadcasted_iota(jnp.int32, (rows_per_seq, WINDOW), 0) % DEC_SEQ
    cc = lax.broadcasted_iota(jnp.int32, (rows_per_seq, WINDOW), 1)
    mask_old = cc > rr
    new0 = WINDOW - DEC_SEQ
    mask_new = (cc >= new0) & (cc - new0 <= rr)
    row8 = lax.broadcasted_iota(jnp.int32, (SUBLANES, A_KV_WIDTH), 0)
    sink = sinkrow_ref[...]
    nt = (((1,), (1,)), ((), ()))

    def shifted(c_ref, new_ref, o_ref, n):
        base = pl.multiple_of(n * WINDOW, WINDOW)
        old = c_ref[pl.ds(base, WINDOW), :]
        rolled = pltpu.roll(old, WINDOW - DEC_SEQ, 0)
        fresh = get_rows(new_ref, pl.multiple_of(n * SUBLANES, SUBLANES), SUBLANES)
        tail = jnp.where(row8 >= SUBLANES - DEC_SEQ, pltpu.roll(fresh, SUBLANES - DEC_SEQ, 0),
                         rolled[WINDOW - SUBLANES:, :])
        new = jnp.concatenate([rolled[:WINDOW - SUBLANES, :], tail], axis=0)
        o_ref[pl.ds(base, WINDOW), :] = new
        return old.astype(BF16), new.astype(BF16)

    def body(n, carry):
        k_old, k_new = shifted(ck_ref, kn, ok_ref, n)
        v_old, v_new = shifted(cv_ref, vn, ov_ref, n)
        q = get_rows(qm, pl.multiple_of(n * rows_per_seq, rows_per_seq), rows_per_seq).astype(BF16)
        s_old = jnp.where(mask_old, lax.dot_general(q, k_old, nt, preferred_element_type=F32), NEG_INF)
        s_new = jnp.where(mask_new, lax.dot_general(q, k_new, nt, preferred_element_type=F32), NEG_INF)
        m = jnp.maximum(jnp.maximum(jnp.max(s_old, axis=-1, keepdims=True),
                                    jnp.max(s_new, axis=-1, keepdims=True)), sink)
        p_old = jnp.exp(s_old - m)
        p_new = jnp.exp(s_new - m)
        den = (jnp.sum(p_old, axis=-1, keepdims=True) + jnp.sum(p_new, axis=-1, keepdims=True)
               + jnp.exp(sink - m))
        o = (jnp.dot(p_old.astype(BF16), v_old, preferred_element_type=F32)
             + jnp.dot(p_new.astype(BF16), v_new, preferred_element_type=F32))
        o = o / den
        for c in range(halves):
            osc[c, pl.ds(pl.multiple_of(n * rows_per_seq, rows_per_seq), rows_per_seq), :] = o[:, LANES * c:LANES * (c + 1)]
        return carry

    lax.fori_loop(0, s_blk, body, 0)

    for t in range(DEC_SEQ):
        for j in range(A_KV_HEADS):
            acc = jnp.zeros((s_blk, A_KV_WIDTH), F32)
            for g in range(A_HEADS // A_KV_HEADS):
                r = (4 * j + g) * DEC_SEQ + t
                ov = get_strided(osc, r, rows_per_seq)
                sh = ((g - j) * A_HEAD_DIM) % A_KV_WIDTH
                orr = ov if sh == 0 else pltpu.roll(ov, sh, 1)
                acc = jnp.where((lane256 >= A_HEAD_DIM * g) & (lane256 < A_HEAD_DIM * (g + 1)), orr, acc)
            mix_ref[t, :, A_KV_WIDTH * j:A_KV_WIDTH * (j + 1)] = acc.astype(BF16)

    zb0 = A_Q_WIDTH + 2 * A_KV_WIDTH
    us, vbs = [], []
    for t in range(DEC_SEQ):
        us.append(_gelu_tanh(z_ref[t, :, zb0:zb0 + B_WIDTH]))
        v = _layer_norm(_gelu_tanh(z_ref[t, :, zb0 + B_WIDTH:zb0 + 2 * B_WIDTH]), lng_ref[...], lnb_ref[...])
        vb_ref[t] = v
        vbs.append(v)
    for t in range(DEC_SEQ):
        mixed = bexp_ref[t:t + 1, :]
        for s in range(t + 1):
            mixed = mixed + wexp_ref[DEC_SEQ * t + s:DEC_SEQ * t + s + 1, :] * vbs[s]
        mix_ref[t, :, A_Q_WIDTH:A_Q_WIDTH + B_WIDTH] = (us[t] * mixed).astype(BF16)


def _even_s(z3, ck, cv, sinkrow, tabs, wexp, bexp, lng, lnb):
    s_blk = SEQ_BLOCK
    rows_per_seq = A_HEADS * DEC_SEQ
    full = lambda shape: pl.BlockSpec(shape, lambda s: (0,) * len(shape))
    cache = pl.BlockSpec((s_blk * WINDOW, A_KV_WIDTH), lambda s: (s, 0))
    return pl.pallas_call(
        _even_s_kernel,
        grid=(DEC_BATCH // s_blk,),
        in_specs=[full((rows_per_seq, 1)),
                  pl.BlockSpec((DEC_SEQ, s_blk, EVEN_IN), lambda s: (0, s, 0)),
                  cache, cache,
                  full((SUBLANES, LANES)), full((SUBLANES, LANES)), full((SUBLANES, LANES)),
                  full((DEC_SEQ * DEC_SEQ, B_WIDTH)), full((SUBLANES, B_WIDTH)),
                  full((1, B_WIDTH)), full((1, B_WIDTH))],
        out_specs=[pl.BlockSpec((DEC_SEQ, s_blk, D_MODEL), lambda s: (0, s, 0)),
                   cache, cache,
                   pl.BlockSpec((DEC_SEQ, s_blk, B_WIDTH), lambda s: (0, s, 0))],
        out_shape=[jax.ShapeDtypeStruct((DEC_SEQ, DEC_BATCH, D_MODEL), BF16),
                   jax.ShapeDtypeStruct((DEC_BATCH * WINDOW, A_KV_WIDTH), F32),
                   jax.ShapeDtypeStruct((DEC_BATCH * WINDOW, A_KV_WIDTH), F32),
                   jax.ShapeDtypeStruct((DEC_SEQ, DEC_BATCH, B_WIDTH), F32)],
        scratch_shapes=[pltpu.VMEM((A_KV_WIDTH // LANES, s_blk * rows_per_seq, LANES), F32),
                        pltpu.VMEM((A_KV_WIDTH // LANES, s_blk * rows_per_seq, LANES), F32),
                        pltpu.VMEM((A_KV_WIDTH // LANES, s_blk * SUBLANES, LANES), F32),
                        pltpu.VMEM((A_KV_WIDTH // LANES, s_blk * SUBLANES, LANES), F32)],
        compiler_params=_params(1),
    )(sinkrow, z3, ck, cv, *tabs, wexp, bexp, lng, lnb)


ODD_ROWS = 256
ODD_RC = 64


def _odd_p_kernel(z_ref, cw_ref, cb_ref, lng_ref, lnb_ref, dw_ref, db_ref, dsc_ref,
                  mix_ref, ctail_ref, dtail_ref, gext, dext, cbuf, dbuf):
    b = pl.program_id(1)
    last = pl.num_programs(1) - 1
    tr = ODD_ROWS

    @pl.when(b == 0)
    def _():
        gext[0:C_TAIL, :] = jnp.zeros((C_TAIL, C_WIDTH), F32)
        dext[0:D_TAIL, :] = jnp.zeros((D_TAIL, D_WIDTH), F32)

    gext[C_TAIL:C_TAIL + tr, :] = z_ref[:, 0:C_WIDTH] * jax.nn.sigmoid(z_ref[:, C_WIDTH:2 * C_WIDTH])
    dext[D_TAIL:D_TAIL + tr, :] = z_ref[:, 2 * C_WIDTH:2 * C_WIDTH + D_WIDTH]

    gw = D_GROUP_WIDTH
    lead = C_TAIL - (C_CONV_WIDTH - 1)
    for rc in range(tr // ODD_RC):
        r0 = rc * ODD_RC
        for c in range(C_WIDTH // gw):
            ls = slice(gw * c, gw * (c + 1))
            acc = jnp.broadcast_to(cb_ref[:, ls], (ODD_RC, gw))
            for k in range(C_CONV_WIDTH):
                acc = acc + cw_ref[k:k + 1, ls] * gext[r0 + lead + k:r0 + lead + k + ODD_RC, ls]
            cbuf[r0:r0 + ODD_RC, ls] = acc
        y = _layer_norm(cbuf[r0:r0 + ODD_RC, :], lng_ref[...], lnb_ref[...])
        mix_ref[r0:r0 + ODD_RC, 0:C_WIDTH] = (y * jax.nn.sigmoid(y)).astype(BF16)

        pos1 = (b * tr + r0 + 1 + lax.broadcasted_iota(jnp.int32, (ODD_RC, gw), 0))
        for g, wsz in enumerate(POOL_SIZES):
            ls = slice(gw * g, gw * (g + 1))
            win = dext[D_TAIL + r0:D_TAIL + r0 + ODD_RC, ls]
            for jj in range(1, wsz):
                win = win + dext[D_TAIL + r0 - jj:D_TAIL + r0 - jj + ODD_RC, ls]
            cnt = jnp.minimum(pos1, wsz).astype(F32)
            diff = win / cnt - dext[D_TAIL + r0:D_TAIL + r0 + ODD_RC, ls]
            dbuf[r0:r0 + ODD_RC, ls] = diff.astype(BF16)

    for g in range(D_GROUPS):
        ls = slice(gw * g, gw * (g + 1))
        o = jnp.dot(dbuf[:, ls], dw_ref[g].astype(BF16), preferred_element_type=F32) + db_ref[:, ls]
        mix_ref[:, C_WIDTH + gw * g:C_WIDTH + gw * (g + 1)] = (o * dsc_ref[:, ls]).astype(BF16)

    gext[0:C_TAIL, :] = gext[tr:tr + C_TAIL, :]
    dext[0:D_TAIL, :] = dext[tr:tr + D_TAIL, :]

    @pl.when(b == last)
    def _():
        ctail_ref[...] = gext[0:C_TAIL, :]
        dtail_ref[...] = dext[0:D_TAIL, :]


def _odd_p(z, cw, cb, lng, lnb, dw, db, dsc):
    n, t, _ = z.shape
    tr = ODD_ROWS
    full = lambda shape: pl.BlockSpec(shape, lambda i, b: (0,) * len(shape))
    return pl.pallas_call(
        _odd_p_kernel,
        grid=(n, t // tr),
        in_specs=[pl.BlockSpec((None, tr, ODD_IN), lambda i, b: (i, b, 0)),
                  full((C_CONV_WIDTH, C_WIDTH)), full((1, C_WIDTH)), full((1, C_WIDTH)), full((1, C_WIDTH)),
                  full((D_GROUPS, D_GROUP_WIDTH, D_GROUP_WIDTH)), full((1, D_WIDTH)), full((1, D_WIDTH))],
        out_specs=[pl.BlockSpec((None, tr, D_MODEL), lambda i, b: (i, b, 0)),
                   pl.BlockSpec((None, C_TAIL, C_WIDTH), lambda i, b: (i, 0, 0)),
                   pl.BlockSpec((None, D_TAIL, D_WIDTH), lambda i, b: (i, 0, 0))],
        out_shape=[jax.ShapeDtypeStruct((n, t, D_MODEL), BF16),
                   jax.ShapeDtypeStruct((n, C_TAIL, C_WIDTH), F32),
                   jax.ShapeDtypeStruct((n, D_TAIL, D_WIDTH), F32)],
        scratch_shapes=[pltpu.VMEM((C_TAIL + tr, C_WIDTH), F32), pltpu.VMEM((D_TAIL + tr, D_WIDTH), F32),
                        pltpu.VMEM((tr, C_WIDTH), F32), pltpu.VMEM((tr, D_WIDTH), BF16)],
        compiler_params=_params(2),
    )(z, cw, cb, lng, lnb, dw, db, dsc)


def _odd_s_kernel(z_ref, cs_ref, ds_ref, cw_ref, cb_ref, lng_ref, lnb_ref, dw_ref, db_ref, dsc_ref,
                  mix_ref, co_ref, do_ref, cbuf, dbuf):
    s_blk = SEQ_BLOCK
    gw = D_GROUP_WIDTH
    nc, nd = C_CONV_WIDTH - 1, POOL_MAX - 1
    for c in range(C_WIDTH // gw):
        ls = slice(gw * c, gw * (c + 1))
        accs = [jnp.broadcast_to(cb_ref[:, ls], (s_blk, gw)) for _ in range(DEC_SEQ)]
        for j in range(nc + DEC_SEQ):
            if j < nc:
                e = cs_ref[j, :, ls]
            else:
                t = j - nc
                e = z_ref[t, :, gw * c:gw * (c + 1)] * jax.nn.sigmoid(
                    z_ref[t, :, C_WIDTH + gw * c:C_WIDTH + gw * (c + 1)])
            if j >= DEC_SEQ:
                co_ref[j - DEC_SEQ, :, ls] = e
            for t in range(DEC_SEQ):
                k = j - t
                if 0 <= k < C_CONV_WIDTH:
                    accs[t] = accs[t] + cw_ref[k:k + 1, ls] * e
        for t in range(DEC_SEQ):
            cbuf[t, :, ls] = accs[t]
    for t in range(DEC_SEQ):
        y = _layer_norm(cbuf[t], lng_ref[...], lnb_ref[...])
        mix_ref[t, :, 0:C_WIDTH] = (y * jax.nn.sigmoid(y)).astype(BF16)

    for g, wsz in enumerate(POOL_SIZES):
        ls = slice(gw * g, gw * (g + 1))
        ext = []
        for j in range(nd + DEC_SEQ):
            if j < nd:
                e = ds_ref[j, :, ls]
            else:
                e = z_ref[j - nd, :, 2 * C_WIDTH + gw * g:2 * C_WIDTH + gw * (g + 1)]
            if j >= DEC_SEQ:
                do_ref[j - DEC_SEQ, :, ls] = e
            ext.append(e)
        for t in range(DEC_SEQ):
            win = ext[nd + t]
            for jj in range(1, wsz):
                win = win + ext[nd + t - jj]
            cnt = float(min(PAST_LEN + t + 1, wsz))
            dbuf[s_blk * t:s_blk * (t + 1), ls] = (win / cnt - ext[nd + t]).astype(BF16)
    for g in range(D_GROUPS):
        ls = slice(gw * g, gw * (g + 1))
        o = jnp.dot(dbuf[:, ls], dw_ref[g].astype(BF16), preferred_element_type=F32) + db_ref[:, ls]
        o = (o * dsc_ref[:, ls]).astype(BF16)
        for t in range(DEC_SEQ):
            mix_ref[t, :, C_WIDTH + gw * g:C_WIDTH + gw * (g + 1)] = o[s_blk * t:s_blk * (t + 1), :]


def _odd_s(z3, cs, ds, cw, cb, lng, lnb, dw, db, dsc):
    s_blk = SEQ_BLOCK
    nc, nd = C_CONV_WIDTH - 1, POOL_MAX - 1
    full = lambda shape: pl.BlockSpec(shape, lambda s: (0,) * len(shape))
    cst = pl.BlockSpec((nc, s_blk, C_WIDTH), lambda s: (0, s, 0))
    dst = pl.BlockSpec((nd, s_blk, D_WIDTH), lambda s: (0, s, 0))
    return pl.pallas_call(
        _odd_s_kernel,
        grid=(DEC_BATCH // s_blk,),
        in_specs=[pl.BlockSpec((DEC_SEQ, s_blk, ODD_IN), lambda s: (0, s, 0)), cst, dst,
                  full((C_CONV_WIDTH, C_WIDTH)), full((1, C_WIDTH)), full((1, C_WIDTH)), full((1, C_WIDTH)),
                  full((D_GROUPS, D_GROUP_WIDTH, D_GROUP_WIDTH)), full((1, D_WIDTH)), full((1, D_WIDTH))],
        out_specs=[pl.BlockSpec((DEC_SEQ, s_blk, D_MODEL), lambda s: (0, s, 0)), cst, dst],
        out_shape=[jax.ShapeDtypeStruct((DEC_SEQ, DEC_BATCH, D_MODEL), BF16),
                   jax.ShapeDtypeStruct((nc, DEC_BATCH, C_WIDTH), F32),
                   jax.ShapeDtypeStruct((nd, DEC_BATCH, D_WIDTH), F32)],
        scratch_shapes=[pltpu.VMEM((DEC_SEQ, s_blk, C_WIDTH), F32),
                        pltpu.VMEM((DEC_SEQ * s_blk, D_WIDTH), BF16)],
        compiler_params=_params(1),
    )(z3, cs, ds, cw, cb, lng, lnb, dw, db, dsc)


FFN_TF = 512
FFN_TM = 1024


def _ffn_p_kernel(h_ref, wg_ref, wv_ref, wd_ref, cwg_ref, cwv_ref, cbg_ref, cbv_ref,
                  o_ref, nfg_ref, nfv_ref, carry_g, carry_v):
    i, f = pl.program_id(1), pl.program_id(2)
    tm = FFN_TM
    h = h_ref[...]
    row = lax.broadcasted_iota(jnp.int32, (tm, FFN_TF), 0)

    @pl.when(i == 0)
    def _():
        carry_g[f] = jnp.zeros((SUBLANES, FFN_TF), F32)
        carry_v[f] = jnp.zeros((SUBLANES, FFN_TF), F32)

    def conv(w_ref, cw_ref, cb_ref, carry, nf_ref):
        up = jnp.dot(h, w_ref[...], preferred_element_type=F32)
        prev = carry[f]
        p1 = jnp.broadcast_to(prev[SUBLANES - 1:SUBLANES, :], (tm, FFN_TF))
        p2 = jnp.broadcast_to(prev[SUBLANES - 2:SUBLANES - 1, :], (tm, FFN_TF))
        m1 = jnp.where(row == 0, p1, pltpu.roll(up, 1, 0))
        m2 = jnp.where(row == 0, p2, jnp.where(row == 1, p1, pltpu.roll(up, 2, 0)))
        tail = up[tm - SUBLANES:, :]
        carry[f] = tail
        nf_ref[...] = tail
        return cw_ref[2:3, :] * up + cw_ref[1:2, :] * m1 + cw_ref[0:1, :] * m2 + cb_ref[...]

    g = conv(wg_ref, cwg_ref, cbg_ref, carry_g, nfg_ref)
    v = conv(wv_ref, cwv_ref, cbv_ref, carry_v, nfv_ref)
    act = (_gelu_tanh(g) * v).astype(BF16)
    part = jnp.dot(act, wd_ref[...], preferred_element_type=F32)

    @pl.when(f == 0)
    def _():
        o_ref[...] = part

    @pl.when(f > 0)
    def _():
        o_ref[...] += part


def _ffn_p(h, n, t, w_up, w_down, cw, cb):
    tm, tf = FFN_TM, FFN_TF
    nf = D_FF // tf
    tpb = t // tm
    return pl.pallas_call(
        _ffn_p_kernel,
        grid=(n, tpb, nf),
        in_specs=[pl.BlockSpec((tm, D_MODEL), lambda b, i, f: (b * tpb + i, 0)),
                  pl.BlockSpec((D_MODEL, tf), lambda b, i, f: (0, f)),
                  pl.BlockSpec((D_MODEL, tf), lambda b, i, f: (0, nf + f)),
                  pl.BlockSpec((tf, D_MODEL), lambda b, i, f: (f, 0)),
                  pl.BlockSpec((FFN_CONV_WIDTH, tf), lambda b, i, f: (0, f)),
                  pl.BlockSpec((FFN_CONV_WIDTH, tf), lambda b, i, f: (0, nf + f)),
                  pl.BlockSpec((1, tf), lambda b, i, f: (0, f)),
                  pl.BlockSpec((1, tf), lambda b, i, f: (0, nf + f))],
        out_specs=[pl.BlockSpec((tm, D_MODEL), lambda b, i, f: (b * tpb + i, 0)),
                   pl.BlockSpec((None, SUBLANES, tf), lambda b, i, f: (b, 0, f)),
                   pl.BlockSpec((None, SUBLANES, tf), lambda b, i, f: (b, 0, f))],
        out_shape=[jax.ShapeDtypeStruct((n * t, D_MODEL), F32),
                   jax.ShapeDtypeStruct((n, SUBLANES, D_FF), F32),
                   jax.ShapeDtypeStruct((n, SUBLANES, D_FF), F32)],
        scratch_shapes=[pltpu.VMEM((nf, SUBLANES, tf), F32), pltpu.VMEM((nf, SUBLANES, tf), F32)],
        compiler_params=_params(3),
    )(h, w_up, w_up, w_down, cw, cw, cb, cb)


def _ffn_s_kernel(h_ref, wg_ref, wv_ref, wd_ref, cwg_ref, cwv_ref, cbg_ref, cbv_ref, sg_ref, sv_ref,
                  o_ref, nsg_ref, nsv_ref):
    f = pl.program_id(0)
    nb = DEC_BATCH
    h = h_ref[...]
    nst = FFN_CONV_WIDTH - 1

    def conv(w_ref, cw_ref, cb_ref, st_ref, ns_ref):
        up = jnp.dot(h, w_ref[...], preferred_element_type=F32)
        ext = [st_ref[j] for j in range(nst)]
        ext += [up[nb * t:nb * (t + 1), :] for t in range(DEC_SEQ)]
        for j in range(nst):
            ns_ref[j] = ext[DEC_SEQ + j]
        return [cw_ref[0:1, :] * ext[t] + cw_ref[1:2, :] * ext[t + 1] + cw_ref[2:3, :] * ext[t + 2] + cb_ref[...]
                for t in range(DEC_SEQ)]

    g = conv(wg_ref, cwg_ref, cbg_ref, sg_ref, nsg_ref)
    v = conv(wv_ref, cwv_ref, cbv_ref, sv_ref, nsv_ref)
    act = jnp.concatenate([(_gelu_tanh(g[t]) * v[t]).astype(BF16) for t in range(DEC_SEQ)], axis=0)
    part = jnp.dot(act, wd_ref[...], preferred_element_type=F32)

    @pl.when(f == 0)
    def _():
        o_ref[...] = part

    @pl.when(f > 0)
    def _():
        o_ref[...] += part


def _ffn_s(h, state, w_up, w_down, cw, cb):
    tf = FFN_TF
    nf = D_FF // tf
    m = h.shape[0]
    nst = FFN_CONV_WIDTH - 1
    return pl.pallas_call(
        _ffn_s_kernel,
        grid=(nf,),
        in_specs=[pl.BlockSpec((m, D_MODEL), lambda f: (0, 0)),
                  pl.BlockSpec((D_MODEL, tf), lambda f: (0, f)),
                  pl.BlockSpec((D_MODEL, tf), lambda f: (0, nf + f)),
                  pl.BlockSpec((tf, D_MODEL), lambda f: (f, 0)),
                  pl.BlockSpec((FFN_CONV_WIDTH, tf), lambda f: (0, f)),
                  pl.BlockSpec((FFN_CONV_WIDTH, tf), lambda f: (0, nf + f)),
                  pl.BlockSpec((1, tf), lambda f: (0, f)),
                  pl.BlockSpec((1, tf), lambda f: (0, nf + f)),
                  pl.BlockSpec((nst, DEC_BATCH, tf), lambda f: (0, 0, f)),
                  pl.BlockSpec((nst, DEC_BATCH, tf), lambda f: (0, 0, nf + f))],
        out_specs=[pl.BlockSpec((m, D_MODEL), lambda f: (0, 0)),
                   pl.BlockSpec((nst, DEC_BATCH, tf), lambda f: (0, 0, f)),
                   pl.BlockSpec((nst, DEC_BATCH, tf), lambda f: (0, 0, f))],
        out_shape=[jax.ShapeDtypeStruct((m, D_MODEL), F32),
                   jax.ShapeDtypeStruct((nst, DEC_BATCH, D_FF), F32),
                   jax.ShapeDtypeStruct((nst, DEC_BATCH, D_FF), F32)],
        compiler_params=_params(1),
    )(h, w_up, w_up, w_down, cw, cw, cb, cb, state, state)


def _rope_tables(pos):
    half = ROT_DIM // 2
    inv = ROPE_THETA ** (-jnp.arange(half, dtype=F32) * 2.0 / ROT_DIM)
    ang = pos.astype(F32)[:, None] * inv[None, :]
    cos, sin = jnp.cos(ang), jnp.sin(ang)
    t = pos.shape[0]
    rest = A_HEAD_DIM - ROT_DIM
    ch = jnp.concatenate([cos, cos, jnp.ones((t, rest), F32)], axis=1)
    ah = jnp.concatenate([-sin, jnp.zeros((t, half + rest), F32)], axis=1)
    bh = jnp.concatenate([jnp.zeros((t, half), F32), sin, jnp.zeros((t, rest), F32)], axis=1)
    rep = LANES // A_HEAD_DIM
    return tuple(jnp.tile(x, (1, rep)) for x in (ch, ah, bh))


def _pad_rows(x, rows):
    return jnp.concatenate([x, jnp.zeros((rows - x.shape[0],) + x.shape[1:], x.dtype)], axis=0)


def kernel(x_prompt, x_sample, cache_a_k, cache_a_v, state_c_conv, state_d_pool, state_ffn_conv, c_prompt, c_sample, norm_g, ada_w, ada_b, w_in_e, w_out_e, a_sinks, b_ln_g, b_ln_b, b_ws, b_bias, w_in_o, w_out_o, c_conv_w, c_conv_b, c_ln_g, c_ln_b, d_w, d_b, d_scale, ffn_w_up, ffn_conv_w, ffn_conv_b, ffn_w_down):
    assert DEPTH == 2 and x_prompt.shape == (BATCH, SEQ, D_MODEL) and x_sample.shape == (DEC_BATCH, DEC_SEQ, D_MODEL)
    w_in_e_b, w_out_e_b = w_in_e.astype(BF16), w_out_e.astype(BF16)
    w_in_o_b, w_out_o_b = w_in_o.astype(BF16), w_out_o.astype(BF16)
    w_up_b, w_down_b = ffn_w_up.astype(BF16), ffn_w_down.astype(BF16)

    pad_rows = DEC_BATCH + SUBLANES
    c_all = _pad_rows(jnp.concatenate([c_sample, c_prompt], axis=0), pad_rows)
    mods = _ada(c_all, ada_w.reshape(2 * DEPTH, D_MODEL, 3 * D_MODEL), ada_b.reshape(2 * DEPTH, 1, 3 * D_MODEL))
    mods_p = mods[:, DEC_BATCH:DEC_BATCH + BATCH].reshape(2 * DEPTH, BATCH, 1, 3 * D_MODEL)
    gains = norm_g.reshape(4 * DEPTH, 1, D_MODEL)

    grp_p = _Group(BATCH * SEQ, 512, mods_p, SEQ)
    grp_s = _Group(DEC_BATCH * DEC_SEQ, DEC_BATCH * DEC_SEQ, mods, None)
    xp = x_prompt.reshape(BATCH * SEQ, D_MODEL)
    xs = jnp.transpose(x_sample, (1, 0, 2)).reshape(DEC_SEQ * DEC_BATCH, D_MODEL)

    tabs_p = _rope_tables(jnp.arange(SEQ))
    tabs_s = tuple(_pad_rows(x, SUBLANES) for x in _rope_tables(PAST_LEN + jnp.arange(DEC_SEQ)))

    row1 = lambda v: v.reshape(1, -1)
    bexp_p = jnp.repeat(jnp.transpose(b_bias[0]), B_HEAD_DIM, axis=1)
    ws4 = jnp.tril(b_ws[0])[:, :DEC_SEQ, :DEC_SEQ]
    wexp_s = jnp.repeat(jnp.transpose(ws4, (1, 2, 0)).reshape(DEC_SEQ * DEC_SEQ, B_HEADS), B_HEAD_DIM, axis=1)
    bexp_s = _pad_rows(bexp_p[:DEC_SEQ], SUBLANES)
    sinkrow = jnp.repeat(a_sinks[0], DEC_SEQ).reshape(A_HEADS * DEC_SEQ, 1)

    def run(grp, x, is_prompt):
        outs = {}
        (h,) = _resnorm(grp, x, None, gains, nxt=(0, 0))
        tm_mm = 1024 if is_prompt else grp.m
        z = _mm(h, w_in_e_b[0], tm_mm, 896)
        if is_prompt:
            mix, nk, nv = _even_p(z.reshape(BATCH, SEQ, EVEN_IN), a_sinks[0], tabs_p, b_ws[0], bexp_p,
                                  row1(b_ln_g[0]), row1(b_ln_b[0]))
            outs["ak"], outs["av"] = nk, nv
        else:
            mix, nk, nv, vb = _even_s(z.reshape(DEC_SEQ, DEC_BATCH, EVEN_IN),
                                      cache_a_k[0].reshape(DEC_BATCH * WINDOW, A_KV_WIDTH),
                                      cache_a_v[0].reshape(DEC_BATCH * WINDOW, A_KV_WIDTH),
                                      sinkrow, tabs_s, wexp_s, bexp_s, row1(b_ln_g[0]), row1(b_ln_b[0]))
            outs["ak"], outs["av"], outs["bv"] = nk, nv, vb
        out = _mm(mix.reshape(grp.m, D_MODEL), w_out_e_b[0], tm_mm, 1024)
        x, h = _resnorm(grp, x, out, gains, res=(0, 1), nxt=(1, 2))
        x, h, outs["ff0"] = ffn(grp, x, h, 0, is_prompt, nxt=(2, 4))
        z = _mm(h, w_in_o_b[0], tm_mm, 1024)
        dwa, dba, dsa = d_w[0], row1(d_b[0]), row1(d_scale[0])
        cargs = (c_conv_w[0], row1(c_conv_b[0]), row1(c_ln_g[0]), row1(c_ln_b[0]), dwa, dba, dsa)
        if is_prompt:
            mix, ct, dt = _odd_p(z.reshape(BATCH, SEQ, ODD_IN), *cargs)
            outs["cc"] = ct[:, C_TAIL - (C_CONV_WIDTH - 1):]
            outs["dp"] = dt[:, D_TAIL - (POOL_MAX - 1):]
        else:
            mix, co, do = _odd_s(z.reshape(DEC_SEQ, DEC_BATCH, ODD_IN),
                                 jnp.transpose(state_c_conv[0], (1, 0, 2)),
                                 jnp.transpose(state_d_pool[0], (1, 0, 2)), *cargs)
            outs["cc"] = jnp.transpose(co, (1, 0, 2))
            outs["dp"] = jnp.transpose(do, (1, 0, 2))
        out = _mm(mix.reshape(grp.m, D_MODEL), w_out_o_b[0], tm_mm, 1024)
        x, h = _resnorm(grp, x, out, gains, res=(2, 5), nxt=(3, 6))
        x, _, outs["ff1"] = ffn(grp, x, h, 1, is_prompt, nxt=None)
        return x, outs

    def ffn(grp, x, h, i, is_prompt, nxt):
        cw, cb = ffn_conv_w[i], row1(ffn_conv_b[i])
        nst = FFN_CONV_WIDTH - 1
        if is_prompt:
            out, nfg, nfv = _ffn_p(h, BATCH, SEQ, w_up_b[i], w_down_b[i], cw, cb)
            nf = jnp.concatenate([nfg[:, SUBLANES - nst:], nfv[:, SUBLANES - nst:]], axis=-1)
        else:
            st = jnp.transpose(state_ffn_conv[i], (1, 0, 2))
            out, nsg, nsv = _ffn_s(h, st, w_up_b[i], w_down_b[i], cw, cb)
            nf = jnp.transpose(jnp.concatenate([nsg, nsv], axis=-1), (1, 0, 2))
        res = _resnorm(grp, x, out, gains, res=(2 * i + 1, 4 * i + 3), nxt=nxt)
        if nxt is None:
            return res[0], None, nf
        return res[0], res[1], nf

    yp, op = run(grp_p, xp, True)
    ys, os_ = run(grp_s, xs, False)

    kv5 = lambda a, nb: a.reshape(1, nb, WINDOW, A_KV_HEADS, A_HEAD_DIM)
    y_prompt = yp.reshape(BATCH, SEQ, D_MODEL)
    y_sample = jnp.transpose(ys.reshape(DEC_SEQ, DEC_BATCH, D_MODEL), (1, 0, 2))
    return (y_prompt, y_sample,
            kv5(op["ak"], BATCH), kv5(os_["ak"], DEC_BATCH), kv5(op["av"], BATCH), kv5(os_["av"], DEC_BATCH),
            jnp.transpose(os_["bv"], (1, 0, 2))[None],
            op["cc"][None], os_["cc"][None], op["dp"][None], os_["dp"][None],
            jnp.stack([op["ff0"], op["ff1"]]), jnp.stack([os_["ff0"], os_["ff1"]]))
```

```python
import functools

import jax
import jax.numpy as jnp
from jax import lax
from jax.experimental import pallas as pl
from jax.experimental.pallas import tpu as pltpu

D_MODEL = 2048
BATCH = 2
SEQ = 4096
DEPTH = 2
DEC_BATCH = 128
DEC_SEQ = 4
PAST_LEN = 8192
A_HEADS = 16
A_KV_HEADS = 4
A_HEAD_DIM = 64
A_Q_WIDTH = A_HEADS * A_HEAD_DIM
A_KV_WIDTH = A_KV_HEADS * A_HEAD_DIM
WINDOW = 128
ROT_DIM = A_HEAD_DIM // 4
ROPE_THETA = 500000.0
B_HEADS = 8
B_HEAD_DIM = 128
B_WIDTH = B_HEADS * B_HEAD_DIM
CHUNK = 128
C_WIDTH = 1024
C_CONV_WIDTH = 31
POOL_SIZES = (2, 4, 8, 16)
D_GROUPS = len(POOL_SIZES)
POOL_MAX = max(POOL_SIZES)
D_WIDTH = 1024
D_GROUP_WIDTH = D_WIDTH // D_GROUPS
D_FF = 5632
FFN_CONV_WIDTH = 3
EVEN_IN = A_Q_WIDTH + 2 * A_KV_WIDTH + 2 * B_WIDTH
ODD_IN = 2 * C_WIDTH + D_WIDTH
EPS = 1e-6
NEG_INF = -1e30

LANES = 128
SUBLANES = 8
VMEM_LIMIT = 56 * 1024 * 1024

BF16 = jnp.bfloat16
F32 = jnp.float32

SEQ_BLOCK = 32
C_TAIL = 32
D_TAIL = 16


def _params(n_axes):
    return pltpu.CompilerParams(dimension_semantics=("arbitrary",) * n_axes,
                                vmem_limit_bytes=VMEM_LIMIT)


def _gelu_tanh(x):
    return 0.5 * x * (1.0 + jnp.tanh(0.7978845608028654 * (x + 0.044715 * (x * x * x))))


def _rms(x, g):
    return x * lax.rsqrt(jnp.mean(x * x, axis=-1, keepdims=True) + EPS) * g


def _layer_norm(x, g, b):
    mu = jnp.mean(x, axis=-1, keepdims=True)
    xc = x - mu
    var = jnp.mean(xc * xc, axis=-1, keepdims=True)
    return xc * lax.rsqrt(var + EPS) * g + b


def _rows(m, tm):
    r = m.shape[0]
    if r == 1 or r == tm:
        return m
    return jnp.concatenate([m] * (tm // r), axis=0)


def _rope(x, c, a, b):
    return x * c + pltpu.roll(x, LANES - ROT_DIM // 2, 1) * a + pltpu.roll(x, ROT_DIM // 2, 1) * b


def _ada_kernel(c_ref, w_ref, b_ref, o_ref):
    c = c_ref[...]
    s = (c * jax.nn.sigmoid(c)).astype(BF16)
    o_ref[...] = jnp.dot(s, w_ref[...].astype(BF16), preferred_element_type=F32) + b_ref[...]


def _ada(c_all, w, b):
    r = c_all.shape[0]
    tn = 1024
    n_sub, _, n = w.shape
    return pl.pallas_call(
        _ada_kernel,
        grid=(n_sub, n // tn),
        in_specs=[pl.BlockSpec((r, D_MODEL), lambda k, j: (0, 0)),
                  pl.BlockSpec((None, D_MODEL, tn), lambda k, j: (k, 0, j)),
                  pl.BlockSpec((None, 1, tn), lambda k, j: (k, 0, j))],
        out_specs=pl.BlockSpec((None, r, tn), lambda k, j: (k, 0, j)),
        out_shape=jax.ShapeDtypeStruct((n_sub, r, n), F32),
        compiler_params=_params(2),
        name="ada",
    )(c_all, w, b)


def _resnorm_kernel(*refs, has_res, has_next):
    it = iter(refs)
    x_ref = next(it)
    if has_res:
        o_ref, gate_ref, gpost_ref = next(it), next(it), next(it)
    if has_next:
        gpre_ref, scale_ref, shift_ref = next(it), next(it), next(it)
    if has_res:
        xo_ref = next(it)
    if has_next:
        ho_ref = next(it)
    x = x_ref[...]
    tm = x.shape[0]
    if has_res:
        x = x + _rows(gate_ref[...], tm) * _rms(o_ref[...], gpost_ref[...])
        xo_ref[...] = x
    if has_next:
        h = _rms(x, gpre_ref[...]) * (1.0 + _rows(scale_ref[...], tm)) + _rows(shift_ref[...], tm)
        ho_ref[...] = h.astype(BF16)


class _Group:
    def __init__(self, m, tm, mods, per_batch_rows):
        self.m, self.tm, self.mods = m, tm, mods
        self.per_batch_rows = per_batch_rows

    def mod_spec(self, k, col):
        if self.per_batch_rows is None:
            return pl.BlockSpec((None, DEC_BATCH, D_MODEL), lambda i: (k, 0, col))
        tpb = self.per_batch_rows // self.tm
        return pl.BlockSpec((None, None, 1, D_MODEL), lambda i: (k, i // tpb, 0, col))


def _resnorm(grp, x, out, gains, res=None, nxt=None):
    tm = grp.tm
    row = pl.BlockSpec((tm, D_MODEL), lambda i: (i, 0))
    args, specs, out_shapes, out_specs = [x], [row], [], []

    def gain_spec(idx):
        return pl.BlockSpec((None, 1, D_MODEL), lambda i: (idx, 0, 0))

    if res is not None:
        args += [out, grp.mods, gains]
        specs += [row, grp.mod_spec(res[0], 2), gain_spec(res[1])]
        out_shapes.append(jax.ShapeDtypeStruct((grp.m, D_MODEL), F32))
        out_specs.append(row)
    if nxt is not None:
        args += [gains, grp.mods, grp.mods]
        specs += [gain_spec(nxt[1]), grp.mod_spec(nxt[0], 1), grp.mod_spec(nxt[0], 0)]
        out_shapes.append(jax.ShapeDtypeStruct((grp.m, D_MODEL), BF16))
        out_specs.append(row)
    res_out = pl.pallas_call(
        functools.partial(_resnorm_kernel, has_res=res is not None, has_next=nxt is not None),
        grid=(grp.m // tm,),
        in_specs=specs, out_specs=out_specs, out_shape=out_shapes,
        compiler_params=_params(1),
        name="resnorm",
    )(*args)
    return res_out


def _mm_kernel(a_ref, w_ref, o_ref):
    o_ref[...] = jnp.dot(a_ref[...], w_ref[...], preferred_element_type=F32).astype(o_ref.dtype)


def _mm(a, w, layer, tm, tn, name, out_dtype=F32):
    m, k = a.shape
    n = w.shape[2]
    return pl.pallas_call(
        _mm_kernel,
        grid=(n // tn, m // tm),
        in_specs=[pl.BlockSpec((tm, k), lambda j, i: (i, 0)),
                  pl.BlockSpec((None, k, tn), lambda j, i: (layer, 0, j))],
        out_specs=pl.BlockSpec((tm, tn), lambda j, i: (i, j)),
        out_shape=jax.ShapeDtypeStruct((m, n), out_dtype),
        compiler_params=_params(2),
        name=name,
    )(a, w)


def _mm_res_kernel(a_ref, w_ref, x_ref, gate_ref, gpost_ref, gpre_ref, scale_ref, shift_ref, xo_ref, ho_ref):
    out = jnp.dot(a_ref[...], w_ref[...], preferred_element_type=F32)
    tm = out.shape[0]
    x = x_ref[...] + _rows(gate_ref[...], tm) * _rms(out, gpost_ref[...])
    xo_ref[...] = x
    h = _rms(x, gpre_ref[...]) * (1.0 + _rows(scale_ref[...], tm)) + _rows(shift_ref[...], tm)
    ho_ref[...] = h.astype(BF16)


def _mm_res(grp, a, w, layer, x, gains, res, nxt, name):
    tm = grp.tm
    k = a.shape[1]
    row = pl.BlockSpec((tm, D_MODEL), lambda i: (i, 0))
    gain_spec = lambda idx: pl.BlockSpec((None, 1, D_MODEL), lambda i: (idx, 0, 0))
    return pl.pallas_call(
        _mm_res_kernel,
        grid=(grp.m // tm,),
        in_specs=[pl.BlockSpec((tm, k), lambda i: (i, 0)),
                  pl.BlockSpec((None, k, D_MODEL), lambda i: (layer, 0, 0)),
                  row, grp.mod_spec(res[0], 2), gain_spec(res[1]),
                  gain_spec(nxt[1]), grp.mod_spec(nxt[0], 1), grp.mod_spec(nxt[0], 0)],
        out_specs=[row, row],
        out_shape=[jax.ShapeDtypeStruct((grp.m, D_MODEL), F32), jax.ShapeDtypeStruct((grp.m, D_MODEL), BF16)],
        compiler_params=_params(1),
        name=name,
    )(a, w, x, grp.mods, gains, gains, grp.mods, grp.mods)


def _even_p_kernel(sink_ref, z_ref, cos_ref, sa_ref, sb_ref, ws_ref, bexp_ref, lng_ref, lnb_ref,
                   mix_ref, nk_ref, nv_ref, kprev, vprev):
    b = pl.program_id(1)
    last = pl.num_programs(1) - 1
    qb = WINDOW
    cos_t, sa_t, sb_t = cos_ref[...], sa_ref[...], sb_ref[...]

    @pl.when(b == 0)
    def _():
        kprev[...] = jnp.zeros_like(kprev)
        vprev[...] = jnp.zeros_like(vprev)

    lane = lax.broadcasted_iota(jnp.int32, (qb, LANES), 1)
    lo = lane < A_HEAD_DIM
    lane2 = lax.broadcasted_iota(jnp.int32, (2 * qb, LANES), 1)
    lo2 = lane2 < A_HEAD_DIM
    row = lax.broadcasted_iota(jnp.int32, (qb, 2 * qb), 0)
    col = lax.broadcasted_iota(jnp.int32, (qb, 2 * qb), 1)
    in_window = ((col > row) & (col < qb)) | ((col >= qb) & (col - qb <= row))
    mask = in_window & (col >= jnp.where(b > 0, 0, qb))
    nt = (((1,), (1,)), ((), ()))

    for jk in range(A_KV_HEADS // 2):
        ksl = slice(A_Q_WIDTH + LANES * jk, A_Q_WIDTH + LANES * (jk + 1))
        vsl = slice(A_Q_WIDTH + A_KV_WIDTH + LANES * jk, A_Q_WIDTH + A_KV_WIDTH + LANES * (jk + 1))
        csl = slice(LANES * jk, LANES * (jk + 1))
        kc = _rope(z_ref[:, ksl], cos_t, sa_t, sb_t)
        vc = z_ref[:, vsl]
        kall = jnp.concatenate([kprev[:, csl], kc], axis=0)
        vall = jnp.concatenate([vprev[:, csl], vc], axis=0)
        krol = pltpu.roll(kall, A_HEAD_DIM, 1)
        vrol = pltpu.roll(vall, A_HEAD_DIM, 1)
        for sub in range(2):
            kvh = 2 * jk + sub
            kd = (jnp.where(lo2, kall, krol) if sub == 0 else jnp.where(lo2, krol, kall)).astype(BF16)
            vd = (jnp.where(lo2, vall, vrol) if sub == 0 else jnp.where(lo2, vrol, vall)).astype(BF16)
            for qs in range(2):
                js = 2 * kvh + qs
                qsl = _rope(z_ref[:, LANES * js:LANES * (js + 1)], cos_t, sa_t, sb_t) * (A_HEAD_DIM ** -0.5)
                outs = []
                for half in range(2):
                    h = 2 * js + half
                    qm = jnp.where(lo if half == 0 else jnp.logical_not(lo), qsl, 0.0).astype(BF16)
                    s = lax.dot_general(qm, kd, nt, preferred_element_type=F32)
                    s = jnp.where(mask, s, NEG_INF)
                    sink = sink_ref[h]
                    m = jnp.maximum(jnp.max(s, axis=-1, keepdims=True), sink)
                    p = jnp.exp(s - m)
                    den = jnp.sum(p, axis=-1, keepdims=True) + jnp.exp(sink - m)
                    o = jnp.dot(p.astype(BF16), vd, preferred_element_type=F32)
                    outs.append(o / den)
                mix_ref[:, LANES * js:LANES * (js + 1)] = jnp.where(lo, outs[0], outs[1]).astype(BF16)
        kprev[:, csl] = kc
        vprev[:, csl] = vc

        @pl.when(b == last)
        def _():
            nk_ref[:, csl] = kc
            nv_ref[:, csl] = vc

    zb0 = A_Q_WIDTH + 2 * A_KV_WIDTH
    u = _gelu_tanh(z_ref[:, zb0:zb0 + B_WIDTH])
    vb = _layer_norm(_gelu_tanh(z_ref[:, zb0 + B_WIDTH:zb0 + 2 * B_WIDTH]), lng_ref[...], lnb_ref[...])
    ri = lax.broadcasted_iota(jnp.int32, (CHUNK, CHUNK), 0)
    ci = lax.broadcasted_iota(jnp.int32, (CHUNK, CHUNK), 1)
    tri = ri >= ci
    for h in range(B_HEADS):
        hs = slice(B_HEAD_DIM * h, B_HEAD_DIM * (h + 1))
        w = jnp.where(tri, ws_ref[h], 0.0).astype(BF16)
        mixed = jnp.dot(w, vb[:, hs].astype(BF16), preferred_element_type=F32) + bexp_ref[:, hs]
        mix_ref[:, A_Q_WIDTH + B_HEAD_DIM * h:A_Q_WIDTH + B_HEAD_DIM * (h + 1)] = (u[:, hs] * mixed).astype(BF16)


def _even_p(z, sinks, tabs, ws, bexp, lng, lnb):
    n, t, _ = z.shape
    qb = WINDOW
    full = lambda shape: pl.BlockSpec(shape, lambda i, b: (0,) * len(shape))
    tab = pl.BlockSpec((qb, LANES), lambda i, b: (b, 0))
    return pl.pallas_call(
        _even_p_kernel,
        grid=(n, t // qb),
        in_specs=[pl.BlockSpec(memory_space=pltpu.SMEM),
                  pl.BlockSpec((None, qb, EVEN_IN), lambda i, b: (i, b, 0)),
                  tab, tab, tab,
                  full((B_HEADS, CHUNK, CHUNK)), full((CHUNK, B_WIDTH)),
                  full((1, B_WIDTH)), full((1, B_WIDTH))],
        out_specs=[pl.BlockSpec((None, qb, D_MODEL), lambda i, b: (i, b, 0)),
                   pl.BlockSpec((None, qb, A_KV_WIDTH), lambda i, b: (i, 0, 0)),
                   pl.BlockSpec((None, qb, A_KV_WIDTH), lambda i, b: (i, 0, 0))],
        out_shape=[jax.ShapeDtypeStruct((n, t, D_MODEL), BF16),
                   jax.ShapeDtypeStruct((n, qb, A_KV_WIDTH), F32),
                   jax.ShapeDtypeStruct((n, qb, A_KV_WIDTH), F32)],
        scratch_shapes=[pltpu.VMEM((qb, A_KV_WIDTH), F32), pltpu.VMEM((qb, A_KV_WIDTH), F32)],
        compiler_params=_params(2),
        name="even_p",
    )(sinks, z, *tabs, ws, bexp, lng, lnb)


def _even_s_kernel(sinkrow_ref, z_ref, ck_ref, cv_ref, cos_ref, sa_ref, sb_ref, wexp_ref, bexp_ref,
                   lng_ref, lnb_ref, mix_ref, ok_ref, ov_ref, vb_ref, qm, osc, kn, vn):
    s_blk = SEQ_BLOCK
    rows_per_seq = A_HEADS * DEC_SEQ
    lane256 = lax.broadcasted_iota(jnp.int32, (s_blk, A_KV_WIDTH), 1)
    kn[...] = jnp.zeros_like(kn)
    vn[...] = jnp.zeros_like(vn)
    halves = A_KV_WIDTH // LANES

    def put(ref, start, stride, val):
        for c in range(halves):
            ref[c, pl.ds(start, s_blk, stride=stride), :] = val[:, LANES * c:LANES * (c + 1)]

    def get_rows(ref, start, size):
        return jnp.concatenate([ref[c, pl.ds(start, size), :] for c in range(halves)], axis=1)

    def get_strided(ref, start, stride):
        return jnp.concatenate([ref[c, pl.ds(start, s_blk, stride=stride), :] for c in range(halves)], axis=1)

    for t in range(DEC_SEQ):
        c, a, bb = cos_ref[t:t + 1, :], sa_ref[t:t + 1, :], sb_ref[t:t + 1, :]

        def rope2(lo_lane):
            return jnp.concatenate([_rope(z_ref[t, :, lo_lane:lo_lane + LANES], c, a, bb),
                                    _rope(z_ref[t, :, lo_lane + LANES:lo_lane + 2 * LANES], c, a, bb)], axis=1)

        put(kn, t, SUBLANES, rope2(A_Q_WIDTH))
        put(vn, t, SUBLANES, z_ref[t, :, A_Q_WIDTH + A_KV_WIDTH:A_Q_WIDTH + 2 * A_KV_WIDTH])
        for j in range(A_KV_HEADS):
            qs = rope2(A_KV_WIDTH * j) * (A_HEAD_DIM ** -0.5)
            keep = (lane256 >= A_HEAD_DIM * j) & (lane256 < A_HEAD_DIM * (j + 1))
            for g in range(A_HEADS // A_KV_HEADS):
                sh = ((j - g) * A_HEAD_DIM) % A_KV_WIDTH
                qr = qs if sh == 0 else pltpu.roll(qs, sh, 1)
                r = (4 * j + g) * DEC_SEQ + t
                put(qm, r, rows_per_seq, jnp.where(keep, qr, 0.0))

    rr = lax.broadcasted_iota(jnp.int32, (rows_per_seq, WINDOW), 0) % DEC_SEQ
    cc = lax.broadcasted_iota(jnp.int32, (rows_per_seq, WINDOW), 1)
    mask_old = cc > rr
    new0 = WINDOW - DEC_SEQ
    mask_new = (cc >= new0) & (cc - new0 <= rr)
    row8 = lax.broadcasted_iota(jnp.int32, (SUBLANES, A_KV_WIDTH), 0)
    sink = sinkrow_ref[...]
    nt = (((1,), (1,)), ((), ()))

    def shifted(c_ref, new_ref, o_ref, n):
        base = pl.multiple_of(n * WINDOW, WINDOW)
        old = c_ref[pl.ds(base, WINDOW), :]
        rolled = pltpu.roll(old, WINDOW - DEC_SEQ, 0)
        fresh = get_rows(new_ref, pl.multiple_of(n * SUBLANES, SUBLANES), SUBLANES)
        tail = jnp.where(row8 >= SUBLANES - DEC_SEQ, pltpu.roll(fresh, SUBLANES - DEC_SEQ, 0),
                         rolled[WINDOW - SUBLANES:, :])
        new = jnp.concatenate([rolled[:WINDOW - SUBLANES, :], tail], axis=0)
        o_ref[pl.ds(base, WINDOW), :] = new
        return old.astype(BF16), new.astype(BF16)

    def body(n, carry):
        k_old, k_new = shifted(ck_ref, kn, ok_ref, n)
        v_old, v_new = shifted(cv_ref, vn, ov_ref, n)
        q = get_rows(qm, pl.multiple_of(n * rows_per_seq, rows_per_seq), rows_per_seq).astype(BF16)
        s_old = jnp.where(mask_old, lax.dot_general(q, k_old, nt, preferred_element_type=F32), NEG_INF)
        s_new = jnp.where(mask_new, lax.dot_general(q, k_new, nt, preferred_element_type=F32), NEG_INF)
        m = jnp.maximum(jnp.maximum(jnp.max(s_old, axis=-1, keepdims=True),
                                    jnp.max(s_new, axis=-1, keepdims=True)), sink)
        p_old = jnp.exp(s_old - m)
        p_new = jnp.exp(s_new - m)
        den = (jnp.sum(p_old, axis=-1, keepdims=True) + jnp.sum(p_new, axis=-1, keepdims=True)
               + jnp.exp(sink - m))
        o = (jnp.dot(p_old.astype(BF16), v_old, preferred_element_type=F32)
             + jnp.dot(p_new.astype(BF16), v_new, preferred_element_type=F32))
        o = o / den
        for c in range(halves):
            osc[c, pl.ds(pl.multiple_of(n * rows_per_seq, rows_per_seq), rows_per_seq), :] = o[:, LANES * c:LANES * (c + 1)]
        return carry

    lax.fori_loop(0, s_blk, body, 0)

    for t in range(DEC_SEQ):
        for j in range(A_KV_HEADS):
            acc = jnp.zeros((s_blk, A_KV_WIDTH), F32)
            for g in range(A_HEADS // A_KV_HEADS):
                r = (4 * j + g) * DEC_SEQ + t
                ov = get_strided(osc, r, rows_per_seq)
                sh = ((g - j) * A_HEAD_DIM) % A_KV_WIDTH
                orr = ov if sh == 0 else pltpu.roll(ov, sh, 1)
                acc = jnp.where((lane256 >= A_HEAD_DIM * g) & (lane256 < A_HEAD_DIM * (g + 1)), orr, acc)
            mix_ref[t, :, A_KV_WIDTH * j:A_KV_WIDTH * (j + 1)] = acc.astype(BF16)

    zb0 = A_Q_WIDTH + 2 * A_KV_WIDTH
    us, vbs = [], []
    for t in range(DEC_SEQ):
        us.append(_gelu_tanh(z_ref[t, :, zb0:zb0 + B_WIDTH]))
        v = _layer_norm(_gelu_tanh(z_ref[t, :, zb0 + B_WIDTH:zb0 + 2 * B_WIDTH]), lng_ref[...], lnb_ref[...])
        vb_ref[t] = v
        vbs.append(v)
    for t in range(DEC_SEQ):
        mixed = bexp_ref[t:t + 1, :]
        for s in range(t + 1):
            mixed = mixed + wexp_ref[DEC_SEQ * t + s:DEC_SEQ * t + s + 1, :] * vbs[s]
        mix_ref[t, :, A_Q_WIDTH:A_Q_WIDTH + B_WIDTH] = (us[t] * mixed).astype(BF16)


def _even_s(z3, ck, cv, sinkrow, tabs, wexp, bexp, lng, lnb):
    s_blk = SEQ_BLOCK
    rows_per_seq = A_HEADS * DEC_SEQ
    full = lambda shape: pl.BlockSpec(shape, lambda s: (0,) * len(shape))
    cache = pl.BlockSpec((s_blk * WINDOW, A_KV_WIDTH), lambda s: (s, 0))
    return pl.pallas_call(
        _even_s_kernel,
        grid=(DEC_BATCH // s_blk,),
        in_specs=[full((rows_per_seq, 1)),
                  pl.BlockSpec((DEC_SEQ, s_blk, EVEN_IN), lambda s: (0, s, 0)),
                  cache, cache,
                  full((SUBLANES, LANES)), full((SUBLANES, LANES)), full((SUBLANES, LANES)),
                  full((DEC_SEQ * DEC_SEQ, B_WIDTH)), full((SUBLANES, B_WIDTH)),
                  full((1, B_WIDTH)), full((1, B_WIDTH))],
        out_specs=[pl.BlockSpec((DEC_SEQ, s_blk, D_MODEL), lambda s: (0, s, 0)),
                   cache, cache,
                   pl.BlockSpec((DEC_SEQ, s_blk, B_WIDTH), lambda s: (0, s, 0))],
        out_shape=[jax.ShapeDtypeStruct((DEC_SEQ, DEC_BATCH, D_MODEL), BF16),
                   jax.ShapeDtypeStruct((DEC_BATCH * WINDOW, A_KV_WIDTH), F32),
                   jax.ShapeDtypeStruct((DEC_BATCH * WINDOW, A_KV_WIDTH), F32),
                   jax.ShapeDtypeStruct((DEC_SEQ, DEC_BATCH, B_WIDTH), F32)],
        scratch_shapes=[pltpu.VMEM((A_KV_WIDTH // LANES, s_blk * rows_per_seq, LANES), F32),
                        pltpu.VMEM((A_KV_WIDTH // LANES, s_blk * rows_per_seq, LANES), F32),
                        pltpu.VMEM((A_KV_WIDTH // LANES, s_blk * SUBLANES, LANES), F32),
                        pltpu.VMEM((A_KV_WIDTH // LANES, s_blk * SUBLANES, LANES), F32)],
        compiler_params=_params(1),
        name="even_s",
    )(sinkrow, z3, ck, cv, *tabs, wexp, bexp, lng, lnb)


ODD_ROWS = 256
ODD_RC = 64


def _odd_p_kernel(z_ref, cw_ref, cb_ref, lng_ref, lnb_ref, dw_ref, db_ref, dsc_ref,
                  mix_ref, ctail_ref, dtail_ref, gext, dext, cbuf, dbuf, shbuf):
    b = pl.program_id(1)
    last = pl.num_programs(1) - 1
    tr = ODD_ROWS

    @pl.when(b == 0)
    def _():
        gext[0:C_TAIL, :] = jnp.zeros((C_TAIL, C_WIDTH), F32)
        dext[0:D_TAIL, :] = jnp.zeros((D_TAIL, D_WIDTH), F32)

    gext[C_TAIL:C_TAIL + tr, :] = z_ref[:, 0:C_WIDTH] * jax.nn.sigmoid(z_ref[:, C_WIDTH:2 * C_WIDTH])
    dext[D_TAIL:D_TAIL + tr, :] = z_ref[:, 2 * C_WIDTH:2 * C_WIDTH + D_WIDTH]

    gw = D_GROUP_WIDTH
    lead = C_TAIL - (C_CONV_WIDTH - 1)
    sh_rows = C_TAIL + tr - SUBLANES
    for s in range(1, SUBLANES):
        shbuf[s, 0:sh_rows, :] = gext[s:s + sh_rows, :]

    def tap(off, r0, ls):
        a, s = divmod(off, SUBLANES)
        src = gext if s == 0 else shbuf.at[s]
        return src[r0 + SUBLANES * a:r0 + SUBLANES * a + ODD_RC, ls]

    for rc in range(tr // ODD_RC):
        r0 = rc * ODD_RC
        for c in range(C_WIDTH // gw):
            ls = slice(gw * c, gw * (c + 1))
            acc = jnp.broadcast_to(cb_ref[:, ls], (ODD_RC, gw))
            for k in range(C_CONV_WIDTH):
                acc = acc + cw_ref[k:k + 1, ls] * tap(lead + k, r0, ls)
            cbuf[r0:r0 + ODD_RC, ls] = acc
        y = _layer_norm(cbuf[r0:r0 + ODD_RC, :], lng_ref[...], lnb_ref[...])
        mix_ref[r0:r0 + ODD_RC, 0:C_WIDTH] = (y * jax.nn.sigmoid(y)).astype(BF16)

        pos1 = (b * tr + r0 + 1 + lax.broadcasted_iota(jnp.int32, (ODD_RC, gw), 0))
        for g, wsz in enumerate(POOL_SIZES):
            ls = slice(gw * g, gw * (g + 1))
            win = dext[D_TAIL + r0:D_TAIL + r0 + ODD_RC, ls]
            for jj in range(1, wsz):
                win = win + dext[D_TAIL + r0 - jj:D_TAIL + r0 - jj + ODD_RC, ls]
            cnt = jnp.minimum(pos1, wsz).astype(F32)
            diff = win / cnt - dext[D_TAIL + r0:D_TAIL + r0 + ODD_RC, ls]
            dbuf[r0:r0 + ODD_RC, ls] = diff.astype(BF16)

    for g in range(D_GROUPS):
        ls = slice(gw * g, gw * (g + 1))
        o = jnp.dot(dbuf[:, ls], dw_ref[g].astype(BF16), preferred_element_type=F32) + db_ref[:, ls]
        mix_ref[:, C_WIDTH + gw * g:C_WIDTH + gw * (g + 1)] = (o * dsc_ref[:, ls]).astype(BF16)

    gext[0:C_TAIL, :] = gext[tr:tr + C_TAIL, :]
    dext[0:D_TAIL, :] = dext[tr:tr + D_TAIL, :]

    @pl.when(b == last)
    def _():
        ctail_ref[...] = gext[0:C_TAIL, :]
        dtail_ref[...] = dext[0:D_TAIL, :]


def _odd_p(z, cw, cb, lng, lnb, dw, db, dsc):
    n, t, _ = z.shape
    tr = ODD_ROWS
    full = lambda shape: pl.BlockSpec(shape, lambda i, b: (0,) * len(shape))
    return pl.pallas_call(
        _odd_p_kernel,
        grid=(n, t // tr),
        in_specs=[pl.BlockSpec((None, tr, ODD_IN), lambda i, b: (i, b, 0)),
                  full((C_CONV_WIDTH, C_WIDTH)), full((1, C_WIDTH)), full((1, C_WIDTH)), full((1, C_WIDTH)),
                  full((D_GROUPS, D_GROUP_WIDTH, D_GROUP_WIDTH)), full((1, D_WIDTH)), full((1, D_WIDTH))],
        out_specs=[pl.BlockSpec((None, tr, D_MODEL), lambda i, b: (i, b, 0)),
                   pl.BlockSpec((None, C_TAIL, C_WIDTH), lambda i, b: (i, 0, 0)),
                   pl.BlockSpec((None, D_TAIL, D_WIDTH), lambda i, b: (i, 0, 0))],
        out_shape=[jax.ShapeDtypeStruct((n, t, D_MODEL), BF16),
                   jax.ShapeDtypeStruct((n, C_TAIL, C_WIDTH), F32),
                   jax.ShapeDtypeStruct((n, D_TAIL, D_WIDTH), F32)],
        scratch_shapes=[pltpu.VMEM((C_TAIL + tr, C_WIDTH), F32), pltpu.VMEM((D_TAIL + tr, D_WIDTH), F32),
                        pltpu.VMEM((tr, C_WIDTH), F32), pltpu.VMEM((tr, D_WIDTH), BF16),
                        pltpu.VMEM((SUBLANES, C_TAIL + tr, C_WIDTH), F32)],
        compiler_params=_params(2),
        name="odd_p",
    )(z, cw, cb, lng, lnb, dw, db, dsc)


def _odd_s_kernel(z_ref, cs_ref, ds_ref, cw_ref, cb_ref, lng_ref, lnb_ref, dw_ref, db_ref, dsc_ref,
                  mix_ref, co_ref, do_ref, cbuf, dbuf):
    s_blk = SEQ_BLOCK
    gw = D_GROUP_WIDTH
    nc, nd = C_CONV_WIDTH - 1, POOL_MAX - 1
    for c in range(C_WIDTH // gw):
        ls = slice(gw * c, gw * (c + 1))
        accs = [jnp.broadcast_to(cb_ref[:, ls], (s_blk, gw)) for _ in range(DEC_SEQ)]
        for j in range(nc + DEC_SEQ):
            if j < nc:
                e = cs_ref[j, :, ls]
            else:
                t = j - nc
                e = z_ref[t, :, gw * c:gw * (c + 1)] * jax.nn.sigmoid(
                    z_ref[t, :, C_WIDTH + gw * c:C_WIDTH + gw * (c + 1)])
            if j >= DEC_SEQ:
                co_ref[j - DEC_SEQ, :, ls] = e
            for t in range(DEC_SEQ):
                k = j - t
                if 0 <= k < C_CONV_WIDTH:
                    accs[t] = accs[t] + cw_ref[k:k + 1, ls] * e
        for t in range(DEC_SEQ):
            cbuf[t, :, ls] = accs[t]
    for t in range(DEC_SEQ):
        y = _layer_norm(cbuf[t], lng_ref[...], lnb_ref[...])
        mix_ref[t, :, 0:C_WIDTH] = (y * jax.nn.sigmoid(y)).astype(BF16)

    for g, wsz in enumerate(POOL_SIZES):
        ls = slice(gw * g, gw * (g + 1))
        ext = []
        for j in range(nd + DEC_SEQ):
            if j < nd:
                e = ds_ref[j, :, ls]
            else:
                e = z_ref[j - nd, :, 2 * C_WIDTH + gw * g:2 * C_WIDTH + gw * (g + 1)]
            if j >= DEC_SEQ:
                do_ref[j - DEC_SEQ, :, ls] = e
            ext.append(e)
        for t in range(DEC_SEQ):
            win = ext[nd + t]
            for jj in range(1, wsz):
                win = win + ext[nd + t - jj]
            cnt = float(min(PAST_LEN + t + 1, wsz))
            dbuf[s_blk * t:s_blk * (t + 1), ls] = (win / cnt - ext[nd + t]).astype(BF16)
    for g in range(D_GROUPS):
        ls = slice(gw * g, gw * (g + 1))
        o = jnp.dot(dbuf[:, ls], dw_ref[g].astype(BF16), preferred_element_type=F32) + db_ref[:, ls]
        o = (o * dsc_ref[:, ls]).astype(BF16)
        for t in range(DEC_SEQ):
            mix_ref[t, :, C_WIDTH + gw * g:C_WIDTH + gw * (g + 1)] = o[s_blk * t:s_blk * (t + 1), :]


def _odd_s(z3, cs, ds, cw, cb, lng, lnb, dw, db, dsc):
    s_blk = SEQ_BLOCK
    nc, nd = C_CONV_WIDTH - 1, POOL_MAX - 1
    full = lambda shape: pl.BlockSpec(shape, lambda s: (0,) * len(shape))
    cst = pl.BlockSpec((nc, s_blk, C_WIDTH), lambda s: (0, s, 0))
    dst = pl.BlockSpec((nd, s_blk, D_WIDTH), lambda s: (0, s, 0))
    return pl.pallas_call(
        _odd_s_kernel,
        grid=(DEC_BATCH // s_blk,),
        in_specs=[pl.BlockSpec((DEC_SEQ, s_blk, ODD_IN), lambda s: (0, s, 0)), cst, dst,
                  full((C_CONV_WIDTH, C_WIDTH)), full((1, C_WIDTH)), full((1, C_WIDTH)), full((1, C_WIDTH)),
                  full((D_GROUPS, D_GROUP_WIDTH, D_GROUP_WIDTH)), full((1, D_WIDTH)), full((1, D_WIDTH))],
        out_specs=[pl.BlockSpec((DEC_SEQ, s_blk, D_MODEL), lambda s: (0, s, 0)), cst, dst],
        out_shape=[jax.ShapeDtypeStruct((DEC_SEQ, DEC_BATCH, D_MODEL), BF16),
                   jax.ShapeDtypeStruct((nc, DEC_BATCH, C_WIDTH), F32),
                   jax.ShapeDtypeStruct((nd, DEC_BATCH, D_WIDTH), F32)],
        scratch_shapes=[pltpu.VMEM((DEC_SEQ, s_blk, C_WIDTH), F32),
                        pltpu.VMEM((DEC_SEQ * s_blk, D_WIDTH), BF16)],
        compiler_params=_params(1),
        name="odd_s",
    )(z3, cs, ds, cw, cb, lng, lnb, dw, db, dsc)


FFN_TF = 512
FFN_TM = 1024


FFN_SPLIT = 4


def _ffn_p_kernel(h_ref, wg_ref, wv_ref, wd_ref, cwg_ref, cwv_ref, cbg_ref, cbv_ref,
                  o_ref, nfg_ref, nfv_ref, carry_g, carry_v):
    i, f = pl.program_id(1), pl.program_id(2)
    rh = FFN_TM // FFN_SPLIT
    row = lax.broadcasted_iota(jnp.int32, (rh, FFN_TF), 0)

    @pl.when(i == 0)
    def _():
        carry_g[f] = jnp.zeros((SUBLANES, FFN_TF), F32)
        carry_v[f] = jnp.zeros((SUBLANES, FFN_TF), F32)

    @pl.when(f == 0)
    def _():
        o_ref[...] = jnp.zeros_like(o_ref)

    def conv(up, prev, cw_ref, cb_ref):
        p1 = jnp.broadcast_to(prev[SUBLANES - 1:SUBLANES, :], (rh, FFN_TF))
        p2 = jnp.broadcast_to(prev[SUBLANES - 2:SUBLANES - 1, :], (rh, FFN_TF))
        m1 = jnp.where(row == 0, p1, pltpu.roll(up, 1, 0))
        m2 = jnp.where(row == 0, p2, jnp.where(row == 1, p1, pltpu.roll(up, 2, 0)))
        return cw_ref[2:3, :] * up + cw_ref[1:2, :] * m1 + cw_ref[0:1, :] * m2 + cb_ref[...]

    ups = []
    for s in range(FFN_SPLIT):
        hs = h_ref[rh * s:rh * (s + 1), :]
        ups.append((jnp.dot(hs, wg_ref[...], preferred_element_type=F32),
                    jnp.dot(hs, wv_ref[...], preferred_element_type=F32)))
    prev_g, prev_v = carry_g[f], carry_v[f]
    for s in range(FFN_SPLIT):
        ug, uv = ups[s]
        g = conv(ug, prev_g, cwg_ref, cbg_ref)
        v = conv(uv, prev_v, cwv_ref, cbv_ref)
        prev_g, prev_v = ug[rh - SUBLANES:, :], uv[rh - SUBLANES:, :]
        act = (_gelu_tanh(g) * v).astype(BF16)
        o_ref[rh * s:rh * (s + 1), :] += jnp.dot(act, wd_ref[...], preferred_element_type=F32)
    carry_g[f] = prev_g
    carry_v[f] = prev_v
    nfg_ref[...] = prev_g
    nfv_ref[...] = prev_v


def _ffn_p(h, n, t, w_up, w_down, cw, cb, layer, name):
    tm, tf = FFN_TM, FFN_TF
    nf = D_FF // tf
    tpb = t // tm
    return pl.pallas_call(
        _ffn_p_kernel,
        grid=(n, tpb, nf),
        in_specs=[pl.BlockSpec((tm, D_MODEL), lambda b, i, f: (b * tpb + i, 0)),
                  pl.BlockSpec((None, D_MODEL, tf), lambda b, i, f: (layer, 0, f)),
                  pl.BlockSpec((None, D_MODEL, tf), lambda b, i, f: (layer, 0, nf + f)),
                  pl.BlockSpec((None, tf, D_MODEL), lambda b, i, f: (layer, f, 0)),
                  pl.BlockSpec((None, FFN_CONV_WIDTH, tf), lambda b, i, f: (layer, 0, f)),
                  pl.BlockSpec((None, FFN_CONV_WIDTH, tf), lambda b, i, f: (layer, 0, nf + f)),
                  pl.BlockSpec((None, 1, tf), lambda b, i, f: (layer, 0, f)),
                  pl.BlockSpec((None, 1, tf), lambda b, i, f: (layer, 0, nf + f))],
        out_specs=[pl.BlockSpec((tm, D_MODEL), lambda b, i, f: (b * tpb + i, 0)),
                   pl.BlockSpec((None, None, SUBLANES, tf), lambda b, i, f: (b, i, 0, f)),
                   pl.BlockSpec((None, None, SUBLANES, tf), lambda b, i, f: (b, i, 0, f))],
        out_shape=[jax.ShapeDtypeStruct((n * t, D_MODEL), F32),
                   jax.ShapeDtypeStruct((n, tpb, SUBLANES, D_FF), F32),
                   jax.ShapeDtypeStruct((n, tpb, SUBLANES, D_FF), F32)],
        scratch_shapes=[pltpu.VMEM((nf, SUBLANES, tf), F32), pltpu.VMEM((nf, SUBLANES, tf), F32)],
        compiler_params=_params(3),
        name=name,
    )(h, w_up, w_up, w_down, cw, cw, cb, cb)


def _ffn_s_kernel(h_ref, wg_ref, wv_ref, wd_ref, cwg_ref, cwv_ref, cbg_ref, cbv_ref, sg_ref, sv_ref,
                  o_ref, nsg_ref, nsv_ref):
    f = pl.program_id(0)
    nb = DEC_BATCH
    h = h_ref[...]
    nst = FFN_CONV_WIDTH - 1

    def conv(w_ref, cw_ref, cb_ref, st_ref, ns_ref):
        up = jnp.dot(h, w_ref[...], preferred_element_type=F32)
        ext = [st_ref[j] for j in range(nst)]
        ext += [up[nb * t:nb * (t + 1), :] for t in range(DEC_SEQ)]
        for j in range(nst):
            ns_ref[j] = ext[DEC_SEQ + j]
        return [cw_ref[0:1, :] * ext[t] + cw_ref[1:2, :] * ext[t + 1] + cw_ref[2:3, :] * ext[t + 2] + cb_ref[...]
                for t in range(DEC_SEQ)]

    g = conv(wg_ref, cwg_ref, cbg_ref, sg_ref, nsg_ref)
    v = conv(wv_ref, cwv_ref, cbv_ref, sv_ref, nsv_ref)
    act = jnp.concatenate([(_gelu_tanh(g[t]) * v[t]).astype(BF16) for t in range(DEC_SEQ)], axis=0)
    part = jnp.dot(act, wd_ref[...], preferred_element_type=F32)

    @pl.when(f == 0)
    def _():
        o_ref[...] = part

    @pl.when(f > 0)
    def _():
        o_ref[...] += part


def _ffn_s(h, state, w_up, w_down, cw, cb, layer, name):
    tf = FFN_TF
    nf = D_FF // tf
    m = h.shape[0]
    nst = FFN_CONV_WIDTH - 1
    return pl.pallas_call(
        _ffn_s_kernel,
        grid=(nf,),
        in_specs=[pl.BlockSpec((m, D_MODEL), lambda f: (0, 0)),
                  pl.BlockSpec((None, D_MODEL, tf), lambda f: (layer, 0, f)),
                  pl.BlockSpec((None, D_MODEL, tf), lambda f: (layer, 0, nf + f)),
                  pl.BlockSpec((None, tf, D_MODEL), lambda f: (layer, f, 0)),
                  pl.BlockSpec((None, FFN_CONV_WIDTH, tf), lambda f: (layer, 0, f)),
                  pl.BlockSpec((None, FFN_CONV_WIDTH, tf), lambda f: (layer, 0, nf + f)),
                  pl.BlockSpec((None, 1, tf), lambda f: (layer, 0, f)),
                  pl.BlockSpec((None, 1, tf), lambda f: (layer, 0, nf + f)),
                  pl.BlockSpec((nst, DEC_BATCH, tf), lambda f: (0, 0, f)),
                  pl.BlockSpec((nst, DEC_BATCH, tf), lambda f: (0, 0, nf + f))],
        out_specs=[pl.BlockSpec((m, D_MODEL), lambda f: (0, 0)),
                   pl.BlockSpec((nst, DEC_BATCH, tf), lambda f: (0, 0, f)),
                   pl.BlockSpec((nst, DEC_BATCH, tf), lambda f: (0, 0, f))],
        out_shape=[jax.ShapeDtypeStruct((m, D_MODEL), F32),
                   jax.ShapeDtypeStruct((nst, DEC_BATCH, D_FF), F32),
                   jax.ShapeDtypeStruct((nst, DEC_BATCH, D_FF), F32)],
        compiler_params=_params(1),
        name=name,
    )(h, w_up, w_up, w_down, cw, cw, cb, cb, state, state)


def _rope_tables(pos):
    half = ROT_DIM // 2
    inv = ROPE_THETA ** (-jnp.arange(half, dtype=F32) * 2.0 / ROT_DIM)
    ang = pos.astype(F32)[:, None] * inv[None, :]
    cos, sin = jnp.cos(ang), jnp.sin(ang)
    t = pos.shape[0]
    rest = A_HEAD_DIM - ROT_DIM
    ch = jnp.concatenate([cos, cos, jnp.ones((t, rest), F32)], axis=1)
    ah = jnp.concatenate([-sin, jnp.zeros((t, half + rest), F32)], axis=1)
    bh = jnp.concatenate([jnp.zeros((t, half), F32), sin, jnp.zeros((t, rest), F32)], axis=1)
    rep = LANES // A_HEAD_DIM
    return tuple(jnp.tile(x, (1, rep)) for x in (ch, ah, bh))


def _pad_rows(x, rows):
    return jnp.concatenate([x, jnp.zeros((rows - x.shape[0],) + x.shape[1:], x.dtype)], axis=0)


def kernel(x_prompt, x_sample, cache_a_k, cache_a_v, state_c_conv, state_d_pool, state_ffn_conv, c_prompt, c_sample, norm_g, ada_w, ada_b, w_in_e, w_out_e, a_sinks, b_ln_g, b_ln_b, b_ws, b_bias, w_in_o, w_out_o, c_conv_w, c_conv_b, c_ln_g, c_ln_b, d_w, d_b, d_scale, ffn_w_up, ffn_conv_w, ffn_conv_b, ffn_w_down):
    assert DEPTH == 2 and x_prompt.shape == (BATCH, SEQ, D_MODEL) and x_sample.shape == (DEC_BATCH, DEC_SEQ, D_MODEL)
    w_in_e_b, w_out_e_b = w_in_e.astype(BF16), w_out_e.astype(BF16)
    w_in_o_b, w_out_o_b = w_in_o.astype(BF16), w_out_o.astype(BF16)
    w_up_b, w_down_b = ffn_w_up.astype(BF16), ffn_w_down.astype(BF16)

    pad_rows = DEC_BATCH + SUBLANES
    c_all = _pad_rows(jnp.concatenate([c_sample, c_prompt], axis=0), pad_rows)
    mods = _ada(c_all, ada_w.reshape(2 * DEPTH, D_MODEL, 3 * D_MODEL), ada_b.reshape(2 * DEPTH, 1, 3 * D_MODEL))
    mods_p = mods[:, DEC_BATCH:DEC_BATCH + BATCH].reshape(2 * DEPTH, BATCH, 1, 3 * D_MODEL)
    gains = norm_g.reshape(4 * DEPTH, 1, D_MODEL)

    grp_p = _Group(BATCH * SEQ, 512, mods_p, SEQ)
    grp_s = _Group(DEC_BATCH * DEC_SEQ, DEC_BATCH * DEC_SEQ, mods, None)
    xp = x_prompt.reshape(BATCH * SEQ, D_MODEL)
    xs = jnp.transpose(x_sample, (1, 0, 2)).reshape(DEC_SEQ * DEC_BATCH, D_MODEL)

    tabs_p = _rope_tables(jnp.arange(SEQ))
    tabs_s = tuple(_pad_rows(x, SUBLANES) for x in _rope_tables(PAST_LEN + jnp.arange(DEC_SEQ)))

    row1 = lambda v: v.reshape(1, -1)
    bexp_p = jnp.repeat(jnp.transpose(b_bias[0]), B_HEAD_DIM, axis=1)
    ws4 = jnp.tril(b_ws[0])[:, :DEC_SEQ, :DEC_SEQ]
    wexp_s = jnp.repeat(jnp.transpose(ws4, (1, 2, 0)).reshape(DEC_SEQ * DEC_SEQ, B_HEADS), B_HEAD_DIM, axis=1)
    bexp_s = _pad_rows(bexp_p[:DEC_SEQ], SUBLANES)
    sinkrow = jnp.repeat(a_sinks[0], DEC_SEQ).reshape(A_HEADS * DEC_SEQ, 1)

    ffn_cb = ffn_conv_b.reshape(DEPTH, 1, 2 * D_FF)

    def run(grp, x, is_prompt):
        outs = {}
        tag = "p" if is_prompt else "s"
        (h,) = _resnorm(grp, x, None, gains, nxt=(0, 0))
        tm_mm = 1024 if is_prompt else grp.m
        z = _mm(h, w_in_e_b, 0, tm_mm, 896, "in_even_" + tag)
        if is_prompt:
            mix, nk, nv = _even_p(z.reshape(BATCH, SEQ, EVEN_IN), a_sinks[0], tabs_p, b_ws[0], bexp_p,
                                  row1(b_ln_g[0]), row1(b_ln_b[0]))
            outs["ak"], outs["av"] = nk, nv
        else:
            mix, nk, nv, vb = _even_s(z.reshape(DEC_SEQ, DEC_BATCH, EVEN_IN),
                                      cache_a_k[0].reshape(DEC_BATCH * WINDOW, A_KV_WIDTH),
                                      cache_a_v[0].reshape(DEC_BATCH * WINDOW, A_KV_WIDTH),
                                      sinkrow, tabs_s, wexp_s, bexp_s, row1(b_ln_g[0]), row1(b_ln_b[0]))
            outs["ak"], outs["av"], outs["bv"] = nk, nv, vb
        x, h = _mm_res(grp, mix.reshape(grp.m, D_MODEL), w_out_e_b, 0, x, gains, (0, 1), (1, 2), "out_even_" + tag)
        x, h, outs["ff0"] = ffn(grp, x, h, 0, is_prompt, nxt=(2, 4))
        z = _mm(h, w_in_o_b, 0, tm_mm, 1024, "in_odd_" + tag)
        dwa, dba, dsa = d_w[0], row1(d_b[0]), row1(d_scale[0])
        cargs = (c_conv_w[0], row1(c_conv_b[0]), row1(c_ln_g[0]), row1(c_ln_b[0]), dwa, dba, dsa)
        if is_prompt:
            mix, ct, dt = _odd_p(z.reshape(BATCH, SEQ, ODD_IN), *cargs)
            outs["cc"] = ct[:, C_TAIL - (C_CONV_WIDTH - 1):]
            outs["dp"] = dt[:, D_TAIL - (POOL_MAX - 1):]
        else:
            mix, co, do = _odd_s(z.reshape(DEC_SEQ, DEC_BATCH, ODD_IN),
                                 jnp.transpose(state_c_conv[0], (1, 0, 2)),
                                 jnp.transpose(state_d_pool[0], (1, 0, 2)), *cargs)
            outs["cc"] = jnp.transpose(co, (1, 0, 2))
            outs["dp"] = jnp.transpose(do, (1, 0, 2))
        x, h = _mm_res(grp, mix.reshape(grp.m, D_MODEL), w_out_o_b, 0, x, gains, (2, 5), (3, 6), "out_odd_" + tag)
        x, _, outs["ff1"] = ffn(grp, x, h, 1, is_prompt, nxt=None)
        return x, outs

    def ffn(grp, x, h, i, is_prompt, nxt):
        nst = FFN_CONV_WIDTH - 1
        if is_prompt:
            out, nfg, nfv = _ffn_p(h, BATCH, SEQ, w_up_b, w_down_b, ffn_conv_w, ffn_cb, i, "ffn_p")
            nf = jnp.concatenate([nfg[:, -1, SUBLANES - nst:], nfv[:, -1, SUBLANES - nst:]], axis=-1)
        else:
            st = jnp.transpose(state_ffn_conv[i], (1, 0, 2))
            out, nsg, nsv = _ffn_s(h, st, w_up_b, w_down_b, ffn_conv_w, ffn_cb, i, "ffn_s")
            nf = jnp.transpose(jnp.concatenate([nsg, nsv], axis=-1), (1, 0, 2))
        res = _resnorm(grp, x, out, gains, res=(2 * i + 1, 4 * i + 3), nxt=nxt)
        if nxt is None:
            return res[0], None, nf
        return res[0], res[1], nf

    yp, op = run(grp_p, xp, True)
    ys, os_ = run(grp_s, xs, False)

    kv5 = lambda a, nb: a.reshape(1, nb, WINDOW, A_KV_HEADS, A_HEAD_DIM)
    y_prompt = yp.reshape(BATCH, SEQ, D_MODEL)
    y_sample = jnp.transpose(ys.reshape(DEC_SEQ, DEC_BATCH, D_MODEL), (1, 0, 2))
    return (y_prompt, y_sample,
            kv5(op["ak"], BATCH), kv5(os_["ak"], DEC_BATCH), kv5(op["av"], BATCH), kv5(os_["av"], DEC_BATCH),
            jnp.transpose(os_["bv"], (1, 0, 2))[None],
            op["cc"][None], os_["cc"][None], op["dp"][None], os_["dp"][None],
            jnp.stack([op["ff0"], op["ff1"]]), jnp.stack([os_["ff0"], os_["ff1"]]))
```

```python
import functools

import jax
import jax.numpy as jnp
from jax import lax
from jax.experimental import pallas as pl
from jax.experimental.pallas import tpu as pltpu

D_MODEL = 2048
BATCH = 2
SEQ = 4096
DEPTH = 2
DEC_BATCH = 128
DEC_SEQ = 4
PAST_LEN = 8192
A_HEADS = 16
A_KV_HEADS = 4
A_HEAD_DIM = 64
A_Q_WIDTH = A_HEADS * A_HEAD_DIM
A_KV_WIDTH = A_KV_HEADS * A_HEAD_DIM
WINDOW = 128
ROT_DIM = A_HEAD_DIM // 4
ROPE_THETA = 500000.0
B_HEADS = 8
B_HEAD_DIM = 128
B_WIDTH = B_HEADS * B_HEAD_DIM
CHUNK = 128
C_WIDTH = 1024
C_CONV_WIDTH = 31
POOL_SIZES = (2, 4, 8, 16)
D_GROUPS = len(POOL_SIZES)
POOL_MAX = max(POOL_SIZES)
D_WIDTH = 1024
D_GROUP_WIDTH = D_WIDTH // D_GROUPS
D_FF = 5632
FFN_CONV_WIDTH = 3
EVEN_IN = A_Q_WIDTH + 2 * A_KV_WIDTH + 2 * B_WIDTH
ODD_IN = 2 * C_WIDTH + D_WIDTH
EPS = 1e-6
NEG_INF = -1e30

LANES = 128
SUBLANES = 8
VMEM_LIMIT = 56 * 1024 * 1024

BF16 = jnp.bfloat16
F32 = jnp.float32

SEQ_BLOCK = 32
C_TAIL = 32
D_TAIL = 16


def _params(n_axes):
    return pltpu.CompilerParams(dimension_semantics=("arbitrary",) * n_axes,
                                vmem_limit_bytes=VMEM_LIMIT)


def _gelu_tanh(x):
    return 0.5 * x * (1.0 + jnp.tanh(0.7978845608028654 * (x + 0.044715 * (x * x * x))))


def _rms(x, g):
    return x * lax.rsqrt(jnp.mean(x * x, axis=-1, keepdims=True) + EPS) * g


def _layer_norm(x, g, b):
    mu = jnp.mean(x, axis=-1, keepdims=True)
    xc = x - mu
    var = jnp.mean(xc * xc, axis=-1, keepdims=True)
    return xc * lax.rsqrt(var + EPS) * g + b


def _rows(m, tm):
    r = m.shape[0]
    if r == 1 or r == tm:
        return m
    return jnp.concatenate([m] * (tm // r), axis=0)


def _rope(x, c, a, b):
    return x * c + pltpu.roll(x, LANES - ROT_DIM // 2, 1) * a + pltpu.roll(x, ROT_DIM // 2, 1) * b


def _ada_kernel(c_ref, w_ref, b_ref, o_ref):
    c = c_ref[...]
    s = (c * jax.nn.sigmoid(c)).astype(BF16)
    o_ref[...] = jnp.dot(s, w_ref[...].astype(BF16), preferred_element_type=F32) + b_ref[...]


def _ada(c_all, w, b):
    r = c_all.shape[0]
    tn = 1024
    n_sub, _, n = w.shape
    return pl.pallas_call(
        _ada_kernel,
        grid=(n_sub, n // tn),
        in_specs=[pl.BlockSpec((r, D_MODEL), lambda k, j: (0, 0)),
                  pl.BlockSpec((None, D_MODEL, tn), lambda k, j: (k, 0, j)),
                  pl.BlockSpec((None, 1, tn), lambda k, j: (k, 0, j))],
        out_specs=pl.BlockSpec((None, r, tn), lambda k, j: (k, 0, j)),
        out_shape=jax.ShapeDtypeStruct((n_sub, r, n), F32),
        compiler_params=_params(2),
        name="ada",
    )(c_all, w, b)


def _resnorm_kernel(*refs, has_res, has_next):
    it = iter(refs)
    x_ref = next(it)
    if has_res:
        o_ref, gate_ref, gpost_ref = next(it), next(it), next(it)
    if has_next:
        gpre_ref, scale_ref, shift_ref = next(it), next(it), next(it)
    if has_res:
        xo_ref = next(it)
    if has_next:
        ho_ref = next(it)
    x = x_ref[...]
    tm = x.shape[0]
    if has_res:
        x = x + _rows(gate_ref[...], tm) * _rms(o_ref[...], gpost_ref[...])
        xo_ref[...] = x
    if has_next:
        h = _rms(x, gpre_ref[...]) * (1.0 + _rows(scale_ref[...], tm)) + _rows(shift_ref[...], tm)
        ho_ref[...] = h.astype(BF16)


class _Group:
    def __init__(self, m, tm, mods, per_batch_rows):
        self.m, self.tm, self.mods = m, tm, mods
        self.per_batch_rows = per_batch_rows

    def mod_spec(self, k, col):
        if self.per_batch_rows is None:
            return pl.BlockSpec((None, DEC_BATCH, D_MODEL), lambda i: (k, 0, col))
        tpb = self.per_batch_rows // self.tm
        return pl.BlockSpec((None, None, 1, D_MODEL), lambda i: (k, i // tpb, 0, col))


def _resnorm(grp, x, out, gains, res=None, nxt=None):
    tm = grp.tm
    row = pl.BlockSpec((tm, D_MODEL), lambda i: (i, 0))
    args, specs, out_shapes, out_specs = [x], [row], [], []

    def gain_spec(idx):
        return pl.BlockSpec((None, 1, D_MODEL), lambda i: (idx, 0, 0))

    if res is not None:
        args += [out, grp.mods, gains]
        specs += [row, grp.mod_spec(res[0], 2), gain_spec(res[1])]
        out_shapes.append(jax.ShapeDtypeStruct((grp.m, D_MODEL), F32))
        out_specs.append(row)
    if nxt is not None:
        args += [gains, grp.mods, grp.mods]
        specs += [gain_spec(nxt[1]), grp.mod_spec(nxt[0], 1), grp.mod_spec(nxt[0], 0)]
        out_shapes.append(jax.ShapeDtypeStruct((grp.m, D_MODEL), BF16))
        out_specs.append(row)
    res_out = pl.pallas_call(
        functools.partial(_resnorm_kernel, has_res=res is not None, has_next=nxt is not None),
        grid=(grp.m // tm,),
        in_specs=specs, out_specs=out_specs, out_shape=out_shapes,
        compiler_params=_params(1),
        name="resnorm",
    )(*args)
    return res_out


def _mm_kernel(a_ref, w_ref, o_ref):
    o_ref[...] = jnp.dot(a_ref[...], w_ref[...], preferred_element_type=F32).astype(o_ref.dtype)


def _mm(a, w, layer, tm, tn, name, out_dtype=F32):
    m, k = a.shape
    n = w.shape[2]
    return pl.pallas_call(
        _mm_kernel,
        grid=(n // tn, m // tm),
        in_specs=[pl.BlockSpec((tm, k), lambda j, i: (i, 0)),
                  pl.BlockSpec((None, k, tn), lambda j, i: (layer, 0, j))],
        out_specs=pl.BlockSpec((tm, tn), lambda j, i: (i, j)),
        out_shape=jax.ShapeDtypeStruct((m, n), out_dtype),
        compiler_params=_params(2),
        name=name,
    )(a, w)


MM_RES_SPLIT = 2


def _mm_res_kernel(a_ref, w_ref, x_ref, gate_ref, gpost_ref, gpre_ref, scale_ref, shift_ref, xo_ref, ho_ref):
    tm = a_ref.shape[0]
    rs = max(tm // MM_RES_SPLIT, gate_ref.shape[0])
    outs = [jnp.dot(a_ref[r:r + rs, :], w_ref[...], preferred_element_type=F32) for r in range(0, tm, rs)]
    for r, out in zip(range(0, tm, rs), outs):
        x = x_ref[r:r + rs, :] + _rows(gate_ref[...], rs) * _rms(out, gpost_ref[...])
        xo_ref[r:r + rs, :] = x
        h = _rms(x, gpre_ref[...]) * (1.0 + _rows(scale_ref[...], rs)) + _rows(shift_ref[...], rs)
        ho_ref[r:r + rs, :] = h.astype(BF16)


def _mm_res(grp, a, w, layer, x, gains, res, nxt, name):
    tm = grp.tm
    k = a.shape[1]
    row = pl.BlockSpec((tm, D_MODEL), lambda i: (i, 0))
    gain_spec = lambda idx: pl.BlockSpec((None, 1, D_MODEL), lambda i: (idx, 0, 0))
    return pl.pallas_call(
        _mm_res_kernel,
        grid=(grp.m // tm,),
        in_specs=[pl.BlockSpec((tm, k), lambda i: (i, 0)),
                  pl.BlockSpec((None, k, D_MODEL), lambda i: (layer, 0, 0)),
                  row, grp.mod_spec(res[0], 2), gain_spec(res[1]),
                  gain_spec(nxt[1]), grp.mod_spec(nxt[0], 1), grp.mod_spec(nxt[0], 0)],
        out_specs=[row, row],
        out_shape=[jax.ShapeDtypeStruct((grp.m, D_MODEL), F32), jax.ShapeDtypeStruct((grp.m, D_MODEL), BF16)],
        compiler_params=_params(1),
        name=name,
    )(a, w, x, grp.mods, gains, gains, grp.mods, grp.mods)


CORE_ROWS = 512
CORE_HALF = CORE_ROWS // 2


PROJ_CHUNK = 512


def _proj_chunks(dst, lhs_ref, r0, w_ref):
    def make(c):
        def run():
            cols = slice(c, c + PROJ_CHUNK)
            dst[:, cols] = jnp.dot(lhs_ref[r0:r0 + CORE_HALF, :], w_ref[:, cols], preferred_element_type=F32)
        return run
    return iter([make(c) for c in range(0, w_ref.shape[1], PROJ_CHUNK)])


def _issue(pre, n=1):
    for _ in range(n):
        thunk = next(pre, None)
        if thunk is not None:
            thunk()


def _even_block(pre, z, r0, o0, first_lim, sink_ref, cos_ref, sa_ref, sb_ref, ws_ref, bexp_ref, lng_ref, lnb_ref,
                mix_ref, nk_ref, nv_ref, kprev, vprev):
    qb = WINDOW
    zr = slice(r0, r0 + qb)
    orow = slice(o0, o0 + qb)
    cos_t, sa_t, sb_t = cos_ref[orow, :], sa_ref[orow, :], sb_ref[orow, :]
    lane = lax.broadcasted_iota(jnp.int32, (qb, LANES), 1)
    lo = lane < A_HEAD_DIM
    lane2 = lax.broadcasted_iota(jnp.int32, (2 * qb, LANES), 1)
    lo2 = lane2 < A_HEAD_DIM
    row = lax.broadcasted_iota(jnp.int32, (qb, 2 * qb), 0)
    col = lax.broadcasted_iota(jnp.int32, (qb, 2 * qb), 1)
    mask = ((col > row) & (col < qb)) | ((col >= qb) & (col - qb <= row))
    if first_lim is not None:
        mask = mask & (col >= first_lim)
    nt = (((1,), (1,)), ((), ()))

    for jk in range(A_KV_HEADS // 2):
        _issue(pre)
        ksl = slice(A_Q_WIDTH + LANES * jk, A_Q_WIDTH + LANES * (jk + 1))
        vsl = slice(A_Q_WIDTH + A_KV_WIDTH + LANES * jk, A_Q_WIDTH + A_KV_WIDTH + LANES * (jk + 1))
        csl = slice(LANES * jk, LANES * (jk + 1))
        kc = _rope(z[zr, ksl], cos_t, sa_t, sb_t)
        vc = z[zr, vsl]
        kall = jnp.concatenate([kprev[:, csl], kc], axis=0)
        vall = jnp.concatenate([vprev[:, csl], vc], axis=0)
        krol = pltpu.roll(kall, A_HEAD_DIM, 1)
        vrol = pltpu.roll(vall, A_HEAD_DIM, 1)
        for sub in range(2):
            kvh = 2 * jk + sub
            kd = (jnp.where(lo2, kall, krol) if sub == 0 else jnp.where(lo2, krol, kall)).astype(BF16)
            vd = (jnp.where(lo2, vall, vrol) if sub == 0 else jnp.where(lo2, vrol, vall)).astype(BF16)
            for qs in range(2):
                js = 2 * kvh + qs
                qsl = _rope(z[zr, LANES * js:LANES * (js + 1)], cos_t, sa_t, sb_t) * (A_HEAD_DIM ** -0.5)
                outs = []
                for half in range(2):
                    h = 2 * js + half
                    qm = jnp.where(lo if half == 0 else jnp.logical_not(lo), qsl, 0.0).astype(BF16)
                    s = lax.dot_general(qm, kd, nt, preferred_element_type=F32)
                    s = jnp.where(mask, s, NEG_INF)
                    sink = sink_ref[h]
                    m = jnp.maximum(jnp.max(s, axis=-1, keepdims=True), sink)
                    p = jnp.exp(s - m)
                    den = jnp.sum(p, axis=-1, keepdims=True) + jnp.exp(sink - m)
                    o = jnp.dot(p.astype(BF16), vd, preferred_element_type=F32)
                    outs.append(o / den)
                mix_ref[orow, LANES * js:LANES * (js + 1)] = jnp.where(lo, outs[0], outs[1]).astype(BF16)
        kprev[:, csl] = kc
        vprev[:, csl] = vc
        nk_ref[:, csl] = kc
        nv_ref[:, csl] = vc

    _issue(pre, 2)
    zb0 = A_Q_WIDTH + 2 * A_KV_WIDTH
    u = _gelu_tanh(z[zr, zb0:zb0 + B_WIDTH])
    vb = _layer_norm(_gelu_tanh(z[zr, zb0 + B_WIDTH:zb0 + 2 * B_WIDTH]), lng_ref[...], lnb_ref[...])
    ri = lax.broadcasted_iota(jnp.int32, (CHUNK, CHUNK), 0)
    ci = lax.broadcasted_iota(jnp.int32, (CHUNK, CHUNK), 1)
    tri = ri >= ci
    for h in range(B_HEADS):
        hs = slice(B_HEAD_DIM * h, B_HEAD_DIM * (h + 1))
        w = jnp.where(tri, ws_ref[h], 0.0).astype(BF16)
        mixed = jnp.dot(w, vb[:, hs].astype(BF16), preferred_element_type=F32) + bexp_ref[:, hs]
        mix_ref[orow, A_Q_WIDTH + B_HEAD_DIM * h:A_Q_WIDTH + B_HEAD_DIM * (h + 1)] = (u[:, hs] * mixed).astype(BF16)


def _even_p_kernel(sink_ref, hc_ref, hn_ref, w_ref, cos_ref, sa_ref, sb_ref, ws_ref, bexp_ref, lng_ref, lnb_ref,
                   mix_ref, nk_ref, nv_ref, za, zb, kprev, vprev):
    step = pl.program_id(0)
    b = step % (SEQ // CORE_ROWS)
    args = (sink_ref, cos_ref, sa_ref, sb_ref, ws_ref, bexp_ref, lng_ref, lnb_ref, mix_ref, nk_ref, nv_ref,
            kprev, vprev)

    @pl.when(step == 0)
    def _():
        za[...] = jnp.dot(hc_ref[0:CORE_HALF, :], w_ref[...], preferred_element_type=F32)

    @pl.when(b == 0)
    def _():
        kprev[...] = jnp.zeros_like(kprev)
        vprev[...] = jnp.zeros_like(vprev)

    pre = _proj_chunks(zb, hc_ref, CORE_HALF, w_ref)
    _even_block(pre, za, 0, 0, jnp.where(b > 0, 0, WINDOW), *args)
    _even_block(pre, za, WINDOW, WINDOW, None, *args)
    _issue(pre, EVEN_IN // PROJ_CHUNK)
    pre = _proj_chunks(za, hn_ref, 0, w_ref)
    _even_block(pre, zb, 0, CORE_HALF, None, *args)
    _even_block(pre, zb, WINDOW, CORE_HALF + WINDOW, None, *args)
    _issue(pre, EVEN_IN // PROJ_CHUNK)


def _even_p(h, w, layer, sinks, tabs, ws, bexp, lng, lnb):
    rows = CORE_ROWS
    steps = h.shape[0] // rows
    per_seq = SEQ // rows
    last_half = h.shape[0] // CORE_HALF - 1
    full = lambda shape: pl.BlockSpec(shape, lambda s: (0,) * len(shape))
    tab = pl.BlockSpec((rows, LANES), lambda s: (s % per_seq, 0))
    kv = pl.BlockSpec((None, WINDOW, A_KV_WIDTH), lambda s: (s // per_seq, 0, 0))
    return pl.pallas_call(
        _even_p_kernel,
        grid=(steps,),
        in_specs=[pl.BlockSpec(memory_space=pltpu.SMEM),
                  pl.BlockSpec((rows, D_MODEL), lambda s: (s, 0)),
                  pl.BlockSpec((CORE_HALF, D_MODEL), lambda s: (jnp.minimum(2 * s + 2, last_half), 0)),
                  pl.BlockSpec((None, D_MODEL, EVEN_IN), lambda s: (layer, 0, 0)),
                  tab, tab, tab,
                  full((B_HEADS, CHUNK, CHUNK)),
                  pl.BlockSpec((CHUNK, B_WIDTH), lambda s: (0, 0)),
                  full((1, B_WIDTH)), full((1, B_WIDTH))],
        out_specs=[pl.BlockSpec((rows, D_MODEL), lambda s: (s, 0)), kv, kv],
        out_shape=[jax.ShapeDtypeStruct((h.shape[0], D_MODEL), BF16),
                   jax.ShapeDtypeStruct((BATCH, WINDOW, A_KV_WIDTH), F32),
                   jax.ShapeDtypeStruct((BATCH, WINDOW, A_KV_WIDTH), F32)],
        scratch_shapes=[pltpu.VMEM((CORE_HALF, EVEN_IN), F32), pltpu.VMEM((CORE_HALF, EVEN_IN), F32),
                        pltpu.VMEM((WINDOW, A_KV_WIDTH), F32), pltpu.VMEM((WINDOW, A_KV_WIDTH), F32)],
        compiler_params=_params(1),
        name="even_p",
    )(sinks, h, h, w, *tabs, ws, bexp, lng, lnb)


def _even_s_kernel(sinkrow_ref, z_ref, ck_ref, cv_ref, cos_ref, sa_ref, sb_ref, wexp_ref, bexp_ref,
                   lng_ref, lnb_ref, mix_ref, ok_ref, ov_ref, vb_ref, qm, osc, kn, vn):
    s_blk = SEQ_BLOCK
    rows_per_seq = A_HEADS * DEC_SEQ
    lane256 = lax.broadcasted_iota(jnp.int32, (s_blk, A_KV_WIDTH), 1)
    kn[...] = jnp.zeros_like(kn)
    vn[...] = jnp.zeros_like(vn)
    halves = A_KV_WIDTH // LANES

    def put(ref, start, stride, val):
        for c in range(halves):
            ref[c, pl.ds(start, s_blk, stride=stride), :] = val[:, LANES * c:LANES * (c + 1)]

    def get_rows(ref, start, size):
        return jnp.concatenate([ref[c, pl.ds(start, size), :] for c in range(halves)], axis=1)

    def get_strided(ref, start, stride):
        return jnp.concatenate([ref[c, pl.ds(start, s_blk, stride=stride), :] for c in range(halves)], axis=1)

    for t in range(DEC_SEQ):
        c, a, bb = cos_ref[t:t + 1, :], sa_ref[t:t + 1, :], sb_ref[t:t + 1, :]

        def rope2(lo_lane):
            return jnp.concatenate([_rope(z_ref[t, :, lo_lane:lo_lane + LANES], c, a, bb),
                                    _rope(z_ref[t, :, lo_lane + LANES:lo_lane + 2 * LANES], c, a, bb)], axis=1)

        put(kn, t, SUBLANES, rope2(A_Q_WIDTH))
        put(vn, t, SUBLANES, z_ref[t, :, A_Q_WIDTH + A_KV_WIDTH:A_Q_WIDTH + 2 * A_KV_WIDTH])
        for j in range(A_KV_HEADS):
            qs = rope2(A_KV_WIDTH * j) * (A_HEAD_DIM ** -0.5)
            keep = (lane256 >= A_HEAD_DIM * j) & (lane256 < A_HEAD_DIM * (j + 1))
            for g in range(A_HEADS // A_KV_HEADS):
                sh = ((j - g) * A_HEAD_DIM) % A_KV_WIDTH
                qr = qs if sh == 0 else pltpu.roll(qs, sh, 1)
                r = (4 * j + g) * DEC_SEQ + t
                put(qm, r, rows_per_seq, jnp.where(keep, qr, 0.0))

    rr = lax.broadcasted_iota(jnp.int32, (rows_per_seq, WINDOW), 0) % DEC_SEQ
    cc = lax.broadcasted_iota(jnp.int32, (rows_per_seq, WINDOW), 1)
    mask_old = cc > rr
    new0 = WINDOW - DEC_SEQ
    mask_new = (cc >= new0) & (cc - new0 <= rr)
    row8 = lax.broadcasted_iota(jnp.int32, (SUBLANES, A_KV_WIDTH), 0)
    sink = sinkrow_ref[...]
    nt = (((1,), (1,)), ((), ()))

    def shifted(c_ref, new_ref, o_ref, n):
        base = pl.multiple_of(n * WINDOW, WINDOW)
        old = c_ref[pl.ds(base, WINDOW), :]
        rolled = pltpu.roll(old, WINDOW - DEC_SEQ, 0)
        fresh = get_rows(new_ref, pl.multiple_of(n * SUBLANES, SUBLANES), SUBLANES)
        tail = jnp.where(row8 >= SUBLANES - DEC_SEQ, pltpu.roll(fresh, SUBLANES - DEC_SEQ, 0),
                         rolled[WINDOW - SUBLANES:, :])
        new = jnp.concatenate([rolled[:WINDOW - SUBLANES, :], tail], axis=0)
        o_ref[pl.ds(base, WINDOW), :] = new
        return old.astype(BF16), new.astype(BF16)

    def body(n, carry):
        k_old, k_new = shifted(ck_ref, kn, ok_ref, n)
        v_old, v_new = shifted(cv_ref, vn, ov_ref, n)
        q = get_rows(qm, pl.multiple_of(n * rows_per_seq, rows_per_seq), rows_per_seq).astype(BF16)
        s_old = jnp.where(mask_old, lax.dot_general(q, k_old, nt, preferred_element_type=F32), NEG_INF)
        s_new = jnp.where(mask_new, lax.dot_general(q, k_new, nt, preferred_element_type=F32), NEG_INF)
        m = jnp.maximum(jnp.maximum(jnp.max(s_old, axis=-1, keepdims=True),
                                    jnp.max(s_new, axis=-1, keepdims=True)), sink)
        p_old = jnp.exp(s_old - m)
        p_new = jnp.exp(s_new - m)
        den = (jnp.sum(p_old, axis=-1, keepdims=True) + jnp.sum(p_new, axis=-1, keepdims=True)
               + jnp.exp(sink - m))
        o = (jnp.dot(p_old.astype(BF16), v_old, preferred_element_type=F32)
             + jnp.dot(p_new.astype(BF16), v_new, preferred_element_type=F32))
        o = o / den
        for c in range(halves):
            osc[c, pl.ds(pl.multiple_of(n * rows_per_seq, rows_per_seq), rows_per_seq), :] = o[:, LANES * c:LANES * (c + 1)]
        return carry

    lax.fori_loop(0, s_blk, body, 0, unroll=4)

    for t in range(DEC_SEQ):
        for j in range(A_KV_HEADS):
            acc = jnp.zeros((s_blk, A_KV_WIDTH), F32)
            for g in range(A_HEADS // A_KV_HEADS):
                r = (4 * j + g) * DEC_SEQ + t
                ov = get_strided(osc, r, rows_per_seq)
                sh = ((g - j) * A_HEAD_DIM) % A_KV_WIDTH
                orr = ov if sh == 0 else pltpu.roll(ov, sh, 1)
                acc = jnp.where((lane256 >= A_HEAD_DIM * g) & (lane256 < A_HEAD_DIM * (g + 1)), orr, acc)
            mix_ref[t, :, A_KV_WIDTH * j:A_KV_WIDTH * (j + 1)] = acc.astype(BF16)

    zb0 = A_Q_WIDTH + 2 * A_KV_WIDTH
    us, vbs = [], []
    for t in range(DEC_SEQ):
        us.append(_gelu_tanh(z_ref[t, :, zb0:zb0 + B_WIDTH]))
        v = _layer_norm(_gelu_tanh(z_ref[t, :, zb0 + B_WIDTH:zb0 + 2 * B_WIDTH]), lng_ref[...], lnb_ref[...])
        vb_ref[t] = v
        vbs.append(v)
    for t in range(DEC_SEQ):
        mixed = bexp_ref[t:t + 1, :]
        for s in range(t + 1):
            mixed = mixed + wexp_ref[DEC_SEQ * t + s:DEC_SEQ * t + s + 1, :] * vbs[s]
        mix_ref[t, :, A_Q_WIDTH:A_Q_WIDTH + B_WIDTH] = (us[t] * mixed).astype(BF16)


def _even_s(z3, ck, cv, sinkrow, tabs, wexp, bexp, lng, lnb):
    s_blk = SEQ_BLOCK
    rows_per_seq = A_HEADS * DEC_SEQ
    full = lambda shape: pl.BlockSpec(shape, lambda s: (0,) * len(shape))
    cache = pl.BlockSpec((s_blk * WINDOW, A_KV_WIDTH), lambda s: (s, 0))
    return pl.pallas_call(
        _even_s_kernel,
        grid=(DEC_BATCH // s_blk,),
        in_specs=[full((rows_per_seq, 1)),
                  pl.BlockSpec((DEC_SEQ, s_blk, EVEN_IN), lambda s: (0, s, 0)),
                  cache, cache,
                  full((SUBLANES, LANES)), full((SUBLANES, LANES)), full((SUBLANES, LANES)),
                  full((DEC_SEQ * DEC_SEQ, B_WIDTH)), full((SUBLANES, B_WIDTH)),
                  full((1, B_WIDTH)), full((1, B_WIDTH))],
        out_specs=[pl.BlockSpec((DEC_SEQ, s_blk, D_MODEL), lambda s: (0, s, 0)),
                   cache, cache,
                   pl.BlockSpec((DEC_SEQ, s_blk, B_WIDTH), lambda s: (0, s, 0))],
        out_shape=[jax.ShapeDtypeStruct((DEC_SEQ, DEC_BATCH, D_MODEL), BF16),
                   jax.ShapeDtypeStruct((DEC_BATCH * WINDOW, A_KV_WIDTH), F32),
                   jax.ShapeDtypeStruct((DEC_BATCH * WINDOW, A_KV_WIDTH), F32),
                   jax.ShapeDtypeStruct((DEC_SEQ, DEC_BATCH, B_WIDTH), F32)],
        scratch_shapes=[pltpu.VMEM((A_KV_WIDTH // LANES, s_blk * rows_per_seq, LANES), F32),
                        pltpu.VMEM((A_KV_WIDTH // LANES, s_blk * rows_per_seq, LANES), F32),
                        pltpu.VMEM((A_KV_WIDTH // LANES, s_blk * SUBLANES, LANES), F32),
                        pltpu.VMEM((A_KV_WIDTH // LANES, s_blk * SUBLANES, LANES), F32)],
        compiler_params=_params(1),
        name="even_s",
    )(sinkrow, z3, ck, cv, *tabs, wexp, bexp, lng, lnb)


ODD_ROWS = CORE_HALF
ODD_RC = 64


def _odd_block(pre, z_ref, o0, pos0, cw_ref, cb_ref, lng_ref, lnb_ref, dw_ref, db_ref, dsc_ref,
               mix_ref, ctail_ref, dtail_ref, gext, dext, cbuf, dbuf, shbuf):
    tr = ODD_ROWS
    _issue(pre, 2)
    gext[C_TAIL:C_TAIL + tr, :] = z_ref[:, 0:C_WIDTH] * jax.nn.sigmoid(z_ref[:, C_WIDTH:2 * C_WIDTH])
    dext[D_TAIL:D_TAIL + tr, :] = z_ref[:, 2 * C_WIDTH:2 * C_WIDTH + D_WIDTH]

    gw = D_GROUP_WIDTH
    lead = C_TAIL - (C_CONV_WIDTH - 1)
    sh_rows = C_TAIL + tr - SUBLANES
    for s in range(1, SUBLANES):
        shbuf[s, 0:sh_rows, :] = gext[s:s + sh_rows, :]

    def tap(off, r0, ls):
        a, s = divmod(off, SUBLANES)
        src = gext if s == 0 else shbuf.at[s]
        return src[r0 + SUBLANES * a:r0 + SUBLANES * a + ODD_RC, ls]

    for rc in range(tr // ODD_RC):
        r0 = rc * ODD_RC
        _issue(pre)
        for c in range(C_WIDTH // gw):
            ls = slice(gw * c, gw * (c + 1))
            acc = jnp.broadcast_to(cb_ref[:, ls], (ODD_RC, gw))
            for k in range(C_CONV_WIDTH):
                acc = acc + cw_ref[k:k + 1, ls] * tap(lead + k, r0, ls)
            cbuf[r0:r0 + ODD_RC, ls] = acc
        y = _layer_norm(cbuf[r0:r0 + ODD_RC, :], lng_ref[...], lnb_ref[...])
        mix_ref[o0 + r0:o0 + r0 + ODD_RC, 0:C_WIDTH] = (y * jax.nn.sigmoid(y)).astype(BF16)

        pos1 = (pos0 + r0 + 1 + lax.broadcasted_iota(jnp.int32, (ODD_RC, gw), 0))
        for g, wsz in enumerate(POOL_SIZES):
            ls = slice(gw * g, gw * (g + 1))
            win = dext[D_TAIL + r0:D_TAIL + r0 + ODD_RC, ls]
            for jj in range(1, wsz):
                win = win + dext[D_TAIL + r0 - jj:D_TAIL + r0 - jj + ODD_RC, ls]
            cnt = jnp.minimum(pos1, wsz).astype(F32)
            diff = win / cnt - dext[D_TAIL + r0:D_TAIL + r0 + ODD_RC, ls]
            dbuf[r0:r0 + ODD_RC, ls] = diff.astype(BF16)

    for g in range(D_GROUPS):
        ls = slice(gw * g, gw * (g + 1))
        o = jnp.dot(dbuf[:, ls], dw_ref[g].astype(BF16), preferred_element_type=F32) + db_ref[:, ls]
        mix_ref[o0:o0 + tr, C_WIDTH + gw * g:C_WIDTH + gw * (g + 1)] = (o * dsc_ref[:, ls]).astype(BF16)

    gext[0:C_TAIL, :] = gext[tr:tr + C_TAIL, :]
    dext[0:D_TAIL, :] = dext[tr:tr + D_TAIL, :]
    ctail_ref[...] = gext[0:C_TAIL, :]
    dtail_ref[...] = dext[0:D_TAIL, :]


def _odd_p_kernel(hc_ref, hn_ref, w_ref, cw_ref, cb_ref, lng_ref, lnb_ref, dw_ref, db_ref, dsc_ref,
                  mix_ref, ctail_ref, dtail_ref, za, zb, gext, dext, cbuf, dbuf, shbuf):
    step = pl.program_id(0)
    b = step % (SEQ // CORE_ROWS)
    args = (cw_ref, cb_ref, lng_ref, lnb_ref, dw_ref, db_ref, dsc_ref, mix_ref, ctail_ref, dtail_ref,
            gext, dext, cbuf, dbuf, shbuf)

    @pl.when(step == 0)
    def _():
        za[...] = jnp.dot(hc_ref[0:CORE_HALF, :], w_ref[...], preferred_element_type=F32)

    @pl.when(b == 0)
    def _():
        gext[0:C_TAIL, :] = jnp.zeros((C_TAIL, C_WIDTH), F32)
        dext[0:D_TAIL, :] = jnp.zeros((D_TAIL, D_WIDTH), F32)

    pre = _proj_chunks(zb, hc_ref, CORE_HALF, w_ref)
    _odd_block(pre, za, 0, b * CORE_ROWS, *args)
    _issue(pre, ODD_IN // PROJ_CHUNK)
    pre = _proj_chunks(za, hn_ref, 0, w_ref)
    _odd_block(pre, zb, CORE_HALF, b * CORE_ROWS + CORE_HALF, *args)
    _issue(pre, ODD_IN // PROJ_CHUNK)


def _odd_p(h, w, layer, cw, cb, lng, lnb, dw, db, dsc):
    rows = CORE_ROWS
    steps = h.shape[0] // rows
    per_seq = SEQ // rows
    last_half = h.shape[0] // CORE_HALF - 1
    tr = ODD_ROWS
    full = lambda shape: pl.BlockSpec(shape, lambda s: (0,) * len(shape))
    return pl.pallas_call(
        _odd_p_kernel,
        grid=(steps,),
        in_specs=[pl.BlockSpec((rows, D_MODEL), lambda s: (s, 0)),
                  pl.BlockSpec((CORE_HALF, D_MODEL), lambda s: (jnp.minimum(2 * s + 2, last_half), 0)),
                  pl.BlockSpec((None, D_MODEL, ODD_IN), lambda s: (layer, 0, 0)),
                  full((C_CONV_WIDTH, C_WIDTH)), full((1, C_WIDTH)), full((1, C_WIDTH)), full((1, C_WIDTH)),
                  full((D_GROUPS, D_GROUP_WIDTH, D_GROUP_WIDTH)), full((1, D_WIDTH)), full((1, D_WIDTH))],
        out_specs=[pl.BlockSpec((rows, D_MODEL), lambda s: (s, 0)),
                   pl.BlockSpec((None, C_TAIL, C_WIDTH), lambda s: (s // per_seq, 0, 0)),
                   pl.BlockSpec((None, D_TAIL, D_WIDTH), lambda s: (s // per_seq, 0, 0))],
        out_shape=[jax.ShapeDtypeStruct((h.shape[0], D_MODEL), BF16),
                   jax.ShapeDtypeStruct((BATCH, C_TAIL, C_WIDTH), F32),
                   jax.ShapeDtypeStruct((BATCH, D_TAIL, D_WIDTH), F32)],
        scratch_shapes=[pltpu.VMEM((CORE_HALF, ODD_IN), F32), pltpu.VMEM((CORE_HALF, ODD_IN), F32),
                        pltpu.VMEM((C_TAIL + tr, C_WIDTH), F32), pltpu.VMEM((D_TAIL + tr, D_WIDTH), F32),
                        pltpu.VMEM((tr, C_WIDTH), F32), pltpu.VMEM((tr, D_WIDTH), BF16),
                        pltpu.VMEM((SUBLANES, C_TAIL + tr, C_WIDTH), F32)],
        compiler_params=_params(1),
        name="odd_p",
    )(h, h, w, cw, cb, lng, lnb, dw, db, dsc)


def _odd_s_kernel(z_ref, cs_ref, ds_ref, cw_ref, cb_ref, lng_ref, lnb_ref, dw_ref, db_ref, dsc_ref,
                  mix_ref, co_ref, do_ref, cbuf, dbuf):
    s_blk = SEQ_BLOCK
    gw = D_GROUP_WIDTH
    nc, nd = C_CONV_WIDTH - 1, POOL_MAX - 1
    for c in range(C_WIDTH // gw):
        ls = slice(gw * c, gw * (c + 1))
        accs = [jnp.broadcast_to(cb_ref[:, ls], (s_blk, gw)) for _ in range(DEC_SEQ)]
        for j in range(nc + DEC_SEQ):
            if j < nc:
                e = cs_ref[j, :, ls]
            else:
                t = j - nc
                e = z_ref[t, :, gw * c:gw * (c + 1)] * jax.nn.sigmoid(
                    z_ref[t, :, C_WIDTH + gw * c:C_WIDTH + gw * (c + 1)])
            if j >= DEC_SEQ:
                co_ref[j - DEC_SEQ, :, ls] = e
            for t in range(DEC_SEQ):
                k = j - t
                if 0 <= k < C_CONV_WIDTH:
                    accs[t] = accs[t] + cw_ref[k:k + 1, ls] * e
        for t in range(DEC_SEQ):
            cbuf[t, :, ls] = accs[t]
    for t in range(DEC_SEQ):
        y = _layer_norm(cbuf[t], lng_ref[...], lnb_ref[...])
        mix_ref[t, :, 0:C_WIDTH] = (y * jax.nn.sigmoid(y)).astype(BF16)

    for g, wsz in enumerate(POOL_SIZES):
        ls = slice(gw * g, gw * (g + 1))
        ext = []
        for j in range(nd + DEC_SEQ):
            if j < nd:
                e = ds_ref[j, :, ls]
            else:
                e = z_ref[j - nd, :, 2 * C_WIDTH + gw * g:2 * C_WIDTH + gw * (g + 1)]
            if j >= DEC_SEQ:
                do_ref[j - DEC_SEQ, :, ls] = e
            ext.append(e)
        for t in range(DEC_SEQ):
            win = ext[nd + t]
            for jj in range(1, wsz):
                win = win + ext[nd + t - jj]
            cnt = float(min(PAST_LEN + t + 1, wsz))
            dbuf[s_blk * t:s_blk * (t + 1), ls] = (win / cnt - ext[nd + t]).astype(BF16)
    for g in range(D_GROUPS):
        ls = slice(gw * g, gw * (g + 1))
        o = jnp.dot(dbuf[:, ls], dw_ref[g].astype(BF16), preferred_element_type=F32) + db_ref[:, ls]
        o = (o * dsc_ref[:, ls]).astype(BF16)
        for t in range(DEC_SEQ):
            mix_ref[t, :, C_WIDTH + gw * g:C_WIDTH + gw * (g + 1)] = o[s_blk * t:s_blk * (t + 1), :]


def _odd_s(z3, cs, ds, cw, cb, lng, lnb, dw, db, dsc):
    s_blk = SEQ_BLOCK
    nc, nd = C_CONV_WIDTH - 1, POOL_MAX - 1
    full = lambda shape: pl.BlockSpec(shape, lambda s: (0,) * len(shape))
    cst = pl.BlockSpec((nc, s_blk, C_WIDTH), lambda s: (0, s, 0))
    dst = pl.BlockSpec((nd, s_blk, D_WIDTH), lambda s: (0, s, 0))
    return pl.pallas_call(
        _odd_s_kernel,
        grid=(DEC_BATCH // s_blk,),
        in_specs=[pl.BlockSpec((DEC_SEQ, s_blk, ODD_IN), lambda s: (0, s, 0)), cst, dst,
                  full((C_CONV_WIDTH, C_WIDTH)), full((1, C_WIDTH)), full((1, C_WIDTH)), full((1, C_WIDTH)),
                  full((D_GROUPS, D_GROUP_WIDTH, D_GROUP_WIDTH)), full((1, D_WIDTH)), full((1, D_WIDTH))],
        out_specs=[pl.BlockSpec((DEC_SEQ, s_blk, D_MODEL), lambda s: (0, s, 0)), cst, dst],
        out_shape=[jax.ShapeDtypeStruct((DEC_SEQ, DEC_BATCH, D_MODEL), BF16),
                   jax.ShapeDtypeStruct((nc, DEC_BATCH, C_WIDTH), F32),
                   jax.ShapeDtypeStruct((nd, DEC_BATCH, D_WIDTH), F32)],
        scratch_shapes=[pltpu.VMEM((DEC_SEQ, s_blk, C_WIDTH), F32),
                        pltpu.VMEM((DEC_SEQ * s_blk, D_WIDTH), BF16)],
        compiler_params=_params(1),
        name="odd_s",
    )(z3, cs, ds, cw, cb, lng, lnb, dw, db, dsc)


FFN_TF = 512
FFN_TM = 1024


FFN_SPLIT = 4


def _ffn_p_kernel(h_ref, wg_ref, wv_ref, wd_ref, cwg_ref, cwv_ref, cbg_ref, cbv_ref,
                  o_ref, nfg_ref, nfv_ref, carry_g, carry_v):
    i, f = pl.program_id(1), pl.program_id(2)
    rh = FFN_TM // FFN_SPLIT
    row = lax.broadcasted_iota(jnp.int32, (rh, FFN_TF), 0)

    @pl.when(i == 0)
    def _():
        carry_g[f] = jnp.zeros((SUBLANES, FFN_TF), F32)
        carry_v[f] = jnp.zeros((SUBLANES, FFN_TF), F32)

    @pl.when(f == 0)
    def _():
        o_ref[...] = jnp.zeros_like(o_ref)

    def conv(up, prev, cw_ref, cb_ref):
        p1 = jnp.broadcast_to(prev[SUBLANES - 1:SUBLANES, :], (rh, FFN_TF))
        p2 = jnp.broadcast_to(prev[SUBLANES - 2:SUBLANES - 1, :], (rh, FFN_TF))
        m1 = jnp.where(row == 0, p1, pltpu.roll(up, 1, 0))
        m2 = jnp.where(row == 0, p2, jnp.where(row == 1, p1, pltpu.roll(up, 2, 0)))
        return cw_ref[2:3, :] * up + cw_ref[1:2, :] * m1 + cw_ref[0:1, :] * m2 + cb_ref[...]

    ups = []
    for s in range(FFN_SPLIT):
        hs = h_ref[rh * s:rh * (s + 1), :]
        ups.append((jnp.dot(hs, wg_ref[...], preferred_element_type=F32),
                    jnp.dot(hs, wv_ref[...], preferred_element_type=F32)))
    prev_g, prev_v = carry_g[f], carry_v[f]
    for s in range(FFN_SPLIT):
        ug, uv = ups[s]
        g = conv(ug, prev_g, cwg_ref, cbg_ref)
        v = conv(uv, prev_v, cwv_ref, cbv_ref)
        prev_g, prev_v = ug[rh - SUBLANES:, :], uv[rh - SUBLANES:, :]
        act = (_gelu_tanh(g) * v).astype(BF16)
        o_ref[rh * s:rh * (s + 1), :] += jnp.dot(act, wd_ref[...], preferred_element_type=F32)
    carry_g[f] = prev_g
    carry_v[f] = prev_v
    nfg_ref[...] = prev_g
    nfv_ref[...] = prev_v


def _ffn_p(h, n, t, w_up, w_down, cw, cb, layer, name):
    tm, tf = FFN_TM, FFN_TF
    nf = D_FF // tf
    tpb = t // tm
    return pl.pallas_call(
        _ffn_p_kernel,
        grid=(n, tpb, nf),
        in_specs=[pl.BlockSpec((tm, D_MODEL), lambda b, i, f: (b * tpb + i, 0)),
                  pl.BlockSpec((None, D_MODEL, tf), lambda b, i, f: (layer, 0, f)),
                  pl.BlockSpec((None, D_MODEL, tf), lambda b, i, f: (layer, 0, nf + f)),
                  pl.BlockSpec((None, tf, D_MODEL), lambda b, i, f: (layer, f, 0)),
                  pl.BlockSpec((None, FFN_CONV_WIDTH, tf), lambda b, i, f: (layer, 0, f)),
                  pl.BlockSpec((None, FFN_CONV_WIDTH, tf), lambda b, i, f: (layer, 0, nf + f)),
                  pl.BlockSpec((None, 1, tf), lambda b, i, f: (layer, 0, f)),
                  pl.BlockSpec((None, 1, tf), lambda b, i, f: (layer, 0, nf + f))],
        out_specs=[pl.BlockSpec((tm, D_MODEL), lambda b, i, f: (b * tpb + i, 0)),
                   pl.BlockSpec((None, None, SUBLANES, tf), lambda b, i, f: (b, i, 0, f)),
                   pl.BlockSpec((None, None, SUBLANES, tf), lambda b, i, f: (b, i, 0, f))],
        out_shape=[jax.ShapeDtypeStruct((n * t, D_MODEL), F32),
                   jax.ShapeDtypeStruct((n, tpb, SUBLANES, D_FF), F32),
                   jax.ShapeDtypeStruct((n, tpb, SUBLANES, D_FF), F32)],
        scratch_shapes=[pltpu.VMEM((nf, SUBLANES, tf), F32), pltpu.VMEM((nf, SUBLANES, tf), F32)],
        compiler_params=_params(3),
        name=name,
    )(h, w_up, w_up, w_down, cw, cw, cb, cb)


def _ffn_s_kernel(h_ref, wg_ref, wv_ref, wd_ref, cwg_ref, cwv_ref, cbg_ref, cbv_ref, sg_ref, sv_ref,
                  o_ref, nsg_ref, nsv_ref):
    f = pl.program_id(0)
    nb = DEC_BATCH
    h = h_ref[...]
    nst = FFN_CONV_WIDTH - 1

    def conv(w_ref, cw_ref, cb_ref, st_ref, ns_ref):
        up = jnp.dot(h, w_ref[...], preferred_element_type=F32)
        ext = [st_ref[j] for j in range(nst)]
        ext += [up[nb * t:nb * (t + 1), :] for t in range(DEC_SEQ)]
        for j in range(nst):
            ns_ref[j] = ext[DEC_SEQ + j]
        return [cw_ref[0:1, :] * ext[t] + cw_ref[1:2, :] * ext[t + 1] + cw_ref[2:3, :] * ext[t + 2] + cb_ref[...]
                for t in range(DEC_SEQ)]

    g = conv(wg_ref, cwg_ref, cbg_ref, sg_ref, nsg_ref)
    v = conv(wv_ref, cwv_ref, cbv_ref, sv_ref, nsv_ref)
    act = jnp.concatenate([(_gelu_tanh(g[t]) * v[t]).astype(BF16) for t in range(DEC_SEQ)], axis=0)
    part = jnp.dot(act, wd_ref[...], preferred_element_type=F32)

    @pl.when(f == 0)
    def _():
        o_ref[...] = part

    @pl.when(f > 0)
    def _():
        o_ref[...] += part


def _ffn_s(h, state, w_up, w_down, cw, cb, layer, name):
    tf = FFN_TF
    nf = D_FF // tf
    m = h.shape[0]
    nst = FFN_CONV_WIDTH - 1
    return pl.pallas_call(
        _ffn_s_kernel,
        grid=(nf,),
        in_specs=[pl.BlockSpec((m, D_MODEL), lambda f: (0, 0)),
                  pl.BlockSpec((None, D_MODEL, tf), lambda f: (layer, 0, f)),
                  pl.BlockSpec((None, D_MODEL, tf), lambda f: (layer, 0, nf + f)),
                  pl.BlockSpec((None, tf, D_MODEL), lambda f: (layer, f, 0)),
                  pl.BlockSpec((None, FFN_CONV_WIDTH, tf), lambda f: (layer, 0, f)),
                  pl.BlockSpec((None, FFN_CONV_WIDTH, tf), lambda f: (layer, 0, nf + f)),
                  pl.BlockSpec((None, 1, tf), lambda f: (layer, 0, f)),
                  pl.BlockSpec((None, 1, tf), lambda f: (layer, 0, nf + f)),
                  pl.BlockSpec((nst, DEC_BATCH, tf), lambda f: (0, 0, f)),
                  pl.BlockSpec((nst, DEC_BATCH, tf), lambda f: (0, 0, nf + f))],
        out_specs=[pl.BlockSpec((m, D_MODEL), lambda f: (0, 0)),
                   pl.BlockSpec((nst, DEC_BATCH, tf), lambda f: (0, 0, f)),
                   pl.BlockSpec((nst, DEC_BATCH, tf), lambda f: (0, 0, f))],
        out_shape=[jax.ShapeDtypeStruct((m, D_MODEL), F32),
                   jax.ShapeDtypeStruct((nst, DEC_BATCH, D_FF), F32),
                   jax.ShapeDtypeStruct((nst, DEC_BATCH, D_FF), F32)],
        compiler_params=_params(1),
        name=name,
    )(h, w_up, w_up, w_down, cw, cw, cb, cb, state, state)


def _rope_tables(pos):
    half = ROT_DIM // 2
    inv = ROPE_THETA ** (-jnp.arange(half, dtype=F32) * 2.0 / ROT_DIM)
    ang = pos.astype(F32)[:, None] * inv[None, :]
    cos, sin = jnp.cos(ang), jnp.sin(ang)
    t = pos.shape[0]
    rest = A_HEAD_DIM - ROT_DIM
    ch = jnp.concatenate([cos, cos, jnp.ones((t, rest), F32)], axis=1)
    ah = jnp.concatenate([-sin, jnp.zeros((t, half + rest), F32)], axis=1)
    bh = jnp.concatenate([jnp.zeros((t, half), F32), sin, jnp.zeros((t, rest), F32)], axis=1)
    rep = LANES // A_HEAD_DIM
    return tuple(jnp.tile(x, (1, rep)) for x in (ch, ah, bh))


def _pad_rows(x, rows):
    return jnp.concatenate([x, jnp.zeros((rows - x.shape[0],) + x.shape[1:], x.dtype)], axis=0)


def kernel(x_prompt, x_sample, cache_a_k, cache_a_v, state_c_conv, state_d_pool, state_ffn_conv, c_prompt, c_sample, norm_g, ada_w, ada_b, w_in_e, w_out_e, a_sinks, b_ln_g, b_ln_b, b_ws, b_bias, w_in_o, w_out_o, c_conv_w, c_conv_b, c_ln_g, c_ln_b, d_w, d_b, d_scale, ffn_w_up, ffn_conv_w, ffn_conv_b, ffn_w_down):
    assert DEPTH == 2 and x_prompt.shape == (BATCH, SEQ, D_MODEL) and x_sample.shape == (DEC_BATCH, DEC_SEQ, D_MODEL)
    w_in_e_b, w_out_e_b = w_in_e.astype(BF16), w_out_e.astype(BF16)
    w_in_o_b, w_out_o_b = w_in_o.astype(BF16), w_out_o.astype(BF16)
    w_up_b, w_down_b = ffn_w_up.astype(BF16), ffn_w_down.astype(BF16)

    pad_rows = DEC_BATCH + SUBLANES
    c_all = _pad_rows(jnp.concatenate([c_sample, c_prompt], axis=0), pad_rows)
    mods = _ada(c_all, ada_w.reshape(2 * DEPTH, D_MODEL, 3 * D_MODEL), ada_b.reshape(2 * DEPTH, 1, 3 * D_MODEL))
    mods_p = mods[:, DEC_BATCH:DEC_BATCH + BATCH].reshape(2 * DEPTH, BATCH, 1, 3 * D_MODEL)
    gains = norm_g.reshape(4 * DEPTH, 1, D_MODEL)

    grp_p = _Group(BATCH * SEQ, 512, mods_p, SEQ)
    grp_s = _Group(DEC_BATCH * DEC_SEQ, DEC_BATCH * DEC_SEQ, mods, None)
    xp = x_prompt.reshape(BATCH * SEQ, D_MODEL)
    xs = jnp.transpose(x_sample, (1, 0, 2)).reshape(DEC_SEQ * DEC_BATCH, D_MODEL)

    tabs_p = _rope_tables(jnp.arange(SEQ))
    tabs_s = tuple(_pad_rows(x, SUBLANES) for x in _rope_tables(PAST_LEN + jnp.arange(DEC_SEQ)))

    row1 = lambda v: v.reshape(1, -1)
    bexp_p = jnp.repeat(jnp.transpose(b_bias[0]), B_HEAD_DIM, axis=1)
    ws4 = jnp.tril(b_ws[0])[:, :DEC_SEQ, :DEC_SEQ]
    wexp_s = jnp.repeat(jnp.transpose(ws4, (1, 2, 0)).reshape(DEC_SEQ * DEC_SEQ, B_HEADS), B_HEAD_DIM, axis=1)
    bexp_s = _pad_rows(bexp_p[:DEC_SEQ], SUBLANES)
    sinkrow = jnp.repeat(a_sinks[0], DEC_SEQ).reshape(A_HEADS * DEC_SEQ, 1)

    ffn_cb = ffn_conv_b.reshape(DEPTH, 1, 2 * D_FF)

    def run(grp, x, is_prompt):
        outs = {}
        tag = "p" if is_prompt else "s"
        (h,) = _resnorm(grp, x, None, gains, nxt=(0, 0))
        tm_mm = 1024 if is_prompt else grp.m
        if is_prompt:
            mix, nk, nv = _even_p(h, w_in_e_b, 0, a_sinks[0], tabs_p, b_ws[0], bexp_p,
                                  row1(b_ln_g[0]), row1(b_ln_b[0]))
            outs["ak"], outs["av"] = nk, nv
        else:
            z = _mm(h, w_in_e_b, 0, tm_mm, 896, "in_even_" + tag)
            mix, nk, nv, vb = _even_s(z.reshape(DEC_SEQ, DEC_BATCH, EVEN_IN),
                                      cache_a_k[0].reshape(DEC_BATCH * WINDOW, A_KV_WIDTH),
                                      cache_a_v[0].reshape(DEC_BATCH * WINDOW, A_KV_WIDTH),
                                      sinkrow, tabs_s, wexp_s, bexp_s, row1(b_ln_g[0]), row1(b_ln_b[0]))
            outs["ak"], outs["av"], outs["bv"] = nk, nv, vb
        x, h = _mm_res(grp, mix.reshape(grp.m, D_MODEL), w_out_e_b, 0, x, gains, (0, 1), (1, 2), "out_even_" + tag)
        x, h, outs["ff0"] = ffn(grp, x, h, 0, is_prompt, nxt=(2, 4))
        dwa, dba, dsa = d_w[0], row1(d_b[0]), row1(d_scale[0])
        cargs = (c_conv_w[0], row1(c_conv_b[0]), row1(c_ln_g[0]), row1(c_ln_b[0]), dwa, dba, dsa)
        if is_prompt:
            mix, ct, dt = _odd_p(h, w_in_o_b, 0, *cargs)
            outs["cc"] = ct[:, C_TAIL - (C_CONV_WIDTH - 1):]
            outs["dp"] = dt[:, D_TAIL - (POOL_MAX - 1):]
        else:
            z = _mm(h, w_in_o_b, 0, tm_mm, 1024, "in_odd_" + tag)
            mix, co, do = _odd_s(z.reshape(DEC_SEQ, DEC_BATCH, ODD_IN),
                                 jnp.transpose(state_c_conv[0], (1, 0, 2)),
                                 jnp.transpose(state_d_pool[0], (1, 0, 2)), *cargs)
            outs["cc"] = jnp.transpose(co, (1, 0, 2))
            outs["dp"] = jnp.transpose(do, (1, 0, 2))
        x, h = _mm_res(grp, mix.reshape(grp.m, D_MODEL), w_out_o_b, 0, x, gains, (2, 5), (3, 6), "out_odd_" + tag)
        x, _, outs["ff1"] = ffn(grp, x, h, 1, is_prompt, nxt=None)
        return x, outs

    def ffn(grp, x, h, i, is_prompt, nxt):
        nst = FFN_CONV_WIDTH - 1
        if is_prompt:
            out, nfg, nfv = _ffn_p(h, BATCH, SEQ, w_up_b, w_down_b, ffn_conv_w, ffn_cb, i, "ffn_p")
            nf = jnp.concatenate([nfg[:, -1, SUBLANES - nst:], nfv[:, -1, SUBLANES - nst:]], axis=-1)
        else:
            st = jnp.transpose(state_ffn_conv[i], (1, 0, 2))
            out, nsg, nsv = _ffn_s(h, st, w_up_b, w_down_b, ffn_conv_w, ffn_cb, i, "ffn_s")
            nf = jnp.transpose(jnp.concatenate([nsg, nsv], axis=-1), (1, 0, 2))
        res = _resnorm(grp, x, out, gains, res=(2 * i + 1, 4 * i + 3), nxt=nxt)
        if nxt is None:
            return res[0], None, nf
        return res[0], res[1], nf

    yp, op = run(grp_p, xp, True)
    ys, os_ = run(grp_s, xs, False)

    kv5 = lambda a, nb: a.reshape(1, nb, WINDOW, A_KV_HEADS, A_HEAD_DIM)
    y_prompt = yp.reshape(BATCH, SEQ, D_MODEL)
    y_sample = jnp.transpose(ys.reshape(DEC_SEQ, DEC_BATCH, D_MODEL), (1, 0, 2))
    return (y_prompt, y_sample,
            kv5(op["ak"], BATCH), kv5(os_["ak"], DEC_BATCH), kv5(op["av"], BATCH), kv5(os_["av"], DEC_BATCH),
            jnp.transpose(os_["bv"], (1, 0, 2))[None],
            op["cc"][None], os_["cc"][None], op["dp"][None], os_["dp"][None],
            jnp.stack([op["ff0"], op["ff1"]]), jnp.stack([os_["ff0"], os_["ff1"]]))
```

```python
import functools

import jax
import jax.numpy as jnp
from jax import lax
from jax.experimental import pallas as pl
from jax.experimental.pallas import tpu as pltpu

D_MODEL = 2048
BATCH = 2
SEQ = 4096
DEPTH = 2
DEC_BATCH = 128
DEC_SEQ = 4
PAST_LEN = 8192
A_HEADS = 16
A_KV_HEADS = 4
A_HEAD_DIM = 64
A_Q_WIDTH = A_HEADS * A_HEAD_DIM
A_KV_WIDTH = A_KV_HEADS * A_HEAD_DIM
WINDOW = 128
ROT_DIM = A_HEAD_DIM // 4
ROPE_THETA = 500000.0
B_HEADS = 8
B_HEAD_DIM = 128
B_WIDTH = B_HEADS * B_HEAD_DIM
CHUNK = 128
C_WIDTH = 1024
C_CONV_WIDTH = 31
POOL_SIZES = (2, 4, 8, 16)
D_GROUPS = len(POOL_SIZES)
POOL_MAX = max(POOL_SIZES)
D_WIDTH = 1024
D_GROUP_WIDTH = D_WIDTH // D_GROUPS
D_FF = 5632
FFN_CONV_WIDTH = 3
EVEN_IN = A_Q_WIDTH + 2 * A_KV_WIDTH + 2 * B_WIDTH
ODD_IN = 2 * C_WIDTH + D_WIDTH
EPS = 1e-6
NEG_INF = -1e30

LANES = 128
SUBLANES = 8
VMEM_LIMIT = 56 * 1024 * 1024

BF16 = jnp.bfloat16
F32 = jnp.float32

SEQ_BLOCK = 32
C_TAIL = 32
D_TAIL = 16


def _params(n_axes):
    return pltpu.CompilerParams(dimension_semantics=("arbitrary",) * n_axes,
                                vmem_limit_bytes=VMEM_LIMIT)


_GELU_C = 0.7978845608028654


def _gelu_tanh(x):
    hx = 0.5 * x
    return hx + hx * jnp.tanh(x * (_GELU_C + (_GELU_C * 0.044715) * (x * x)))


def _rms(x, g):
    return x * lax.rsqrt(jnp.mean(x * x, axis=-1, keepdims=True) + EPS) * g


def _layer_norm(x, g, b):
    mu = jnp.mean(x, axis=-1, keepdims=True)
    xc = x - mu
    var = jnp.mean(xc * xc, axis=-1, keepdims=True)
    return xc * lax.rsqrt(var + EPS) * g + b


def _rows(m, tm):
    r = m.shape[0]
    if r == 1 or r == tm:
        return m
    return jnp.concatenate([m] * (tm // r), axis=0)


def _rope(x, c, a, b):
    return x * c + pltpu.roll(x, LANES - ROT_DIM // 2, 1) * a + pltpu.roll(x, ROT_DIM // 2, 1) * b


def _ada_kernel(c_ref, w_ref, b_ref, o_ref):
    c = c_ref[...]
    s = (c * jax.nn.sigmoid(c)).astype(BF16)
    o_ref[...] = jnp.dot(s, w_ref[...].astype(BF16), preferred_element_type=F32) + b_ref[...]


def _ada(c_all, w, b):
    r = c_all.shape[0]
    tn = 1536
    n_sub, _, n = w.shape
    return pl.pallas_call(
        _ada_kernel,
        grid=(n_sub, n // tn),
        in_specs=[pl.BlockSpec((r, D_MODEL), lambda k, j: (0, 0)),
                  pl.BlockSpec((None, D_MODEL, tn), lambda k, j: (k, 0, j)),
                  pl.BlockSpec((None, 1, tn), lambda k, j: (k, 0, j))],
        out_specs=pl.BlockSpec((None, r, tn), lambda k, j: (k, 0, j)),
        out_shape=jax.ShapeDtypeStruct((n_sub, r, n), F32),
        compiler_params=_params(2),
        name="ada",
    )(c_all, w, b)


def _resnorm_kernel(*refs, has_res, has_next):
    it = iter(refs)
    x_ref = next(it)
    if has_res:
        o_ref, gate_ref, gpost_ref = next(it), next(it), next(it)
    if has_next:
        gpre_ref, scale_ref, shift_ref = next(it), next(it), next(it)
    if has_res:
        xo_ref = next(it)
    if has_next:
        ho_ref = next(it)
    x = x_ref[...]
    tm = x.shape[0]
    if has_res:
        x = x + _rows(gate_ref[...], tm) * _rms(o_ref[...], gpost_ref[...])
        xo_ref[...] = x
    if has_next:
        h = _rms(x, gpre_ref[...]) * (1.0 + _rows(scale_ref[...], tm)) + _rows(shift_ref[...], tm)
        ho_ref[...] = h.astype(BF16)


class _Group:
    def __init__(self, m, tm, mods, per_batch_rows):
        self.m, self.tm, self.mods = m, tm, mods
        self.per_batch_rows = per_batch_rows

    def mod_spec(self, k, col):
        if self.per_batch_rows is None:
            return pl.BlockSpec((None, DEC_BATCH, D_MODEL), lambda i: (k, 0, col))
        tpb = self.per_batch_rows // self.tm
        return pl.BlockSpec((None, None, 1, D_MODEL), lambda i: (k, i // tpb, 0, col))


def _resnorm(grp, x, out, gains, res=None, nxt=None):
    tm = grp.tm
    row = pl.BlockSpec((tm, D_MODEL), lambda i: (i, 0))
    args, specs, out_shapes, out_specs = [x], [row], [], []

    def gain_spec(idx):
        return pl.BlockSpec((None, 1, D_MODEL), lambda i: (idx, 0, 0))

    if res is not None:
        args += [out, grp.mods, gains]
        specs += [row, grp.mod_spec(res[0], 2), gain_spec(res[1])]
        out_shapes.append(jax.ShapeDtypeStruct((grp.m, D_MODEL), F32))
        out_specs.append(row)
    if nxt is not None:
        args += [gains, grp.mods, grp.mods]
        specs += [gain_spec(nxt[1]), grp.mod_spec(nxt[0], 1), grp.mod_spec(nxt[0], 0)]
        out_shapes.append(jax.ShapeDtypeStruct((grp.m, D_MODEL), BF16))
        out_specs.append(row)
    res_out = pl.pallas_call(
        functools.partial(_resnorm_kernel, has_res=res is not None, has_next=nxt is not None),
        grid=(grp.m // tm,),
        in_specs=specs, out_specs=out_specs, out_shape=out_shapes,
        compiler_params=_params(1),
        name="resnorm",
    )(*args)
    return res_out


def _mm_kernel(a_ref, w_ref, o_ref):
    o_ref[...] = jnp.dot(a_ref[...], w_ref[...], preferred_element_type=F32).astype(o_ref.dtype)


def _mm(a, w, layer, tm, tn, name, out_dtype=F32):
    m, k = a.shape
    n = w.shape[2]
    return pl.pallas_call(
        _mm_kernel,
        grid=(n // tn, m // tm),
        in_specs=[pl.BlockSpec((tm, k), lambda j, i: (i, 0)),
                  pl.BlockSpec((None, k, tn), lambda j, i: (layer, 0, j))],
        out_specs=pl.BlockSpec((tm, tn), lambda j, i: (i, j)),
        out_shape=jax.ShapeDtypeStruct((m, n), out_dtype),
        compiler_params=_params(2),
        name=name,
    )(a, w)


MM_RES_SPLIT = 2


def _mm_res_kernel(a_ref, w_ref, x_ref, gate_ref, gpost_ref, gpre_ref, scale_ref, shift_ref, xo_ref, ho_ref):
    tm = a_ref.shape[0]
    rs = max(tm // MM_RES_SPLIT, gate_ref.shape[0])
    outs = [jnp.dot(a_ref[r:r + rs, :], w_ref[...], preferred_element_type=F32) for r in range(0, tm, rs)]
    for r, out in zip(range(0, tm, rs), outs):
        x = x_ref[r:r + rs, :] + _rows(gate_ref[...], rs) * _rms(out, gpost_ref[...])
        xo_ref[r:r + rs, :] = x
        h = _rms(x, gpre_ref[...]) * (1.0 + _rows(scale_ref[...], rs)) + _rows(shift_ref[...], rs)
        ho_ref[r:r + rs, :] = h.astype(BF16)


def _mm_res(grp, a, w, layer, x, gains, res, nxt, name):
    tm = grp.tm
    k = a.shape[1]
    row = pl.BlockSpec((tm, D_MODEL), lambda i: (i, 0))
    gain_spec = lambda idx: pl.BlockSpec((None, 1, D_MODEL), lambda i: (idx, 0, 0))
    return pl.pallas_call(
        _mm_res_kernel,
        grid=(grp.m // tm,),
        in_specs=[pl.BlockSpec((tm, k), lambda i: (i, 0)),
                  pl.BlockSpec((None, k, D_MODEL), lambda i: (layer, 0, 0)),
                  row, grp.mod_spec(res[0], 2), gain_spec(res[1]),
                  gain_spec(nxt[1]), grp.mod_spec(nxt[0], 1), grp.mod_spec(nxt[0], 0)],
        out_specs=[row, row],
        out_shape=[jax.ShapeDtypeStruct((grp.m, D_MODEL), F32), jax.ShapeDtypeStruct((grp.m, D_MODEL), BF16)],
        compiler_params=_params(1),
        name=name,
    )(a, w, x, grp.mods, gains, gains, grp.mods, grp.mods)


CORE_ROWS = 512
CORE_HALF = CORE_ROWS // 2


PROJ_CHUNK = 512


def _proj_chunks(dst, lhs_ref, r0, w_ref):
    def make(c):
        def run():
            cols = slice(c, c + PROJ_CHUNK)
            dst[:, cols] = jnp.dot(lhs_ref[r0:r0 + CORE_HALF, :], w_ref[:, cols], preferred_element_type=F32)
        return run
    return iter([make(c) for c in range(0, w_ref.shape[1], PROJ_CHUNK)])


def _issue(pre, n=1):
    for _ in range(n):
        thunk = next(pre, None)
        if thunk is not None:
            thunk()


def _even_block(pre, z, r0, o0, first_lim, sink_ref, cos_ref, sa_ref, sb_ref, ws_ref, bexp_ref, lng_ref, lnb_ref,
                mix_ref, nk_ref, nv_ref, kprev, vprev):
    qb = WINDOW
    zr = slice(r0, r0 + qb)
    orow = slice(o0, o0 + qb)
    cos_t, sa_t, sb_t = cos_ref[orow, :], sa_ref[orow, :], sb_ref[orow, :]
    lane = lax.broadcasted_iota(jnp.int32, (qb, LANES), 1)
    lo = lane < A_HEAD_DIM
    lane2 = lax.broadcasted_iota(jnp.int32, (2 * qb, LANES), 1)
    lo2 = lane2 < A_HEAD_DIM
    row = lax.broadcasted_iota(jnp.int32, (qb, 2 * qb), 0)
    col = lax.broadcasted_iota(jnp.int32, (qb, 2 * qb), 1)
    mask = ((col > row) & (col < qb)) | ((col >= qb) & (col - qb <= row))
    if first_lim is not None:
        mask = mask & (col >= first_lim)
    nt = (((1,), (1,)), ((), ()))

    for jk in range(A_KV_HEADS // 2):
        _issue(pre)
        ksl = slice(A_Q_WIDTH + LANES * jk, A_Q_WIDTH + LANES * (jk + 1))
        vsl = slice(A_Q_WIDTH + A_KV_WIDTH + LANES * jk, A_Q_WIDTH + A_KV_WIDTH + LANES * (jk + 1))
        csl = slice(LANES * jk, LANES * (jk + 1))
        kc = _rope(z[zr, ksl], cos_t, sa_t, sb_t)
        vc = z[zr, vsl]
        kall = jnp.concatenate([kprev[:, csl], kc], axis=0)
        vall = jnp.concatenate([vprev[:, csl], vc], axis=0)
        krol = pltpu.roll(kall, A_HEAD_DIM, 1)
        vrol = pltpu.roll(vall, A_HEAD_DIM, 1)
        for sub in range(2):
            kvh = 2 * jk + sub
            kd = (jnp.where(lo2, kall, krol) if sub == 0 else jnp.where(lo2, krol, kall)).astype(BF16)
            vd = (jnp.where(lo2, vall, vrol) if sub == 0 else jnp.where(lo2, vrol, vall)).astype(BF16)
            for qs in range(2):
                js = 2 * kvh + qs
                qsl = _rope(z[zr, LANES * js:LANES * (js + 1)], cos_t, sa_t, sb_t) * (A_HEAD_DIM ** -0.5)
                outs = []
                for half in range(2):
                    h = 2 * js + half
                    qm = jnp.where(lo if half == 0 else jnp.logical_not(lo), qsl, 0.0).astype(BF16)
                    s = lax.dot_general(qm, kd, nt, preferred_element_type=F32)
                    s = jnp.where(mask, s, NEG_INF)
                    sink = sink_ref[h]
                    m = jnp.maximum(jnp.max(s, axis=-1, keepdims=True), sink)
                    p = jnp.exp(s - m)
                    den = jnp.sum(p, axis=-1, keepdims=True) + jnp.exp(sink - m)
                    o = jnp.dot(p.astype(BF16), vd, preferred_element_type=F32)
                    outs.append(o * (1.0 / den))
                mix_ref[orow, LANES * js:LANES * (js + 1)] = jnp.where(lo, outs[0], outs[1]).astype(BF16)
        kprev[:, csl] = kc
        vprev[:, csl] = vc
        nk_ref[:, csl] = kc
        nv_ref[:, csl] = vc

    _issue(pre, 2)
    zb0 = A_Q_WIDTH + 2 * A_KV_WIDTH
    u = _gelu_tanh(z[zr, zb0:zb0 + B_WIDTH])
    vb = _layer_norm(_gelu_tanh(z[zr, zb0 + B_WIDTH:zb0 + 2 * B_WIDTH]), lng_ref[...], lnb_ref[...])
    ri = lax.broadcasted_iota(jnp.int32, (CHUNK, CHUNK), 0)
    ci = lax.broadcasted_iota(jnp.int32, (CHUNK, CHUNK), 1)
    tri = ri >= ci
    for h in range(B_HEADS):
        hs = slice(B_HEAD_DIM * h, B_HEAD_DIM * (h + 1))
        w = jnp.where(tri, ws_ref[h], 0.0).astype(BF16)
        mixed = jnp.dot(w, vb[:, hs].astype(BF16), preferred_element_type=F32) + bexp_ref[:, hs]
        mix_ref[orow, A_Q_WIDTH + B_HEAD_DIM * h:A_Q_WIDTH + B_HEAD_DIM * (h + 1)] = (u[:, hs] * mixed).astype(BF16)


def _even_p_kernel(sink_ref, hc_ref, hn_ref, w_ref, cos_ref, sa_ref, sb_ref, ws_ref, bexp_ref, lng_ref, lnb_ref,
                   mix_ref, nk_ref, nv_ref, za, zb, kprev, vprev):
    step = pl.program_id(0)
    b = step % (SEQ // CORE_ROWS)
    args = (sink_ref, cos_ref, sa_ref, sb_ref, ws_ref, bexp_ref, lng_ref, lnb_ref, mix_ref, nk_ref, nv_ref,
            kprev, vprev)

    @pl.when(step == 0)
    def _():
        za[...] = jnp.dot(hc_ref[0:CORE_HALF, :], w_ref[...], preferred_element_type=F32)

    @pl.when(b == 0)
    def _():
        kprev[...] = jnp.zeros_like(kprev)
        vprev[...] = jnp.zeros_like(vprev)

    pre = _proj_chunks(zb, hc_ref, CORE_HALF, w_ref)
    _even_block(pre, za, 0, 0, jnp.where(b > 0, 0, WINDOW), *args)
    _even_block(pre, za, WINDOW, WINDOW, None, *args)
    _issue(pre, EVEN_IN // PROJ_CHUNK)
    pre = _proj_chunks(za, hn_ref, 0, w_ref)
    _even_block(pre, zb, 0, CORE_HALF, None, *args)
    _even_block(pre, zb, WINDOW, CORE_HALF + WINDOW, None, *args)
    _issue(pre, EVEN_IN // PROJ_CHUNK)


def _even_p(h, w, layer, sinks, tabs, ws, bexp, lng, lnb):
    rows = CORE_ROWS
    steps = h.shape[0] // rows
    per_seq = SEQ // rows
    last_half = h.shape[0] // CORE_HALF - 1
    full = lambda shape: pl.BlockSpec(shape, lambda s: (0,) * len(shape))
    tab = pl.BlockSpec((rows, LANES), lambda s: (s % per_seq, 0))
    kv = pl.BlockSpec((None, WINDOW, A_KV_WIDTH), lambda s: (s // per_seq, 0, 0))
    return pl.pallas_call(
        _even_p_kernel,
        grid=(steps,),
        in_specs=[pl.BlockSpec(memory_space=pltpu.SMEM),
                  pl.BlockSpec((rows, D_MODEL), lambda s: (s, 0)),
                  pl.BlockSpec((CORE_HALF, D_MODEL), lambda s: (jnp.minimum(2 * s + 2, last_half), 0)),
                  pl.BlockSpec((None, D_MODEL, EVEN_IN), lambda s: (layer, 0, 0)),
                  tab, tab, tab,
                  full((B_HEADS, CHUNK, CHUNK)),
                  pl.BlockSpec((CHUNK, B_WIDTH), lambda s: (0, 0)),
                  full((1, B_WIDTH)), full((1, B_WIDTH))],
        out_specs=[pl.BlockSpec((rows, D_MODEL), lambda s: (s, 0)), kv, kv],
        out_shape=[jax.ShapeDtypeStruct((h.shape[0], D_MODEL), BF16),
                   jax.ShapeDtypeStruct((BATCH, WINDOW, A_KV_WIDTH), F32),
                   jax.ShapeDtypeStruct((BATCH, WINDOW, A_KV_WIDTH), F32)],
        scratch_shapes=[pltpu.VMEM((CORE_HALF, EVEN_IN), F32), pltpu.VMEM((CORE_HALF, EVEN_IN), F32),
                        pltpu.VMEM((WINDOW, A_KV_WIDTH), F32), pltpu.VMEM((WINDOW, A_KV_WIDTH), F32)],
        compiler_params=_params(1),
        name="even_p",
    )(sinks, h, h, w, *tabs, ws, bexp, lng, lnb)


def _even_s_kernel(sinkrow_ref, z_ref, ck_ref, cv_ref, cos_ref, sa_ref, sb_ref, wexp_ref, bexp_ref,
                   lng_ref, lnb_ref, mix_ref, ok_ref, ov_ref, vb_ref, qm, osc, kn, vn):
    s_blk = SEQ_BLOCK
    rows_per_seq = A_HEADS * DEC_SEQ
    lane256 = lax.broadcasted_iota(jnp.int32, (s_blk, A_KV_WIDTH), 1)
    kn[...] = jnp.zeros_like(kn)
    vn[...] = jnp.zeros_like(vn)
    halves = A_KV_WIDTH // LANES

    def put(ref, start, stride, val):
        for c in range(halves):
            ref[c, pl.ds(start, s_blk, stride=stride), :] = val[:, LANES * c:LANES * (c + 1)]

    def get_rows(ref, start, size):
        return jnp.concatenate([ref[c, pl.ds(start, size), :] for c in range(halves)], axis=1)

    def get_strided(ref, start, stride):
        return jnp.concatenate([ref[c, pl.ds(start, s_blk, stride=stride), :] for c in range(halves)], axis=1)

    for t in range(DEC_SEQ):
        c, a, bb = cos_ref[t:t + 1, :], sa_ref[t:t + 1, :], sb_ref[t:t + 1, :]

        def rope2(lo_lane):
            return jnp.concatenate([_rope(z_ref[t, :, lo_lane:lo_lane + LANES], c, a, bb),
                                    _rope(z_ref[t, :, lo_lane + LANES:lo_lane + 2 * LANES], c, a, bb)], axis=1)

        put(kn, t, SUBLANES, rope2(A_Q_WIDTH))
        put(vn, t, SUBLANES, z_ref[t, :, A_Q_WIDTH + A_KV_WIDTH:A_Q_WIDTH + 2 * A_KV_WIDTH])
        for j in range(A_KV_HEADS):
            qs = rope2(A_KV_WIDTH * j) * (A_HEAD_DIM ** -0.5)
            keep = (lane256 >= A_HEAD_DIM * j) & (lane256 < A_HEAD_DIM * (j + 1))
            for g in range(A_HEADS // A_KV_HEADS):
                sh = ((j - g) * A_HEAD_DIM) % A_KV_WIDTH
                qr = qs if sh == 0 else pltpu.roll(qs, sh, 1)
                r = (4 * j + g) * DEC_SEQ + t
                put(qm, r, rows_per_seq, jnp.where(keep, qr, 0.0))

    rr = lax.broadcasted_iota(jnp.int32, (rows_per_seq, WINDOW), 0) % DEC_SEQ
    cc = lax.broadcasted_iota(jnp.int32, (rows_per_seq, WINDOW), 1)
    mask_old = cc > rr
    new0 = WINDOW - DEC_SEQ
    mask_new = (cc >= new0) & (cc - new0 <= rr)
    row8 = lax.broadcasted_iota(jnp.int32, (SUBLANES, A_KV_WIDTH), 0)
    sink = sinkrow_ref[...]
    nt = (((1,), (1,)), ((), ()))

    def shifted(c_ref, new_ref, o_ref, n):
        base = pl.multiple_of(n * WINDOW, WINDOW)
        old = c_ref[pl.ds(base, WINDOW), :]
        rolled = pltpu.roll(old, WINDOW - DEC_SEQ, 0)
        fresh = get_rows(new_ref, pl.multiple_of(n * SUBLANES, SUBLANES), SUBLANES)
        tail = jnp.where(row8 >= SUBLANES - DEC_SEQ, pltpu.roll(fresh, SUBLANES - DEC_SEQ, 0),
                         rolled[WINDOW - SUBLANES:, :])
        new = jnp.concatenate([rolled[:WINDOW - SUBLANES, :], tail], axis=0)
        o_ref[pl.ds(base, WINDOW), :] = new
        return old.astype(BF16), new.astype(BF16)

    def body(n, carry):
        k_old, k_new = shifted(ck_ref, kn, ok_ref, n)
        v_old, v_new = shifted(cv_ref, vn, ov_ref, n)
        q = get_rows(qm, pl.multiple_of(n * rows_per_seq, rows_per_seq), rows_per_seq).astype(BF16)
        s_old = jnp.where(mask_old, lax.dot_general(q, k_old, nt, preferred_element_type=F32), NEG_INF)
        s_new = jnp.where(mask_new, lax.dot_general(q, k_new, nt, preferred_element_type=F32), NEG_INF)
        m = jnp.maximum(jnp.maximum(jnp.max(s_old, axis=-1, keepdims=True),
                                    jnp.max(s_new, axis=-1, keepdims=True)), sink)
        p_old = jnp.exp(s_old - m)
        p_new = jnp.exp(s_new - m)
        den = (jnp.sum(p_old, axis=-1, keepdims=True) + jnp.sum(p_new, axis=-1, keepdims=True)
               + jnp.exp(sink - m))
        o = (jnp.dot(p_old.astype(BF16), v_old, preferred_element_type=F32)
             + jnp.dot(p_new.astype(BF16), v_new, preferred_element_type=F32))
        o = o * (1.0 / den)
        for c in range(halves):
            osc[c, pl.ds(pl.multiple_of(n * rows_per_seq, rows_per_seq), rows_per_seq), :] = o[:, LANES * c:LANES * (c + 1)]
        return carry

    lax.fori_loop(0, s_blk, body, 0, unroll=4)

    for t in range(DEC_SEQ):
        for j in range(A_KV_HEADS):
            acc = jnp.zeros((s_blk, A_KV_WIDTH), F32)
            for g in range(A_HEADS // A_KV_HEADS):
                r = (4 * j + g) * DEC_SEQ + t
                ov = get_strided(osc, r, rows_per_seq)
                sh = ((g - j) * A_HEAD_DIM) % A_KV_WIDTH
                orr = ov if sh == 0 else pltpu.roll(ov, sh, 1)
                acc = jnp.where((lane256 >= A_HEAD_DIM * g) & (lane256 < A_HEAD_DIM * (g + 1)), orr, acc)
            mix_ref[t, :, A_KV_WIDTH * j:A_KV_WIDTH * (j + 1)] = acc.astype(BF16)

    zb0 = A_Q_WIDTH + 2 * A_KV_WIDTH
    us, vbs = [], []
    for t in range(DEC_SEQ):
        us.append(_gelu_tanh(z_ref[t, :, zb0:zb0 + B_WIDTH]))
        v = _layer_norm(_gelu_tanh(z_ref[t, :, zb0 + B_WIDTH:zb0 + 2 * B_WIDTH]), lng_ref[...], lnb_ref[...])
        vb_ref[t] = v
        vbs.append(v)
    for t in range(DEC_SEQ):
        mixed = bexp_ref[t:t + 1, :]
        for s in range(t + 1):
            mixed = mixed + wexp_ref[DEC_SEQ * t + s:DEC_SEQ * t + s + 1, :] * vbs[s]
        mix_ref[t, :, A_Q_WIDTH:A_Q_WIDTH + B_WIDTH] = (us[t] * mixed).astype(BF16)


def _even_s(z3, ck, cv, sinkrow, tabs, wexp, bexp, lng, lnb):
    s_blk = SEQ_BLOCK
    rows_per_seq = A_HEADS * DEC_SEQ
    full = lambda shape: pl.BlockSpec(shape, lambda s: (0,) * len(shape))
    cache = pl.BlockSpec((s_blk * WINDOW, A_KV_WIDTH), lambda s: (s, 0))
    return pl.pallas_call(
        _even_s_kernel,
        grid=(DEC_BATCH // s_blk,),
        in_specs=[full((rows_per_seq, 1)),
                  pl.BlockSpec((DEC_SEQ, s_blk, EVEN_IN), lambda s: (0, s, 0)),
                  cache, cache,
                  full((SUBLANES, LANES)), full((SUBLANES, LANES)), full((SUBLANES, LANES)),
                  full((DEC_SEQ * DEC_SEQ, B_WIDTH)), full((SUBLANES, B_WIDTH)),
                  full((1, B_WIDTH)), full((1, B_WIDTH))],
        out_specs=[pl.BlockSpec((DEC_SEQ, s_blk, D_MODEL), lambda s: (0, s, 0)),
                   cache, cache,
                   pl.BlockSpec((DEC_SEQ, s_blk, B_WIDTH), lambda s: (0, s, 0))],
        out_shape=[jax.ShapeDtypeStruct((DEC_SEQ, DEC_BATCH, D_MODEL), BF16),
                   jax.ShapeDtypeStruct((DEC_BATCH * WINDOW, A_KV_WIDTH), F32),
                   jax.ShapeDtypeStruct((DEC_BATCH * WINDOW, A_KV_WIDTH), F32),
                   jax.ShapeDtypeStruct((DEC_SEQ, DEC_BATCH, B_WIDTH), F32)],
        scratch_shapes=[pltpu.VMEM((A_KV_WIDTH // LANES, s_blk * rows_per_seq, LANES), F32),
                        pltpu.VMEM((A_KV_WIDTH // LANES, s_blk * rows_per_seq, LANES), F32),
                        pltpu.VMEM((A_KV_WIDTH // LANES, s_blk * SUBLANES, LANES), F32),
                        pltpu.VMEM((A_KV_WIDTH // LANES, s_blk * SUBLANES, LANES), F32)],
        compiler_params=_params(1),
        name="even_s",
    )(sinkrow, z3, ck, cv, *tabs, wexp, bexp, lng, lnb)


ODD_ROWS = CORE_HALF
ODD_RC = 64


def _odd_block(pre, z_ref, o0, pos0, cw_ref, cb_ref, lng_ref, lnb_ref, dw_ref, db_ref, dsc_ref,
               mix_ref, ctail_ref, dtail_ref, gext, dext, cbuf, dbuf, shbuf):
    tr = ODD_ROWS
    _issue(pre, 2)
    gext[C_TAIL:C_TAIL + tr, :] = z_ref[:, 0:C_WIDTH] * jax.nn.sigmoid(z_ref[:, C_WIDTH:2 * C_WIDTH])
    dext[D_TAIL:D_TAIL + tr, :] = z_ref[:, 2 * C_WIDTH:2 * C_WIDTH + D_WIDTH]

    gw = D_GROUP_WIDTH
    lead = C_TAIL - (C_CONV_WIDTH - 1)
    sh_rows = C_TAIL + tr - SUBLANES
    for s in range(1, SUBLANES):
        shbuf[s, 0:sh_rows, :] = gext[s:s + sh_rows, :]

    def tap(off, r0, ls):
        a, s = divmod(off, SUBLANES)
        src = gext if s == 0 else shbuf.at[s]
        return src[r0 + SUBLANES * a:r0 + SUBLANES * a + ODD_RC, ls]

    for rc in range(tr // ODD_RC):
        r0 = rc * ODD_RC
        _issue(pre)
        for c in range(C_WIDTH // gw):
            ls = slice(gw * c, gw * (c + 1))
            acc = jnp.broadcast_to(cb_ref[:, ls], (ODD_RC, gw))
            for k in range(C_CONV_WIDTH):
                acc = acc + cw_ref[k:k + 1, ls] * tap(lead + k, r0, ls)
            cbuf[r0:r0 + ODD_RC, ls] = acc
        y = _layer_norm(cbuf[r0:r0 + ODD_RC, :], lng_ref[...], lnb_ref[...])
        mix_ref[o0 + r0:o0 + r0 + ODD_RC, 0:C_WIDTH] = (y * jax.nn.sigmoid(y)).astype(BF16)

        pos1 = (pos0 + r0 + 1 + lax.broadcasted_iota(jnp.int32, (ODD_RC, gw), 0))
        for g, wsz in enumerate(POOL_SIZES):
            ls = slice(gw * g, gw * (g + 1))
            win = dext[D_TAIL + r0:D_TAIL + r0 + ODD_RC, ls]
            for jj in range(1, wsz):
                win = win + dext[D_TAIL + r0 - jj:D_TAIL + r0 - jj + ODD_RC, ls]
            cnt = jnp.minimum(pos1, wsz).astype(F32)
            diff = win / cnt - dext[D_TAIL + r0:D_TAIL + r0 + ODD_RC, ls]
            dbuf[r0:r0 + ODD_RC, ls] = diff.astype(BF16)

    for g in range(D_GROUPS):
        ls = slice(gw * g, gw * (g + 1))
        o = jnp.dot(dbuf[:, ls], dw_ref[g].astype(BF16), preferred_element_type=F32) + db_ref[:, ls]
        mix_ref[o0:o0 + tr, C_WIDTH + gw * g:C_WIDTH + gw * (g + 1)] = (o * dsc_ref[:, ls]).astype(BF16)

    gext[0:C_TAIL, :] = gext[tr:tr + C_TAIL, :]
    dext[0:D_TAIL, :] = dext[tr:tr + D_TAIL, :]
    ctail_ref[...] = gext[0:C_TAIL, :]
    dtail_ref[...] = dext[0:D_TAIL, :]


def _odd_p_kernel(hc_ref, hn_ref, w_ref, cw_ref, cb_ref, lng_ref, lnb_ref, dw_ref, db_ref, dsc_ref,
                  mix_ref, ctail_ref, dtail_ref, za, zb, gext, dext, cbuf, dbuf, shbuf):
    step = pl.program_id(0)
    b = step % (SEQ // CORE_ROWS)
    args = (cw_ref, cb_ref, lng_ref, lnb_ref, dw_ref, db_ref, dsc_ref, mix_ref, ctail_ref, dtail_ref,
            gext, dext, cbuf, dbuf, shbuf)

    @pl.when(step == 0)
    def _():
        za[...] = jnp.dot(hc_ref[0:CORE_HALF, :], w_ref[...], preferred_element_type=F32)

    @pl.when(b == 0)
    def _():
        gext[0:C_TAIL, :] = jnp.zeros((C_TAIL, C_WIDTH), F32)
        dext[0:D_TAIL, :] = jnp.zeros((D_TAIL, D_WIDTH), F32)

    pre = _proj_chunks(zb, hc_ref, CORE_HALF, w_ref)
    _odd_block(pre, za, 0, b * CORE_ROWS, *args)
    _issue(pre, ODD_IN // PROJ_CHUNK)
    pre = _proj_chunks(za, hn_ref, 0, w_ref)
    _odd_block(pre, zb, CORE_HALF, b * CORE_ROWS + CORE_HALF, *args)
    _issue(pre, ODD_IN // PROJ_CHUNK)


def _odd_p(h, w, layer, cw, cb, lng, lnb, dw, db, dsc):
    rows = CORE_ROWS
    steps = h.shape[0] // rows
    per_seq = SEQ // rows
    last_half = h.shape[0] // CORE_HALF - 1
    tr = ODD_ROWS
    full = lambda shape: pl.BlockSpec(shape, lambda s: (0,) * len(shape))
    return pl.pallas_call(
        _odd_p_kernel,
        grid=(steps,),
        in_specs=[pl.BlockSpec((rows, D_MODEL), lambda s: (s, 0)),
                  pl.BlockSpec((CORE_HALF, D_MODEL), lambda s: (jnp.minimum(2 * s + 2, last_half), 0)),
                  pl.BlockSpec((None, D_MODEL, ODD_IN), lambda s: (layer, 0, 0)),
                  full((C_CONV_WIDTH, C_WIDTH)), full((1, C_WIDTH)), full((1, C_WIDTH)), full((1, C_WIDTH)),
                  full((D_GROUPS, D_GROUP_WIDTH, D_GROUP_WIDTH)), full((1, D_WIDTH)), full((1, D_WIDTH))],
        out_specs=[pl.BlockSpec((rows, D_MODEL), lambda s: (s, 0)),
                   pl.BlockSpec((None, C_TAIL, C_WIDTH), lambda s: (s // per_seq, 0, 0)),
                   pl.BlockSpec((None, D_TAIL, D_WIDTH), lambda s: (s // per_seq, 0, 0))],
        out_shape=[jax.ShapeDtypeStruct((h.shape[0], D_MODEL), BF16),
                   jax.ShapeDtypeStruct((BATCH, C_TAIL, C_WIDTH), F32),
                   jax.ShapeDtypeStruct((BATCH, D_TAIL, D_WIDTH), F32)],
        scratch_shapes=[pltpu.VMEM((CORE_HALF, ODD_IN), F32), pltpu.VMEM((CORE_HALF, ODD_IN), F32),
                        pltpu.VMEM((C_TAIL + tr, C_WIDTH), F32), pltpu.VMEM((D_TAIL + tr, D_WIDTH), F32),
                        pltpu.VMEM((tr, C_WIDTH), F32), pltpu.VMEM((tr, D_WIDTH), BF16),
                        pltpu.VMEM((SUBLANES, C_TAIL + tr, C_WIDTH), F32)],
        compiler_params=_params(1),
        name="odd_p",
    )(h, h, w, cw, cb, lng, lnb, dw, db, dsc)


def _odd_s_kernel(z_ref, cs_ref, ds_ref, cw_ref, cb_ref, lng_ref, lnb_ref, dw_ref, db_ref, dsc_ref,
                  mix_ref, co_ref, do_ref, cbuf, dbuf):
    s_blk = SEQ_BLOCK
    gw = D_GROUP_WIDTH
    nc, nd = C_CONV_WIDTH - 1, POOL_MAX - 1
    for c in range(C_WIDTH // gw):
        ls = slice(gw * c, gw * (c + 1))
        accs = [jnp.broadcast_to(cb_ref[:, ls], (s_blk, gw)) for _ in range(DEC_SEQ)]
        for j in range(nc + DEC_SEQ):
            if j < nc:
                e = cs_ref[j, :, ls]
            else:
                t = j - nc
                e = z_ref[t, :, gw * c:gw * (c + 1)] * jax.nn.sigmoid(
                    z_ref[t, :, C_WIDTH + gw * c:C_WIDTH + gw * (c + 1)])
            if j >= DEC_SEQ:
                co_ref[j - DEC_SEQ, :, ls] = e
            for t in range(DEC_SEQ):
                k = j - t
                if 0 <= k < C_CONV_WIDTH:
                    accs[t] = accs[t] + cw_ref[k:k + 1, ls] * e
        for t in range(DEC_SEQ):
            cbuf[t, :, ls] = accs[t]
    for t in range(DEC_SEQ):
        y = _layer_norm(cbuf[t], lng_ref[...], lnb_ref[...])
        mix_ref[t, :, 0:C_WIDTH] = (y * jax.nn.sigmoid(y)).astype(BF16)

    for g, wsz in enumerate(POOL_SIZES):
        ls = slice(gw * g, gw * (g + 1))
        ext = []
        for j in range(nd + DEC_SEQ):
            if j < nd:
                e = ds_ref[j, :, ls]
            else:
                e = z_ref[j - nd, :, 2 * C_WIDTH + gw * g:2 * C_WIDTH + gw * (g + 1)]
            if j >= DEC_SEQ:
                do_ref[j - DEC_SEQ, :, ls] = e
            ext.append(e)
        for t in range(DEC_SEQ):
            win = ext[nd + t]
            for jj in range(1, wsz):
                win = win + ext[nd + t - jj]
            cnt = float(min(PAST_LEN + t + 1, wsz))
            dbuf[s_blk * t:s_blk * (t + 1), ls] = (win / cnt - ext[nd + t]).astype(BF16)
    for g in range(D_GROUPS):
        ls = slice(gw * g, gw * (g + 1))
        o = jnp.dot(dbuf[:, ls], dw_ref[g].astype(BF16), preferred_element_type=F32) + db_ref[:, ls]
        o = (o * dsc_ref[:, ls]).astype(BF16)
        for t in range(DEC_SEQ):
            mix_ref[t, :, C_WIDTH + gw * g:C_WIDTH + gw * (g + 1)] = o[s_blk * t:s_blk * (t + 1), :]


def _odd_s(z3, cs, ds, cw, cb, lng, lnb, dw, db, dsc):
    s_blk = SEQ_BLOCK
    nc, nd = C_CONV_WIDTH - 1, POOL_MAX - 1
    full = lambda shape: pl.BlockSpec(shape, lambda s: (0,) * len(shape))
    cst = pl.BlockSpec((nc, s_blk, C_WIDTH), lambda s: (0, s, 0))
    dst = pl.BlockSpec((nd, s_blk, D_WIDTH), lambda s: (0, s, 0))
    return pl.pallas_call(
        _odd_s_kernel,
        grid=(DEC_BATCH // s_blk,),
        in_specs=[pl.BlockSpec((DEC_SEQ, s_blk, ODD_IN), lambda s: (0, s, 0)), cst, dst,
                  full((C_CONV_WIDTH, C_WIDTH)), full((1, C_WIDTH)), full((1, C_WIDTH)), full((1, C_WIDTH)),
                  full((D_GROUPS, D_GROUP_WIDTH, D_GROUP_WIDTH)), full((1, D_WIDTH)), full((1, D_WIDTH))],
        out_specs=[pl.BlockSpec((DEC_SEQ, s_blk, D_MODEL), lambda s: (0, s, 0)), cst, dst],
        out_shape=[jax.ShapeDtypeStruct((DEC_SEQ, DEC_BATCH, D_MODEL), BF16),
                   jax.ShapeDtypeStruct((nc, DEC_BATCH, C_WIDTH), F32),
                   jax.ShapeDtypeStruct((nd, DEC_BATCH, D_WIDTH), F32)],
        scratch_shapes=[pltpu.VMEM((DEC_SEQ, s_blk, C_WIDTH), F32),
                        pltpu.VMEM((DEC_SEQ * s_blk, D_WIDTH), BF16)],
        compiler_params=_params(1),
        name="odd_s",
    )(z3, cs, ds, cw, cb, lng, lnb, dw, db, dsc)


FFN_TF = 512
FFN_TM = 1024
FFN_SPLIT = 4


def _ffn_p_kernel(*refs, cast_next):
    h_ref, wg_ref, wv_ref, wd_ref, cwg_ref, cwv_ref, cbg_ref, cbv_ref = refs[:8]
    if cast_next:
        nwu_ref, nwd_ref, o_ref, nfg_ref, nfv_ref, nwu_out, nwd_out, carry_g, carry_v = refs[8:]
        nwu_out[...] = nwu_ref[...].astype(BF16)
        nwd_out[...] = nwd_ref[...].astype(BF16)
    else:
        o_ref, nfg_ref, nfv_ref, carry_g, carry_v = refs[8:]
    i, f = pl.program_id(1), pl.program_id(2)
    rh = FFN_TM // FFN_SPLIT

    @pl.when(i == 0)
    def _():
        carry_g[f] = jnp.zeros((SUBLANES, FFN_TF), F32)
        carry_v[f] = jnp.zeros((SUBLANES, FFN_TF), F32)

    @pl.when(f == 0)
    def _():
        o_ref[...] = jnp.zeros_like(o_ref)

    def conv(up, prev, cw_ref, cb_ref):
        w0, w1, w2, cb = cw_ref[0:1, :], cw_ref[1:2, :], cw_ref[2:3, :], cb_ref[...]
        body = w2 * up + w1 * pltpu.roll(up, 1, 0) + w0 * pltpu.roll(up, 2, 0) + cb
        both = jnp.concatenate([prev, up[0:SUBLANES, :]], axis=0)
        m1 = pltpu.roll(both, 1, 0)[SUBLANES:, :]
        m2 = pltpu.roll(both, 2, 0)[SUBLANES:, :]
        head = w2 * up[0:SUBLANES, :] + w1 * m1 + w0 * m2 + cb
        return jnp.concatenate([head, body[SUBLANES:, :]], axis=0)

    ups = []
    for s in range(FFN_SPLIT):
        hs = h_ref[rh * s:rh * (s + 1), :]
        ups.append((jnp.dot(hs, wg_ref[...], preferred_element_type=F32),
                    jnp.dot(hs, wv_ref[...], preferred_element_type=F32)))
    prev_g, prev_v = carry_g[f], carry_v[f]
    for s in range(FFN_SPLIT):
        ug, uv = ups[s]
        g = conv(ug, prev_g, cwg_ref, cbg_ref)
        v = conv(uv, prev_v, cwv_ref, cbv_ref)
        prev_g, prev_v = ug[rh - SUBLANES:, :], uv[rh - SUBLANES:, :]
        act = (_gelu_tanh(g) * v).astype(BF16)
        o_ref[rh * s:rh * (s + 1), :] += jnp.dot(act, wd_ref[...], preferred_element_type=F32)
    carry_g[f] = prev_g
    carry_v[f] = prev_v
    nfg_ref[...] = prev_g
    nfv_ref[...] = prev_v


def _ffn_p(h, n, t, w_up, w_down, cw, cb, layer, w_layer, name, cast_next=None):
    tm, tf = FFN_TM, FFN_TF
    nf = D_FF // tf
    tpb = t // tm
    steps = n * tpb * nf
    in_specs = [pl.BlockSpec((tm, D_MODEL), lambda b, i, f: (b * tpb + i, 0)),
                pl.BlockSpec((None, D_MODEL, tf), lambda b, i, f: (w_layer, 0, f)),
                pl.BlockSpec((None, D_MODEL, tf), lambda b, i, f: (w_layer, 0, nf + f)),
                pl.BlockSpec((None, tf, D_MODEL), lambda b, i, f: (w_layer, f, 0)),
                pl.BlockSpec((None, FFN_CONV_WIDTH, tf), lambda b, i, f: (layer, 0, f)),
                pl.BlockSpec((None, FFN_CONV_WIDTH, tf), lambda b, i, f: (layer, 0, nf + f)),
                pl.BlockSpec((None, 1, tf), lambda b, i, f: (layer, 0, f)),
                pl.BlockSpec((None, 1, tf), lambda b, i, f: (layer, 0, nf + f))]
    out_specs = [pl.BlockSpec((tm, D_MODEL), lambda b, i, f: (b * tpb + i, 0)),
                 pl.BlockSpec((None, None, SUBLANES, tf), lambda b, i, f: (b, i, 0, f)),
                 pl.BlockSpec((None, None, SUBLANES, tf), lambda b, i, f: (b, i, 0, f))]
    out_shape = [jax.ShapeDtypeStruct((n * t, D_MODEL), F32),
                 jax.ShapeDtypeStruct((n, tpb, SUBLANES, D_FF), F32),
                 jax.ShapeDtypeStruct((n, tpb, SUBLANES, D_FF), F32)]
    args = [h, w_up, w_up, w_down, cw, cw, cb, cb]
    if cast_next is not None:
        nwu, nwd, nl = cast_next
        cu, cd = 2 * D_FF // steps, D_FF // steps
        step = lambda b, i, f: (b * tpb + i) * nf + f
        in_specs += [pl.BlockSpec((None, D_MODEL, cu), lambda b, i, f: (nl, 0, step(b, i, f))),
                     pl.BlockSpec((None, cd, D_MODEL), lambda b, i, f: (nl, step(b, i, f), 0))]
        out_specs += [pl.BlockSpec((None, D_MODEL, cu), lambda b, i, f: (0, 0, step(b, i, f))),
                      pl.BlockSpec((None, cd, D_MODEL), lambda b, i, f: (0, step(b, i, f), 0))]
        out_shape += [jax.ShapeDtypeStruct((1, D_MODEL, 2 * D_FF), BF16),
                      jax.ShapeDtypeStruct((1, D_FF, D_MODEL), BF16)]
        args += [nwu, nwd]
    return pl.pallas_call(
        functools.partial(_ffn_p_kernel, cast_next=cast_next is not None),
        grid=(n, tpb, nf),
        in_specs=in_specs, out_specs=out_specs, out_shape=out_shape,
        scratch_shapes=[pltpu.VMEM((nf, SUBLANES, tf), F32), pltpu.VMEM((nf, SUBLANES, tf), F32)],
        compiler_params=_params(3),
        name=name,
    )(*args)


def _ffn_s_kernel(h_ref, wg_ref, wv_ref, wd_ref, cwg_ref, cwv_ref, cbg_ref, cbv_ref, sg_ref, sv_ref,
                  o_ref, nsg_ref, nsv_ref):
    f = pl.program_id(0)
    nb = DEC_BATCH
    h = h_ref[...]
    nst = FFN_CONV_WIDTH - 1

    def conv(w_ref, cw_ref, cb_ref, st_ref, ns_ref):
        up = jnp.dot(h, w_ref[...], preferred_element_type=F32)
        ext = [st_ref[j] for j in range(nst)]
        ext += [up[nb * t:nb * (t + 1), :] for t in range(DEC_SEQ)]
        for j in range(nst):
            ns_ref[j] = ext[DEC_SEQ + j]
        return [cw_ref[0:1, :] * ext[t] + cw_ref[1:2, :] * ext[t + 1] + cw_ref[2:3, :] * ext[t + 2] + cb_ref[...]
                for t in range(DEC_SEQ)]

    g = conv(wg_ref, cwg_ref, cbg_ref, sg_ref, nsg_ref)
    v = conv(wv_ref, cwv_ref, cbv_ref, sv_ref, nsv_ref)
    act = jnp.concatenate([(_gelu_tanh(g[t]) * v[t]).astype(BF16) for t in range(DEC_SEQ)], axis=0)
    part = jnp.dot(act, wd_ref[...], preferred_element_type=F32)

    @pl.when(f == 0)
    def _():
        o_ref[...] = part

    @pl.when(f > 0)
    def _():
        o_ref[...] += part


FFN_S_TF = 512


def _ffn_s(h, state, w_up, w_down, cw, cb, layer, w_layer, name):
    tf = FFN_S_TF
    nf = D_FF // tf
    m = h.shape[0]
    nst = FFN_CONV_WIDTH - 1
    return pl.pallas_call(
        _ffn_s_kernel,
        grid=(nf,),
        in_specs=[pl.BlockSpec((m, D_MODEL), lambda f: (0, 0)),
                  pl.BlockSpec((None, D_MODEL, tf), lambda f: (w_layer, 0, f)),
                  pl.BlockSpec((None, D_MODEL, tf), lambda f: (w_layer, 0, nf + f)),
                  pl.BlockSpec((None, tf, D_MODEL), lambda f: (w_layer, f, 0)),
                  pl.BlockSpec((None, FFN_CONV_WIDTH, tf), lambda f: (layer, 0, f)),
                  pl.BlockSpec((None, FFN_CONV_WIDTH, tf), lambda f: (layer, 0, nf + f)),
                  pl.BlockSpec((None, 1, tf), lambda f: (layer, 0, f)),
                  pl.BlockSpec((None, 1, tf), lambda f: (layer, 0, nf + f)),
                  pl.BlockSpec((nst, DEC_BATCH, tf), lambda f: (0, 0, f)),
                  pl.BlockSpec((nst, DEC_BATCH, tf), lambda f: (0, 0, nf + f))],
        out_specs=[pl.BlockSpec((m, D_MODEL), lambda f: (0, 0)),
                   pl.BlockSpec((nst, DEC_BATCH, tf), lambda f: (0, 0, f)),
                   pl.BlockSpec((nst, DEC_BATCH, tf), lambda f: (0, 0, f))],
        out_shape=[jax.ShapeDtypeStruct((m, D_MODEL), F32),
                   jax.ShapeDtypeStruct((nst, DEC_BATCH, D_FF), F32),
                   jax.ShapeDtypeStruct((nst, DEC_BATCH, D_FF), F32)],
        compiler_params=_params(1),
        name=name,
    )(h, w_up, w_up, w_down, cw, cw, cb, cb, state, state)


def _rope_tables(pos):
    half = ROT_DIM // 2
    inv = ROPE_THETA ** (-jnp.arange(half, dtype=F32) * 2.0 / ROT_DIM)
    ang = pos.astype(F32)[:, None] * inv[None, :]
    cos, sin = jnp.cos(ang), jnp.sin(ang)
    t = pos.shape[0]
    rest = A_HEAD_DIM - ROT_DIM
    ch = jnp.concatenate([cos, cos, jnp.ones((t, rest), F32)], axis=1)
    ah = jnp.concatenate([-sin, jnp.zeros((t, half + rest), F32)], axis=1)
    bh = jnp.concatenate([jnp.zeros((t, half), F32), sin, jnp.zeros((t, rest), F32)], axis=1)
    rep = LANES // A_HEAD_DIM
    return tuple(jnp.tile(x, (1, rep)) for x in (ch, ah, bh))


def _pad_rows(x, rows):
    return jnp.concatenate([x, jnp.zeros((rows - x.shape[0],) + x.shape[1:], x.dtype)], axis=0)


def kernel(x_prompt, x_sample, cache_a_k, cache_a_v, state_c_conv, state_d_pool, state_ffn_conv, c_prompt, c_sample, norm_g, ada_w, ada_b, w_in_e, w_out_e, a_sinks, b_ln_g, b_ln_b, b_ws, b_bias, w_in_o, w_out_o, c_conv_w, c_conv_b, c_ln_g, c_ln_b, d_w, d_b, d_scale, ffn_w_up, ffn_conv_w, ffn_conv_b, ffn_w_down):
    assert DEPTH == 2 and x_prompt.shape == (BATCH, SEQ, D_MODEL) and x_sample.shape == (DEC_BATCH, DEC_SEQ, D_MODEL)
    w_in_e_b, w_out_e_b = w_in_e.astype(BF16), w_out_e.astype(BF16)
    w_in_o_b, w_out_o_b = w_in_o.astype(BF16), w_out_o.astype(BF16)

    pad_rows = DEC_BATCH + SUBLANES
    c_all = _pad_rows(jnp.concatenate([c_sample, c_prompt], axis=0), pad_rows)
    mods = _ada(c_all, ada_w.reshape(2 * DEPTH, D_MODEL, 3 * D_MODEL), ada_b.reshape(2 * DEPTH, 1, 3 * D_MODEL))
    mods_p = mods[:, DEC_BATCH:DEC_BATCH + BATCH].reshape(2 * DEPTH, BATCH, 1, 3 * D_MODEL)
    gains = norm_g.reshape(4 * DEPTH, 1, D_MODEL)

    grp_p = _Group(BATCH * SEQ, 512, mods_p, SEQ)
    grp_s = _Group(DEC_BATCH * DEC_SEQ, DEC_BATCH * DEC_SEQ, mods, None)
    xp = x_prompt.reshape(BATCH * SEQ, D_MODEL)
    xs = jnp.transpose(x_sample, (1, 0, 2)).reshape(DEC_SEQ * DEC_BATCH, D_MODEL)

    tabs_p = _rope_tables(jnp.arange(SEQ))
    tabs_s = tuple(_pad_rows(x, SUBLANES) for x in _rope_tables(PAST_LEN + jnp.arange(DEC_SEQ)))

    row1 = lambda v: v.reshape(1, -1)
    bexp_p = jnp.repeat(jnp.transpose(b_bias[0]), B_HEAD_DIM, axis=1)
    ws4 = jnp.tril(b_ws[0])[:, :DEC_SEQ, :DEC_SEQ]
    wexp_s = jnp.repeat(jnp.transpose(ws4, (1, 2, 0)).reshape(DEC_SEQ * DEC_SEQ, B_HEADS), B_HEAD_DIM, axis=1)
    bexp_s = _pad_rows(bexp_p[:DEC_SEQ], SUBLANES)
    sinkrow = jnp.repeat(a_sinks[0], DEC_SEQ).reshape(A_HEADS * DEC_SEQ, 1)

    ffn_cb = ffn_conv_b.reshape(DEPTH, 1, 2 * D_FF)

    def run(grp, x, is_prompt):
        outs = {}
        tag = "p" if is_prompt else "s"
        (h,) = _resnorm(grp, x, None, gains, nxt=(0, 0))
        tm_mm = 1024 if is_prompt else grp.m
        if is_prompt:
            mix, nk, nv = _even_p(h, w_in_e_b, 0, a_sinks[0], tabs_p, b_ws[0], bexp_p,
                                  row1(b_ln_g[0]), row1(b_ln_b[0]))
            outs["ak"], outs["av"] = nk, nv
        else:
            z = _mm(h, w_in_e_b, 0, tm_mm, 896, "in_even_" + tag)
            mix, nk, nv, vb = _even_s(z.reshape(DEC_SEQ, DEC_BATCH, EVEN_IN),
                                      cache_a_k[0].reshape(DEC_BATCH * WINDOW, A_KV_WIDTH),
                                      cache_a_v[0].reshape(DEC_BATCH * WINDOW, A_KV_WIDTH),
                                      sinkrow, tabs_s, wexp_s, bexp_s, row1(b_ln_g[0]), row1(b_ln_b[0]))
            outs["ak"], outs["av"], outs["bv"] = nk, nv, vb
        x, h = _mm_res(grp, mix.reshape(grp.m, D_MODEL), w_out_e_b, 0, x, gains, (0, 1), (1, 2), "out_even_" + tag)
        x, h, outs["ff0"] = ffn(grp, x, h, 0, is_prompt, nxt=(2, 4))
        dwa, dba, dsa = d_w[0], row1(d_b[0]), row1(d_scale[0])
        cargs = (c_conv_w[0], row1(c_conv_b[0]), row1(c_ln_g[0]), row1(c_ln_b[0]), dwa, dba, dsa)
        if is_prompt:
            mix, ct, dt = _odd_p(h, w_in_o_b, 0, *cargs)
            outs["cc"] = ct[:, C_TAIL - (C_CONV_WIDTH - 1):]
            outs["dp"] = dt[:, D_TAIL - (POOL_MAX - 1):]
        else:
            z = _mm(h, w_in_o_b, 0, tm_mm, 1024, "in_odd_" + tag)
            mix, co, do = _odd_s(z.reshape(DEC_SEQ, DEC_BATCH, ODD_IN),
                                 jnp.transpose(state_c_conv[0], (1, 0, 2)),
                                 jnp.transpose(state_d_pool[0], (1, 0, 2)), *cargs)
            outs["cc"] = jnp.transpose(co, (1, 0, 2))
            outs["dp"] = jnp.transpose(do, (1, 0, 2))
        x, h = _mm_res(grp, mix.reshape(grp.m, D_MODEL), w_out_o_b, 0, x, gains, (2, 5), (3, 6), "out_odd_" + tag)
        x, _, outs["ff1"] = ffn(grp, x, h, 1, is_prompt, nxt=None)
        return x, outs

    ffn_wb = {0: (ffn_w_up[0:1].astype(BF16), ffn_w_down[0:1].astype(BF16))}

    def ffn(grp, x, h, i, is_prompt, nxt):
        nst = FFN_CONV_WIDTH - 1
        wu, wd = ffn_wb[i]
        if is_prompt:
            cast_next = (ffn_w_up, ffn_w_down, i + 1) if i + 1 < DEPTH else None
            res = _ffn_p(h, BATCH, SEQ, wu, wd, ffn_conv_w, ffn_cb, i, 0, "ffn_p", cast_next)
            out, nfg, nfv = res[:3]
            if cast_next is not None:
                ffn_wb[i + 1] = (res[3], res[4])
            nf = jnp.concatenate([nfg[:, -1, SUBLANES - nst:], nfv[:, -1, SUBLANES - nst:]], axis=-1)
        else:
            st = jnp.transpose(state_ffn_conv[i], (1, 0, 2))
            out, nsg, nsv = _ffn_s(h, st, wu, wd, ffn_conv_w, ffn_cb, i, 0, "ffn_s")
            nf = jnp.transpose(jnp.concatenate([nsg, nsv], axis=-1), (1, 0, 2))
        res = _resnorm(grp, x, out, gains, res=(2 * i + 1, 4 * i + 3), nxt=nxt)
        if nxt is None:
            return res[0], None, nf
        return res[0], res[1], nf

    yp, op = run(grp_p, xp, True)
    ys, os_ = run(grp_s, xs, False)

    kv5 = lambda a, nb: a.reshape(1, nb, WINDOW, A_KV_HEADS, A_HEAD_DIM)
    y_prompt = yp.reshape(BATCH, SEQ, D_MODEL)
    y_sample = jnp.transpose(ys.reshape(DEC_SEQ, DEC_BATCH, D_MODEL), (1, 0, 2))
    return (y_prompt, y_sample,
            kv5(op["ak"], BATCH), kv5(os_["ak"], DEC_BATCH), kv5(op["av"], BATCH), kv5(os_["av"], DEC_BATCH),
            jnp.transpose(os_["bv"], (1, 0, 2))[None],
            op["cc"][None], os_["cc"][None], op["dp"][None], os_["dp"][None],
            jnp.stack([op["ff0"], op["ff1"]]), jnp.stack([os_["ff0"], os_["ff1"]]))
```

```python
import functools

import jax
import jax.numpy as jnp
from jax import lax
from jax.experimental import pallas as pl
from jax.experimental.pallas import tpu as pltpu

D_MODEL = 2048
BATCH = 2
SEQ = 4096
DEPTH = 2
DEC_BATCH = 128
DEC_SEQ = 4
PAST_LEN = 8192
A_HEADS = 16
A_KV_HEADS = 4
A_HEAD_DIM = 64
A_Q_WIDTH = A_HEADS * A_HEAD_DIM
A_KV_WIDTH = A_KV_HEADS * A_HEAD_DIM
WINDOW = 128
ROT_DIM = A_HEAD_DIM // 4
ROPE_THETA = 500000.0
B_HEADS = 8
B_HEAD_DIM = 128
B_WIDTH = B_HEADS * B_HEAD_DIM
CHUNK = 128
C_WIDTH = 1024
C_CONV_WIDTH = 31
POOL_SIZES = (2, 4, 8, 16)
D_GROUPS = len(POOL_SIZES)
POOL_MAX = max(POOL_SIZES)
D_WIDTH = 1024
D_GROUP_WIDTH = D_WIDTH // D_GROUPS
D_FF = 5632
FFN_CONV_WIDTH = 3
EVEN_IN = A_Q_WIDTH + 2 * A_KV_WIDTH + 2 * B_WIDTH
ODD_IN = 2 * C_WIDTH + D_WIDTH
EPS = 1e-6
NEG_INF = -1e30

LANES = 128
SUBLANES = 8
VMEM_LIMIT = 56 * 1024 * 1024
EVEN_P_VMEM_LIMIT = 60 * 1024 * 1024

BF16 = jnp.bfloat16
F32 = jnp.float32

SEQ_BLOCK = 32
C_TAIL = 32
D_TAIL = 16


def _params(n_axes, vmem_limit=VMEM_LIMIT):
    return pltpu.CompilerParams(dimension_semantics=("arbitrary",) * n_axes,
                                vmem_limit_bytes=vmem_limit)


_GELU_C = 0.7978845608028654


def _gelu_tanh(x):
    hx = 0.5 * x
    return hx + hx * jnp.tanh(x * (_GELU_C + (_GELU_C * 0.044715) * (x * x)))


def _rms(x, g):
    return x * lax.rsqrt(jnp.mean(x * x, axis=-1, keepdims=True) + EPS) * g


def _layer_norm(x, g, b):
    mu = jnp.mean(x, axis=-1, keepdims=True)
    xc = x - mu
    var = jnp.mean(xc * xc, axis=-1, keepdims=True)
    return xc * lax.rsqrt(var + EPS) * g + b


def _rows(m, tm):
    r = m.shape[0]
    if r == 1 or r == tm:
        return m
    return jnp.concatenate([m] * (tm // r), axis=0)


def _rope(x, c, a, b):
    return x * c + pltpu.roll(x, LANES - ROT_DIM // 2, 1) * a + pltpu.roll(x, ROT_DIM // 2, 1) * b


def _ada_kernel(c_ref, w_ref, b_ref, o_ref):
    c = c_ref[...]
    s = (c * jax.nn.sigmoid(c)).astype(BF16)
    o_ref[...] = jnp.dot(s, w_ref[...].astype(BF16), preferred_element_type=F32) + b_ref[...]


def _ada(c_all, w, b):
    r = c_all.shape[0]
    tn = 1536
    n_sub, _, n = w.shape
    return pl.pallas_call(
        _ada_kernel,
        grid=(n_sub, n // tn),
        in_specs=[pl.BlockSpec((r, D_MODEL), lambda k, j: (0, 0)),
                  pl.BlockSpec((None, D_MODEL, tn), lambda k, j: (k, 0, j)),
                  pl.BlockSpec((None, 1, tn), lambda k, j: (k, 0, j))],
        out_specs=pl.BlockSpec((None, r, tn), lambda k, j: (k, 0, j)),
        out_shape=jax.ShapeDtypeStruct((n_sub, r, n), F32),
        compiler_params=_params(2),
        name="ada",
    )(c_all, w, b)


def _resnorm_kernel(*refs, has_res, has_next):
    it = iter(refs)
    x_ref = next(it)
    if has_res:
        o_ref, gate_ref, gpost_ref = next(it), next(it), next(it)
    if has_next:
        gpre_ref, scale_ref, shift_ref = next(it), next(it), next(it)
    if has_res:
        xo_ref = next(it)
    if has_next:
        ho_ref = next(it)
    x = x_ref[...]
    tm = x.shape[0]
    if has_res:
        x = x + _rows(gate_ref[...], tm) * _rms(o_ref[...], gpost_ref[...])
        xo_ref[...] = x
    if has_next:
        h = _rms(x, gpre_ref[...]) * (1.0 + _rows(scale_ref[...], tm)) + _rows(shift_ref[...], tm)
        ho_ref[...] = h.astype(BF16)


class _Group:
    def __init__(self, m, tm, mods, per_batch_rows):
        self.m, self.tm, self.mods = m, tm, mods
        self.per_batch_rows = per_batch_rows

    def mod_spec(self, k, col):
        if self.per_batch_rows is None:
            return pl.BlockSpec((None, DEC_BATCH, D_MODEL), lambda i: (k, 0, col))
        tpb = self.per_batch_rows // self.tm
        return pl.BlockSpec((None, None, 1, D_MODEL), lambda i: (k, i // tpb, 0, col))


def _resnorm(grp, x, out, gains, res=None, nxt=None):
    tm = grp.tm
    row = pl.BlockSpec((tm, D_MODEL), lambda i: (i, 0))
    args, specs, out_shapes, out_specs = [x], [row], [], []

    def gain_spec(idx):
        return pl.BlockSpec((None, 1, D_MODEL), lambda i: (idx, 0, 0))

    if res is not None:
        args += [out, grp.mods, gains]
        specs += [row, grp.mod_spec(res[0], 2), gain_spec(res[1])]
        out_shapes.append(jax.ShapeDtypeStruct((grp.m, D_MODEL), F32))
        out_specs.append(row)
    if nxt is not None:
        args += [gains, grp.mods, grp.mods]
        specs += [gain_spec(nxt[1]), grp.mod_spec(nxt[0], 1), grp.mod_spec(nxt[0], 0)]
        out_shapes.append(jax.ShapeDtypeStruct((grp.m, D_MODEL), BF16))
        out_specs.append(row)
    res_out = pl.pallas_call(
        functools.partial(_resnorm_kernel, has_res=res is not None, has_next=nxt is not None),
        grid=(grp.m // tm,),
        in_specs=specs, out_specs=out_specs, out_shape=out_shapes,
        compiler_params=_params(1),
        name="resnorm",
    )(*args)
    return res_out


def _mm_kernel(a_ref, w_ref, o_ref):
    o_ref[...] = jnp.dot(a_ref[...], w_ref[...], preferred_element_type=F32).astype(o_ref.dtype)


def _mm(a, w, layer, tm, tn, name, out_dtype=F32):
    m, k = a.shape
    n = w.shape[2]
    return pl.pallas_call(
        _mm_kernel,
        grid=(n // tn, m // tm),
        in_specs=[pl.BlockSpec((tm, k), lambda j, i: (i, 0)),
                  pl.BlockSpec((None, k, tn), lambda j, i: (layer, 0, j))],
        out_specs=pl.BlockSpec((tm, tn), lambda j, i: (i, j)),
        out_shape=jax.ShapeDtypeStruct((m, n), out_dtype),
        compiler_params=_params(2),
        name=name,
    )(a, w)


MM_RES_SPLIT = 4


def _mm_res_kernel(a_ref, w_ref, x_ref, gate_ref, gpost_ref, gpre_ref, scale_ref, shift_ref, xo_ref, ho_ref):
    tm = a_ref.shape[0]
    rs = max(tm // MM_RES_SPLIT, gate_ref.shape[0])
    outs = [jnp.dot(a_ref[r:r + rs, :], w_ref[...], preferred_element_type=F32) for r in range(0, tm, rs)]
    for r, out in zip(range(0, tm, rs), outs):
        x = x_ref[r:r + rs, :] + _rows(gate_ref[...], rs) * _rms(out, gpost_ref[...])
        xo_ref[r:r + rs, :] = x
        h = _rms(x, gpre_ref[...]) * (1.0 + _rows(scale_ref[...], rs)) + _rows(shift_ref[...], rs)
        ho_ref[r:r + rs, :] = h.astype(BF16)


def _mm_res(grp, a, w, layer, x, gains, res, nxt, name):
    tm = grp.tm
    k = a.shape[1]
    row = pl.BlockSpec((tm, D_MODEL), lambda i: (i, 0))
    gain_spec = lambda idx: pl.BlockSpec((None, 1, D_MODEL), lambda i: (idx, 0, 0))
    return pl.pallas_call(
        _mm_res_kernel,
        grid=(grp.m // tm,),
        in_specs=[pl.BlockSpec((tm, k), lambda i: (i, 0)),
                  pl.BlockSpec((None, k, D_MODEL), lambda i: (layer, 0, 0)),
                  row, grp.mod_spec(res[0], 2), gain_spec(res[1]),
                  gain_spec(nxt[1]), grp.mod_spec(nxt[0], 1), grp.mod_spec(nxt[0], 0)],
        out_specs=[row, row],
        out_shape=[jax.ShapeDtypeStruct((grp.m, D_MODEL), F32), jax.ShapeDtypeStruct((grp.m, D_MODEL), BF16)],
        compiler_params=_params(1),
        name=name,
    )(a, w, x, grp.mods, gains, gains, grp.mods, grp.mods)


CORE_ROWS = 512
CORE_HALF = CORE_ROWS // 2


PROJ_CHUNK = 512


def _proj_chunks(dst, lhs_ref, r0, w_ref):
    def make(c):
        def run():
            cols = slice(c, c + PROJ_CHUNK)
            dst[:, cols] = jnp.dot(lhs_ref[r0:r0 + CORE_HALF, :], w_ref[:, cols], preferred_element_type=F32)
        return run
    return iter([make(c) for c in range(0, w_ref.shape[1], PROJ_CHUNK)])


def _issue(pre, n=1):
    for _ in range(n):
        thunk = next(pre, None)
        if thunk is not None:
            thunk()


def _even_block(pre, z, r0, o0, first_lim, sink_ref, cos_ref, sa_ref, sb_ref, ws_ref, bexp_ref, lng_ref, lnb_ref,
                mix_ref, nk_ref, nv_ref, kprev, vprev):
    qb = WINDOW
    zr = slice(r0, r0 + qb)
    orow = slice(o0, o0 + qb)
    cos_t, sa_t, sb_t = cos_ref[orow, :], sa_ref[orow, :], sb_ref[orow, :]
    lane = lax.broadcasted_iota(jnp.int32, (qb, LANES), 1)
    lo = lane < A_HEAD_DIM
    lane2 = lax.broadcasted_iota(jnp.int32, (2 * qb, LANES), 1)
    lo2 = lane2 < A_HEAD_DIM
    row = lax.broadcasted_iota(jnp.int32, (qb, 2 * qb), 0)
    col = lax.broadcasted_iota(jnp.int32, (qb, 2 * qb), 1)
    mask = ((col > row) & (col < qb)) | ((col >= qb) & (col - qb <= row))
    if first_lim is not None:
        mask = mask & (col >= first_lim)
    nt = (((1,), (1,)), ((), ()))

    for jk in range(A_KV_HEADS // 2):
        _issue(pre)
        ksl = slice(A_Q_WIDTH + LANES * jk, A_Q_WIDTH + LANES * (jk + 1))
        vsl = slice(A_Q_WIDTH + A_KV_WIDTH + LANES * jk, A_Q_WIDTH + A_KV_WIDTH + LANES * (jk + 1))
        csl = slice(LANES * jk, LANES * (jk + 1))
        kc = _rope(z[zr, ksl], cos_t, sa_t, sb_t)
        vc = z[zr, vsl]
        kall = jnp.concatenate([kprev[:, csl], kc], axis=0)
        vall = jnp.concatenate([vprev[:, csl], vc], axis=0)
        krol = pltpu.roll(kall, A_HEAD_DIM, 1)
        vrol = pltpu.roll(vall, A_HEAD_DIM, 1)
        for sub in range(2):
            kvh = 2 * jk + sub
            kd = (jnp.where(lo2, kall, krol) if sub == 0 else jnp.where(lo2, krol, kall)).astype(BF16)
            vd = (jnp.where(lo2, vall, vrol) if sub == 0 else jnp.where(lo2, vrol, vall)).astype(BF16)
            for qs in range(2):
                js = 2 * kvh + qs
                qsl = _rope(z[zr, LANES * js:LANES * (js + 1)], cos_t, sa_t, sb_t) * (A_HEAD_DIM ** -0.5)
                outs = []
                for half in range(2):
                    h = 2 * js + half
                    qm = jnp.where(lo if half == 0 else jnp.logical_not(lo), qsl, 0.0).astype(BF16)
                    s = lax.dot_general(qm, kd, nt, preferred_element_type=F32)
                    s = jnp.where(mask, s, NEG_INF)
                    sink = sink_ref[h]
                    m = jnp.maximum(jnp.max(s, axis=-1, keepdims=True), sink)
                    p = jnp.exp(s - m)
                    den = jnp.sum(p, axis=-1, keepdims=True) + jnp.exp(sink - m)
                    o = jnp.dot(p.astype(BF16), vd, preferred_element_type=F32)
                    outs.append(o * (1.0 / den))
                mix_ref[orow, LANES * js:LANES * (js + 1)] = jnp.where(lo, outs[0], outs[1]).astype(BF16)
        kprev[:, csl] = kc
        vprev[:, csl] = vc
        nk_ref[:, csl] = kc
        nv_ref[:, csl] = vc

    _issue(pre, 2)
    zb0 = A_Q_WIDTH + 2 * A_KV_WIDTH
    u = _gelu_tanh(z[zr, zb0:zb0 + B_WIDTH])
    vb = _layer_norm(_gelu_tanh(z[zr, zb0 + B_WIDTH:zb0 + 2 * B_WIDTH]), lng_ref[...], lnb_ref[...])
    ri = lax.broadcasted_iota(jnp.int32, (CHUNK, CHUNK), 0)
    ci = lax.broadcasted_iota(jnp.int32, (CHUNK, CHUNK), 1)
    tri = ri >= ci
    for h in range(B_HEADS):
        hs = slice(B_HEAD_DIM * h, B_HEAD_DIM * (h + 1))
        w = jnp.where(tri, ws_ref[h], 0.0).astype(BF16)
        mixed = jnp.dot(w, vb[:, hs].astype(BF16), preferred_element_type=F32) + bexp_ref[:, hs]
        mix_ref[orow, A_Q_WIDTH + B_HEAD_DIM * h:A_Q_WIDTH + B_HEAD_DIM * (h + 1)] = (u[:, hs] * mixed).astype(BF16)


def _even_p_kernel(sink_ref, xc_ref, xn_ref, w_hbm, gpre_ref, scc_ref, shc_ref, scn_ref, shn_ref,
                   cos_ref, sa_ref, sb_ref, ws_ref, bexp_ref, lng_ref, lnb_ref,
                   mix_ref, nk_ref, nv_ref, w_ref, za, zb, hb, hn, kprev, vprev, *, layer):
    step = pl.program_id(0)
    b = step % (SEQ // CORE_ROWS)
    args = (sink_ref, cos_ref, sa_ref, sb_ref, ws_ref, bexp_ref, lng_ref, lnb_ref, mix_ref, nk_ref, nv_ref,
            kprev, vprev)

    def prenorm(x, sc_ref, sh_ref):
        return (_rms(x, gpre_ref[...]) * (1.0 + sc_ref[...]) + sh_ref[...]).astype(BF16)

    @pl.when(step == 0)
    def _():
        pltpu.sync_copy(w_hbm.at[layer], w_ref)
        hb[...] = prenorm(xc_ref[0:CORE_HALF, :], scc_ref, shc_ref)
        za[...] = jnp.dot(hb[...], w_ref[...], preferred_element_type=F32)

    @pl.when(b == 0)
    def _():
        kprev[...] = jnp.zeros_like(kprev)
        vprev[...] = jnp.zeros_like(vprev)

    hb[...] = prenorm(xc_ref[CORE_HALF:CORE_ROWS, :], scc_ref, shc_ref)
    hn[...] = prenorm(xn_ref[...], scn_ref, shn_ref)
    pre = _proj_chunks(zb, hb, 0, w_ref)
    _even_block(pre, za, 0, 0, jnp.where(b > 0, 0, WINDOW), *args)
    _even_block(pre, za, WINDOW, WINDOW, None, *args)
    _issue(pre, EVEN_IN // PROJ_CHUNK)
    pre = _proj_chunks(za, hn, 0, w_ref)
    _even_block(pre, zb, 0, CORE_HALF, None, *args)
    _even_block(pre, zb, WINDOW, CORE_HALF + WINDOW, None, *args)
    _issue(pre, EVEN_IN // PROJ_CHUNK)


def _even_p(x, w, layer, gains, gain_idx, mods, k_mod, sinks, tabs, ws, bexp, lng, lnb):
    h = x
    rows = CORE_ROWS
    steps = h.shape[0] // rows
    per_seq = SEQ // rows
    last_half = h.shape[0] // CORE_HALF - 1
    nxt_half = lambda s: jnp.minimum(2 * s + 2, last_half)
    mod_c = lambda col: pl.BlockSpec((None, None, 1, D_MODEL), lambda s: (k_mod, s // per_seq, 0, col))
    mod_n = lambda col: pl.BlockSpec((None, None, 1, D_MODEL),
                                     lambda s: (k_mod, nxt_half(s) // (2 * per_seq), 0, col))
    full = lambda shape: pl.BlockSpec(shape, lambda s: (0,) * len(shape))
    tab = pl.BlockSpec((rows, LANES), lambda s: (s % per_seq, 0))
    kv = pl.BlockSpec((None, WINDOW, A_KV_WIDTH), lambda s: (s // per_seq, 0, 0))
    return pl.pallas_call(
        functools.partial(_even_p_kernel, layer=layer),
        grid=(steps,),
        in_specs=[pl.BlockSpec(memory_space=pltpu.SMEM),
                  pl.BlockSpec((rows, D_MODEL), lambda s: (s, 0)),
                  pl.BlockSpec((CORE_HALF, D_MODEL), lambda s: (nxt_half(s), 0)),
                  pl.BlockSpec(memory_space=pl.ANY),
                  pl.BlockSpec((None, 1, D_MODEL), lambda s: (gain_idx, 0, 0)),
                  mod_c(1), mod_c(0), mod_n(1), mod_n(0),
                  tab, tab, tab,
                  full((B_HEADS, CHUNK, CHUNK)),
                  pl.BlockSpec((CHUNK, B_WIDTH), lambda s: (0, 0)),
                  full((1, B_WIDTH)), full((1, B_WIDTH))],
        out_specs=[pl.BlockSpec((rows, D_MODEL), lambda s: (s, 0)), kv, kv],
        out_shape=[jax.ShapeDtypeStruct((h.shape[0], D_MODEL), BF16),
                   jax.ShapeDtypeStruct((BATCH, WINDOW, A_KV_WIDTH), F32),
                   jax.ShapeDtypeStruct((BATCH, WINDOW, A_KV_WIDTH), F32)],
        scratch_shapes=[pltpu.VMEM((D_MODEL, EVEN_IN), BF16),
                        pltpu.VMEM((CORE_HALF, EVEN_IN), F32), pltpu.VMEM((CORE_HALF, EVEN_IN), F32),
                        pltpu.VMEM((CORE_HALF, D_MODEL), BF16), pltpu.VMEM((CORE_HALF, D_MODEL), BF16),
                        pltpu.VMEM((WINDOW, A_KV_WIDTH), F32), pltpu.VMEM((WINDOW, A_KV_WIDTH), F32)],
        compiler_params=_params(1, EVEN_P_VMEM_LIMIT),
        name="even_p",
    )(sinks, x, x, w, gains, mods, mods, mods, mods, *tabs, ws, bexp, lng, lnb)


def _even_s_kernel(sinkrow_ref, z_ref, ck_ref, cv_ref, cos_ref, sa_ref, sb_ref, wexp_ref, bexp_ref,
                   lng_ref, lnb_ref, mix_ref, ok_ref, ov_ref, vb_ref, qm, osc, kn, vn):
    s_blk = SEQ_BLOCK
    rows_per_seq = A_HEADS * DEC_SEQ
    lane256 = lax.broadcasted_iota(jnp.int32, (s_blk, A_KV_WIDTH), 1)
    kn[...] = jnp.zeros_like(kn)
    vn[...] = jnp.zeros_like(vn)
    halves = A_KV_WIDTH // LANES

    def put(ref, start, stride, val):
        for c in range(halves):
            ref[c, pl.ds(start, s_blk, stride=stride), :] = val[:, LANES * c:LANES * (c + 1)]

    def get_rows(ref, start, size):
        return jnp.concatenate([ref[c, pl.ds(start, size), :] for c in range(halves)], axis=1)

    def get_strided(ref, start, stride):
        return jnp.concatenate([ref[c, pl.ds(start, s_blk, stride=stride), :] for c in range(halves)], axis=1)

    for t in range(DEC_SEQ):
        c, a, bb = cos_ref[t:t + 1, :], sa_ref[t:t + 1, :], sb_ref[t:t + 1, :]

        def rope2(lo_lane):
            return jnp.concatenate([_rope(z_ref[t, :, lo_lane:lo_lane + LANES], c, a, bb),
                                    _rope(z_ref[t, :, lo_lane + LANES:lo_lane + 2 * LANES], c, a, bb)], axis=1)

        put(kn, t, SUBLANES, rope2(A_Q_WIDTH))
        put(vn, t, SUBLANES, z_ref[t, :, A_Q_WIDTH + A_KV_WIDTH:A_Q_WIDTH + 2 * A_KV_WIDTH])
        for j in range(A_KV_HEADS):
            qs = rope2(A_KV_WIDTH * j) * (A_HEAD_DIM ** -0.5)
            keep = (lane256 >= A_HEAD_DIM * j) & (lane256 < A_HEAD_DIM * (j + 1))
            for g in range(A_HEADS // A_KV_HEADS):
                sh = ((j - g) * A_HEAD_DIM) % A_KV_WIDTH
                qr = qs if sh == 0 else pltpu.roll(qs, sh, 1)
                r = (4 * j + g) * DEC_SEQ + t
                put(qm, r, rows_per_seq, jnp.where(keep, qr, 0.0))

    rr = lax.broadcasted_iota(jnp.int32, (rows_per_seq, WINDOW), 0) % DEC_SEQ
    cc = lax.broadcasted_iota(jnp.int32, (rows_per_seq, WINDOW), 1)
    mask_old = cc > rr
    new0 = WINDOW - DEC_SEQ
    mask_new = (cc >= new0) & (cc - new0 <= rr)
    row8 = lax.broadcasted_iota(jnp.int32, (SUBLANES, A_KV_WIDTH), 0)
    sink = sinkrow_ref[...]
    nt = (((1,), (1,)), ((), ()))

    def shifted(c_ref, new_ref, o_ref, n):
        base = pl.multiple_of(n * WINDOW, WINDOW)
        old = c_ref[pl.ds(base, WINDOW), :]
        rolled = pltpu.roll(old, WINDOW - DEC_SEQ, 0)
        fresh = get_rows(new_ref, pl.multiple_of(n * SUBLANES, SUBLANES), SUBLANES)
        tail = jnp.where(row8 >= SUBLANES - DEC_SEQ, pltpu.roll(fresh, SUBLANES - DEC_SEQ, 0),
                         rolled[WINDOW - SUBLANES:, :])
        new = jnp.concatenate([rolled[:WINDOW - SUBLANES, :], tail], axis=0)
        o_ref[pl.ds(base, WINDOW), :] = new
        return old.astype(BF16), new.astype(BF16)

    def body(n, carry):
        k_old, k_new = shifted(ck_ref, kn, ok_ref, n)
        v_old, v_new = shifted(cv_ref, vn, ov_ref, n)
        q = get_rows(qm, pl.multiple_of(n * rows_per_seq, rows_per_seq), rows_per_seq).astype(BF16)
        s_old = jnp.where(mask_old, lax.dot_general(q, k_old, nt, preferred_element_type=F32), NEG_INF)
        s_new = jnp.where(mask_new, lax.dot_general(q, k_new, nt, preferred_element_type=F32), NEG_INF)
        m = jnp.maximum(jnp.maximum(jnp.max(s_old, axis=-1, keepdims=True),
                                    jnp.max(s_new, axis=-1, keepdims=True)), sink)
        p_old = jnp.exp(s_old - m)
        p_new = jnp.exp(s_new - m)
        den = (jnp.sum(p_old, axis=-1, keepdims=True) + jnp.sum(p_new, axis=-1, keepdims=True)
               + jnp.exp(sink - m))
        o = (jnp.dot(p_old.astype(BF16), v_old, preferred_element_type=F32)
             + jnp.dot(p_new.astype(BF16), v_new, preferred_element_type=F32))
        o = o * (1.0 / den)
        for c in range(halves):
            osc[c, pl.ds(pl.multiple_of(n * rows_per_seq, rows_per_seq), rows_per_seq), :] = o[:, LANES * c:LANES * (c + 1)]
        return carry

    lax.fori_loop(0, s_blk, body, 0, unroll=8)

    for t in range(DEC_SEQ):
        for j in range(A_KV_HEADS):
            acc = jnp.zeros((s_blk, A_KV_WIDTH), F32)
            for g in range(A_HEADS // A_KV_HEADS):
                r = (4 * j + g) * DEC_SEQ + t
                ov = get_strided(osc, r, rows_per_seq)
                sh = ((g - j) * A_HEAD_DIM) % A_KV_WIDTH
                orr = ov if sh == 0 else pltpu.roll(ov, sh, 1)
                acc = jnp.where((lane256 >= A_HEAD_DIM * g) & (lane256 < A_HEAD_DIM * (g + 1)), orr, acc)
            mix_ref[t, :, A_KV_WIDTH * j:A_KV_WIDTH * (j + 1)] = acc.astype(BF16)

    zb0 = A_Q_WIDTH + 2 * A_KV_WIDTH
    us, vbs = [], []
    for t in range(DEC_SEQ):
        us.append(_gelu_tanh(z_ref[t, :, zb0:zb0 + B_WIDTH]))
        v = _layer_norm(_gelu_tanh(z_ref[t, :, zb0 + B_WIDTH:zb0 + 2 * B_WIDTH]), lng_ref[...], lnb_ref[...])
        vb_ref[t] = v
        vbs.append(v)
    for t in range(DEC_SEQ):
        mixed = bexp_ref[t:t + 1, :]
        for s in range(t + 1):
            mixed = mixed + wexp_ref[DEC_SEQ * t + s:DEC_SEQ * t + s + 1, :] * vbs[s]
        mix_ref[t, :, A_Q_WIDTH:A_Q_WIDTH + B_WIDTH] = (us[t] * mixed).astype(BF16)


def _even_s(z3, ck, cv, sinkrow, tabs, wexp, bexp, lng, lnb):
    s_blk = SEQ_BLOCK
    rows_per_seq = A_HEADS * DEC_SEQ
    full = lambda shape: pl.BlockSpec(shape, lambda s: (0,) * len(shape))
    cache = pl.BlockSpec((s_blk * WINDOW, A_KV_WIDTH), lambda s: (s, 0))
    return pl.pallas_call(
        _even_s_kernel,
        grid=(DEC_BATCH // s_blk,),
        in_specs=[full((rows_per_seq, 1)),
                  pl.BlockSpec((DEC_SEQ, s_blk, EVEN_IN), lambda s: (0, s, 0)),
                  cache, cache,
                  full((SUBLANES, LANES)), full((SUBLANES, LANES)), full((SUBLANES, LANES)),
                  full((DEC_SEQ * DEC_SEQ, B_WIDTH)), full((SUBLANES, B_WIDTH)),
                  full((1, B_WIDTH)), full((1, B_WIDTH))],
        out_specs=[pl.BlockSpec((DEC_SEQ, s_blk, D_MODEL), lambda s: (0, s, 0)),
                   cache, cache,
                   pl.BlockSpec((DEC_SEQ, s_blk, B_WIDTH), lambda s: (0, s, 0))],
        out_shape=[jax.ShapeDtypeStruct((DEC_SEQ, DEC_BATCH, D_MODEL), BF16),
                   jax.ShapeDtypeStruct((DEC_BATCH * WINDOW, A_KV_WIDTH), F32),
                   jax.ShapeDtypeStruct((DEC_BATCH * WINDOW, A_KV_WIDTH), F32),
                   jax.ShapeDtypeStruct((DEC_SEQ, DEC_BATCH, B_WIDTH), F32)],
        scratch_shapes=[pltpu.VMEM((A_KV_WIDTH // LANES, s_blk * rows_per_seq, LANES), F32),
                        pltpu.VMEM((A_KV_WIDTH // LANES, s_blk * rows_per_seq, LANES), F32),
                        pltpu.VMEM((A_KV_WIDTH // LANES, s_blk * SUBLANES, LANES), F32),
                        pltpu.VMEM((A_KV_WIDTH // LANES, s_blk * SUBLANES, LANES), F32)],
        compiler_params=_params(1),
        name="even_s",
    )(sinkrow, z3, ck, cv, *tabs, wexp, bexp, lng, lnb)


ODD_ROWS = CORE_HALF
ODD_RC = 64


def _odd_block(pre, z_ref, o0, pos0, cw_ref, cb_ref, lng_ref, lnb_ref, dw_ref, db_ref, dsc_ref,
               mix_ref, ctail_ref, dtail_ref, gext, dext, cbuf, dbuf, shbuf):
    tr = ODD_ROWS
    _issue(pre, 2)
    gext[C_TAIL:C_TAIL + tr, :] = z_ref[:, 0:C_WIDTH] * jax.nn.sigmoid(z_ref[:, C_WIDTH:2 * C_WIDTH])
    dext[D_TAIL:D_TAIL + tr, :] = z_ref[:, 2 * C_WIDTH:2 * C_WIDTH + D_WIDTH]

    gw = D_GROUP_WIDTH
    lead = C_TAIL - (C_CONV_WIDTH - 1)
    sh_rows = C_TAIL + tr - SUBLANES
    for s in range(1, SUBLANES):
        shbuf[s, 0:sh_rows, :] = gext[s:s + sh_rows, :]

    def tap(off, r0, ls):
        a, s = divmod(off, SUBLANES)
        src = gext if s == 0 else shbuf.at[s]
        return src[r0 + SUBLANES * a:r0 + SUBLANES * a + ODD_RC, ls]

    for rc in range(tr // ODD_RC):
        r0 = rc * ODD_RC
        _issue(pre)
        for c in range(C_WIDTH // gw):
            ls = slice(gw * c, gw * (c + 1))
            acc = jnp.broadcast_to(cb_ref[:, ls], (ODD_RC, gw))
            for k in range(C_CONV_WIDTH):
                acc = acc + cw_ref[k:k + 1, ls] * tap(lead + k, r0, ls)
            cbuf[r0:r0 + ODD_RC, ls] = acc
        y = _layer_norm(cbuf[r0:r0 + ODD_RC, :], lng_ref[...], lnb_ref[...])
        mix_ref[o0 + r0:o0 + r0 + ODD_RC, 0:C_WIDTH] = (y * jax.nn.sigmoid(y)).astype(BF16)

        pos1 = (pos0 + r0 + 1 + lax.broadcasted_iota(jnp.int32, (ODD_RC, gw), 0))
        for g, wsz in enumerate(POOL_SIZES):
            ls = slice(gw * g, gw * (g + 1))
            win = dext[D_TAIL + r0:D_TAIL + r0 + ODD_RC, ls]
            for jj in range(1, wsz):
                win = win + dext[D_TAIL + r0 - jj:D_TAIL + r0 - jj + ODD_RC, ls]
            cnt = jnp.minimum(pos1, wsz).astype(F32)
            diff = win / cnt - dext[D_TAIL + r0:D_TAIL + r0 + ODD_RC, ls]
            dbuf[r0:r0 + ODD_RC, ls] = diff.astype(BF16)

    for g in range(D_GROUPS):
        ls = slice(gw * g, gw * (g + 1))
        o = jnp.dot(dbuf[:, ls], dw_ref[g].astype(BF16), preferred_element_type=F32) + db_ref[:, ls]
        mix_ref[o0:o0 + tr, C_WIDTH + gw * g:C_WIDTH + gw * (g + 1)] = (o * dsc_ref[:, ls]).astype(BF16)

    gext[0:C_TAIL, :] = gext[tr:tr + C_TAIL, :]
    dext[0:D_TAIL, :] = dext[tr:tr + D_TAIL, :]
    ctail_ref[...] = gext[0:C_TAIL, :]
    dtail_ref[...] = dext[0:D_TAIL, :]


def _odd_p_kernel(hc_ref, hn_ref, w_ref, cw_ref, cb_ref, lng_ref, lnb_ref, dw_ref, db_ref, dsc_ref,
                  mix_ref, ctail_ref, dtail_ref, za, zb, gext, dext, cbuf, dbuf, shbuf):
    step = pl.program_id(0)
    b = step % (SEQ // CORE_ROWS)
    args = (cw_ref, cb_ref, lng_ref, lnb_ref, dw_ref, db_ref, dsc_ref, mix_ref, ctail_ref, dtail_ref,
            gext, dext, cbuf, dbuf, shbuf)

    @pl.when(step == 0)
    def _():
        za[...] = jnp.dot(hc_ref[0:CORE_HALF, :], w_ref[...], preferred_element_type=F32)

    @pl.when(b == 0)
    def _():
        gext[0:C_TAIL, :] = jnp.zeros((C_TAIL, C_WIDTH), F32)
        dext[0:D_TAIL, :] = jnp.zeros((D_TAIL, D_WIDTH), F32)

    pre = _proj_chunks(zb, hc_ref, CORE_HALF, w_ref)
    _odd_block(pre, za, 0, b * CORE_ROWS, *args)
    _issue(pre, ODD_IN // PROJ_CHUNK)
    pre = _proj_chunks(za, hn_ref, 0, w_ref)
    _odd_block(pre, zb, CORE_HALF, b * CORE_ROWS + CORE_HALF, *args)
    _issue(pre, ODD_IN // PROJ_CHUNK)


def _odd_p(h, w, layer, cw, cb, lng, lnb, dw, db, dsc):
    rows = CORE_ROWS
    steps = h.shape[0] // rows
    per_seq = SEQ // rows
    last_half = h.shape[0] // CORE_HALF - 1
    tr = ODD_ROWS
    full = lambda shape: pl.BlockSpec(shape, lambda s: (0,) * len(shape))
    return pl.pallas_call(
        _odd_p_kernel,
        grid=(steps,),
        in_specs=[pl.BlockSpec((rows, D_MODEL), lambda s: (s, 0)),
                  pl.BlockSpec((CORE_HALF, D_MODEL), lambda s: (jnp.minimum(2 * s + 2, last_half), 0)),
                  pl.BlockSpec((None, D_MODEL, ODD_IN), lambda s: (layer, 0, 0)),
                  full((C_CONV_WIDTH, C_WIDTH)), full((1, C_WIDTH)), full((1, C_WIDTH)), full((1, C_WIDTH)),
                  full((D_GROUPS, D_GROUP_WIDTH, D_GROUP_WIDTH)), full((1, D_WIDTH)), full((1, D_WIDTH))],
        out_specs=[pl.BlockSpec((rows, D_MODEL), lambda s: (s, 0)),
                   pl.BlockSpec((None, C_TAIL, C_WIDTH), lambda s: (s // per_seq, 0, 0)),
                   pl.BlockSpec((None, D_TAIL, D_WIDTH), lambda s: (s // per_seq, 0, 0))],
        out_shape=[jax.ShapeDtypeStruct((h.shape[0], D_MODEL), BF16),
                   jax.ShapeDtypeStruct((BATCH, C_TAIL, C_WIDTH), F32),
                   jax.ShapeDtypeStruct((BATCH, D_TAIL, D_WIDTH), F32)],
        scratch_shapes=[pltpu.VMEM((CORE_HALF, ODD_IN), F32), pltpu.VMEM((CORE_HALF, ODD_IN), F32),
                        pltpu.VMEM((C_TAIL + tr, C_WIDTH), F32), pltpu.VMEM((D_TAIL + tr, D_WIDTH), F32),
                        pltpu.VMEM((tr, C_WIDTH), F32), pltpu.VMEM((tr, D_WIDTH), BF16),
                        pltpu.VMEM((SUBLANES, C_TAIL + tr, C_WIDTH), F32)],
        compiler_params=_params(1),
        name="odd_p",
    )(h, h, w, cw, cb, lng, lnb, dw, db, dsc)


def _odd_s_kernel(z_ref, cs_ref, ds_ref, cw_ref, cb_ref, lng_ref, lnb_ref, dw_ref, db_ref, dsc_ref,
                  mix_ref, co_ref, do_ref, cbuf, dbuf):
    s_blk = SEQ_BLOCK
    gw = D_GROUP_WIDTH
    nc, nd = C_CONV_WIDTH - 1, POOL_MAX - 1
    for c in range(C_WIDTH // gw):
        ls = slice(gw * c, gw * (c + 1))
        accs = [jnp.broadcast_to(cb_ref[:, ls], (s_blk, gw)) for _ in range(DEC_SEQ)]
        for j in range(nc + DEC_SEQ):
            if j < nc:
                e = cs_ref[j, :, ls]
            else:
                t = j - nc
                e = z_ref[t, :, gw * c:gw * (c + 1)] * jax.nn.sigmoid(
                    z_ref[t, :, C_WIDTH + gw * c:C_WIDTH + gw * (c + 1)])
            if j >= DEC_SEQ:
                co_ref[j - DEC_SEQ, :, ls] = e
            for t in range(DEC_SEQ):
                k = j - t
                if 0 <= k < C_CONV_WIDTH:
                    accs[t] = accs[t] + cw_ref[k:k + 1, ls] * e
        for t in range(DEC_SEQ):
            cbuf[t, :, ls] = accs[t]
    for t in range(DEC_SEQ):
        y = _layer_norm(cbuf[t], lng_ref[...], lnb_ref[...])
        mix_ref[t, :, 0:C_WIDTH] = (y * jax.nn.sigmoid(y)).astype(BF16)

    for g, wsz in enumerate(POOL_SIZES):
        ls = slice(gw * g, gw * (g + 1))
        ext = []
        for j in range(nd + DEC_SEQ):
            if j < nd:
                e = ds_ref[j, :, ls]
            else:
                e = z_ref[j - nd, :, 2 * C_WIDTH + gw * g:2 * C_WIDTH + gw * (g + 1)]
            if j >= DEC_SEQ:
                do_ref[j - DEC_SEQ, :, ls] = e
            ext.append(e)
        for t in range(DEC_SEQ):
            win = ext[nd + t]
            for jj in range(1, wsz):
                win = win + ext[nd + t - jj]
            cnt = float(min(PAST_LEN + t + 1, wsz))
            dbuf[s_blk * t:s_blk * (t + 1), ls] = (win / cnt - ext[nd + t]).astype(BF16)
    for g in range(D_GROUPS):
        ls = slice(gw * g, gw * (g + 1))
        o = jnp.dot(dbuf[:, ls], dw_ref[g].astype(BF16), preferred_element_type=F32) + db_ref[:, ls]
        o = (o * dsc_ref[:, ls]).astype(BF16)
        for t in range(DEC_SEQ):
            mix_ref[t, :, C_WIDTH + gw * g:C_WIDTH + gw * (g + 1)] = o[s_blk * t:s_blk * (t + 1), :]


def _odd_s(z3, cs, ds, cw, cb, lng, lnb, dw, db, dsc):
    s_blk = SEQ_BLOCK
    nc, nd = C_CONV_WIDTH - 1, POOL_MAX - 1
    full = lambda shape: pl.BlockSpec(shape, lambda s: (0,) * len(shape))
    cst = pl.BlockSpec((nc, s_blk, C_WIDTH), lambda s: (0, s, 0))
    dst = pl.BlockSpec((nd, s_blk, D_WIDTH), lambda s: (0, s, 0))
    return pl.pallas_call(
        _odd_s_kernel,
        grid=(DEC_BATCH // s_blk,),
        in_specs=[pl.BlockSpec((DEC_SEQ, s_blk, ODD_IN), lambda s: (0, s, 0)), cst, dst,
                  full((C_CONV_WIDTH, C_WIDTH)), full((1, C_WIDTH)), full((1, C_WIDTH)), full((1, C_WIDTH)),
                  full((D_GROUPS, D_GROUP_WIDTH, D_GROUP_WIDTH)), full((1, D_WIDTH)), full((1, D_WIDTH))],
        out_specs=[pl.BlockSpec((DEC_SEQ, s_blk, D_MODEL), lambda s: (0, s, 0)), cst, dst],
        out_shape=[jax.ShapeDtypeStruct((DEC_SEQ, DEC_BATCH, D_MODEL), BF16),
                   jax.ShapeDtypeStruct((nc, DEC_BATCH, C_WIDTH), F32),
                   jax.ShapeDtypeStruct((nd, DEC_BATCH, D_WIDTH), F32)],
        scratch_shapes=[pltpu.VMEM((DEC_SEQ, s_blk, C_WIDTH), F32),
                        pltpu.VMEM((DEC_SEQ * s_blk, D_WIDTH), BF16)],
        compiler_params=_params(1),
        name="odd_s",
    )(z3, cs, ds, cw, cb, lng, lnb, dw, db, dsc)


FFN_TF = 512
FFN_TM = 1024
FFN_SUB_ROWS = (256, 256, 256, 256)
assert sum(FFN_SUB_ROWS) == FFN_TM


def _ffn_p_kernel(*refs, cast_next):
    h_ref, wg_ref, wv_ref, wd_ref, cwg_ref, cwv_ref, cbg_ref, cbv_ref = refs[:8]
    if cast_next:
        nwu_ref, nwd_ref, o_ref, nfg_ref, nfv_ref, nwu_out, nwd_out, carry_g, carry_v = refs[8:]
        nwu_out[...] = nwu_ref[...].astype(BF16)
        nwd_out[...] = nwd_ref[...].astype(BF16)
    else:
        o_ref, nfg_ref, nfv_ref, carry_g, carry_v = refs[8:]
    i, f = pl.program_id(1), pl.program_id(2)
    starts = [sum(FFN_SUB_ROWS[:s]) for s in range(len(FFN_SUB_ROWS))]

    @pl.when(i == 0)
    def _():
        carry_g[f] = jnp.zeros((SUBLANES, FFN_TF), F32)
        carry_v[f] = jnp.zeros((SUBLANES, FFN_TF), F32)

    @pl.when(f == 0)
    def _():
        o_ref[...] = jnp.zeros_like(o_ref)

    def conv(up, prev, cw_ref, cb_ref):
        w0, w1, w2, cb = cw_ref[0:1, :], cw_ref[1:2, :], cw_ref[2:3, :], cb_ref[...]
        body = w2 * up + w1 * pltpu.roll(up, 1, 0) + w0 * pltpu.roll(up, 2, 0) + cb
        both = jnp.concatenate([prev, up[0:SUBLANES, :]], axis=0)
        m1 = pltpu.roll(both, 1, 0)[SUBLANES:, :]
        m2 = pltpu.roll(both, 2, 0)[SUBLANES:, :]
        head = w2 * up[0:SUBLANES, :] + w1 * m1 + w0 * m2 + cb
        return jnp.concatenate([head, body[SUBLANES:, :]], axis=0)

    ups = []
    for r0, rh in zip(starts, FFN_SUB_ROWS):
        hs = h_ref[r0:r0 + rh, :]
        ups.append((jnp.dot(hs, wg_ref[...], preferred_element_type=F32),
                    jnp.dot(hs, wv_ref[...], preferred_element_type=F32)))
    prev_g, prev_v = carry_g[f], carry_v[f]
    for (ug, uv), r0, rh in zip(ups, starts, FFN_SUB_ROWS):
        g = conv(ug, prev_g, cwg_ref, cbg_ref)
        v = conv(uv, prev_v, cwv_ref, cbv_ref)
        prev_g, prev_v = ug[rh - SUBLANES:, :], uv[rh - SUBLANES:, :]
        act = (_gelu_tanh(g) * v).astype(BF16)
        o_ref[r0:r0 + rh, :] += jnp.dot(act, wd_ref[...], preferred_element_type=F32)
    carry_g[f] = prev_g
    carry_v[f] = prev_v
    nfg_ref[...] = prev_g
    nfv_ref[...] = prev_v


def _ffn_p(h, n, t, w_up, w_down, cw, cb, layer, w_layer, name, cast_next=None):
    tm, tf = FFN_TM, FFN_TF
    nf = D_FF // tf
    tpb = t // tm
    steps = n * tpb * nf
    in_specs = [pl.BlockSpec((tm, D_MODEL), lambda b, i, f: (b * tpb + i, 0)),
                pl.BlockSpec((None, D_MODEL, tf), lambda b, i, f: (w_layer, 0, f)),
                pl.BlockSpec((None, D_MODEL, tf), lambda b, i, f: (w_layer, 0, nf + f)),
                pl.BlockSpec((None, tf, D_MODEL), lambda b, i, f: (w_layer, f, 0)),
                pl.BlockSpec((None, FFN_CONV_WIDTH, tf), lambda b, i, f: (layer, 0, f)),
                pl.BlockSpec((None, FFN_CONV_WIDTH, tf), lambda b, i, f: (layer, 0, nf + f)),
                pl.BlockSpec((None, 1, tf), lambda b, i, f: (layer, 0, f)),
                pl.BlockSpec((None, 1, tf), lambda b, i, f: (layer, 0, nf + f))]
    out_specs = [pl.BlockSpec((tm, D_MODEL), lambda b, i, f: (b * tpb + i, 0)),
                 pl.BlockSpec((None, None, SUBLANES, tf), lambda b, i, f: (b, i, 0, f)),
                 pl.BlockSpec((None, None, SUBLANES, tf), lambda b, i, f: (b, i, 0, f))]
    out_shape = [jax.ShapeDtypeStruct((n * t, D_MODEL), F32),
                 jax.ShapeDtypeStruct((n, tpb, SUBLANES, D_FF), F32),
                 jax.ShapeDtypeStruct((n, tpb, SUBLANES, D_FF), F32)]
    args = [h, w_up, w_up, w_down, cw, cw, cb, cb]
    if cast_next is not None:
        nwu, nwd, nl = cast_next
        cu, cd = 2 * D_FF // steps, D_FF // steps
        step = lambda b, i, f: (b * tpb + i) * nf + f
        in_specs += [pl.BlockSpec((None, D_MODEL, cu), lambda b, i, f: (nl, 0, step(b, i, f))),
                     pl.BlockSpec((None, cd, D_MODEL), lambda b, i, f: (nl, step(b, i, f), 0))]
        out_specs += [pl.BlockSpec((None, D_MODEL, cu), lambda b, i, f: (0, 0, step(b, i, f))),
                      pl.BlockSpec((None, cd, D_MODEL), lambda b, i, f: (0, step(b, i, f), 0))]
        out_shape += [jax.ShapeDtypeStruct((1, D_MODEL, 2 * D_FF), BF16),
                      jax.ShapeDtypeStruct((1, D_FF, D_MODEL), BF16)]
        args += [nwu, nwd]
    return pl.pallas_call(
        functools.partial(_ffn_p_kernel, cast_next=cast_next is not None),
        grid=(n, tpb, nf),
        in_specs=in_specs, out_specs=out_specs, out_shape=out_shape,
        scratch_shapes=[pltpu.VMEM((nf, SUBLANES, tf), F32), pltpu.VMEM((nf, SUBLANES, tf), F32)],
        compiler_params=_params(3),
        name=name,
    )(*args)


def _ffn_s_kernel(h_ref, wg_ref, wv_ref, wd_ref, cwg_ref, cwv_ref, cbg_ref, cbv_ref, sg_ref, sv_ref,
                  o_ref, nsg_ref, nsv_ref):
    f = pl.program_id(0)
    nb = DEC_BATCH
    h = h_ref[...]
    nst = FFN_CONV_WIDTH - 1

    def conv(w_ref, cw_ref, cb_ref, st_ref, ns_ref):
        up = jnp.dot(h, w_ref[...], preferred_element_type=F32)
        ext = [st_ref[j] for j in range(nst)]
        ext += [up[nb * t:nb * (t + 1), :] for t in range(DEC_SEQ)]
        for j in range(nst):
            ns_ref[j] = ext[DEC_SEQ + j]
        return [cw_ref[0:1, :] * ext[t] + cw_ref[1:2, :] * ext[t + 1] + cw_ref[2:3, :] * ext[t + 2] + cb_ref[...]
                for t in range(DEC_SEQ)]

    g = conv(wg_ref, cwg_ref, cbg_ref, sg_ref, nsg_ref)
    v = conv(wv_ref, cwv_ref, cbv_ref, sv_ref, nsv_ref)
    act = jnp.concatenate([(_gelu_tanh(g[t]) * v[t]).astype(BF16) for t in range(DEC_SEQ)], axis=0)
    part = jnp.dot(act, wd_ref[...], preferred_element_type=F32)

    @pl.when(f == 0)
    def _():
        o_ref[...] = part

    @pl.when(f > 0)
    def _():
        o_ref[...] += part


FFN_S_TF = 512


def _ffn_s(h, state, w_up, w_down, cw, cb, layer, w_layer, name):
    tf = FFN_S_TF
    nf = D_FF // tf
    m = h.shape[0]
    nst = FFN_CONV_WIDTH - 1
    return pl.pallas_call(
        _ffn_s_kernel,
        grid=(nf,),
        in_specs=[pl.BlockSpec((m, D_MODEL), lambda f: (0, 0)),
                  pl.BlockSpec((None, D_MODEL, tf), lambda f: (w_layer, 0, f)),
                  pl.BlockSpec((None, D_MODEL, tf), lambda f: (w_layer, 0, nf + f)),
                  pl.BlockSpec((None, tf, D_MODEL), lambda f: (w_layer, f, 0)),
                  pl.BlockSpec((None, FFN_CONV_WIDTH, tf), lambda f: (layer, 0, f)),
                  pl.BlockSpec((None, FFN_CONV_WIDTH, tf), lambda f: (layer, 0, nf + f)),
                  pl.BlockSpec((None, 1, tf), lambda f: (layer, 0, f)),
                  pl.BlockSpec((None, 1, tf), lambda f: (layer, 0, nf + f)),
                  pl.BlockSpec((nst, DEC_BATCH, tf), lambda f: (0, 0, f)),
                  pl.BlockSpec((nst, DEC_BATCH, tf), lambda f: (0, 0, nf + f))],
        out_specs=[pl.BlockSpec((m, D_MODEL), lambda f: (0, 0)),
                   pl.BlockSpec((nst, DEC_BATCH, tf), lambda f: (0, 0, f)),
                   pl.BlockSpec((nst, DEC_BATCH, tf), lambda f: (0, 0, f))],
        out_shape=[jax.ShapeDtypeStruct((m, D_MODEL), F32),
                   jax.ShapeDtypeStruct((nst, DEC_BATCH, D_FF), F32),
                   jax.ShapeDtypeStruct((nst, DEC_BATCH, D_FF), F32)],
        compiler_params=_params(1),
        name=name,
    )(h, w_up, w_up, w_down, cw, cw, cb, cb, state, state)


def _rope_tables(pos):
    half = ROT_DIM // 2
    inv = ROPE_THETA ** (-jnp.arange(half, dtype=F32) * 2.0 / ROT_DIM)
    ang = pos.astype(F32)[:, None] * inv[None, :]
    cos, sin = jnp.cos(ang), jnp.sin(ang)
    t = pos.shape[0]
    rest = A_HEAD_DIM - ROT_DIM
    ch = jnp.concatenate([cos, cos, jnp.ones((t, rest), F32)], axis=1)
    ah = jnp.concatenate([-sin, jnp.zeros((t, half + rest), F32)], axis=1)
    bh = jnp.concatenate([jnp.zeros((t, half), F32), sin, jnp.zeros((t, rest), F32)], axis=1)
    rep = LANES // A_HEAD_DIM
    return tuple(jnp.tile(x, (1, rep)) for x in (ch, ah, bh))


def _pad_rows(x, rows):
    return jnp.concatenate([x, jnp.zeros((rows - x.shape[0],) + x.shape[1:], x.dtype)], axis=0)


def kernel(x_prompt, x_sample, cache_a_k, cache_a_v, state_c_conv, state_d_pool, state_ffn_conv, c_prompt, c_sample, norm_g, ada_w, ada_b, w_in_e, w_out_e, a_sinks, b_ln_g, b_ln_b, b_ws, b_bias, w_in_o, w_out_o, c_conv_w, c_conv_b, c_ln_g, c_ln_b, d_w, d_b, d_scale, ffn_w_up, ffn_conv_w, ffn_conv_b, ffn_w_down):
    assert DEPTH == 2 and x_prompt.shape == (BATCH, SEQ, D_MODEL) and x_sample.shape == (DEC_BATCH, DEC_SEQ, D_MODEL)
    w_in_e_b, w_out_e_b = w_in_e.astype(BF16), w_out_e.astype(BF16)
    w_in_o_b, w_out_o_b = w_in_o.astype(BF16), w_out_o.astype(BF16)

    pad_rows = DEC_BATCH + SUBLANES
    c_all = _pad_rows(jnp.concatenate([c_sample, c_prompt], axis=0), pad_rows)
    mods = _ada(c_all, ada_w.reshape(2 * DEPTH, D_MODEL, 3 * D_MODEL), ada_b.reshape(2 * DEPTH, 1, 3 * D_MODEL))
    mods_p = mods[:, DEC_BATCH:DEC_BATCH + BATCH].reshape(2 * DEPTH, BATCH, 1, 3 * D_MODEL)
    gains = norm_g.reshape(4 * DEPTH, 1, D_MODEL)

    grp_p = _Group(BATCH * SEQ, 512, mods_p, SEQ)
    grp_s = _Group(DEC_BATCH * DEC_SEQ, DEC_BATCH * DEC_SEQ, mods, None)
    xp = x_prompt.reshape(BATCH * SEQ, D_MODEL)
    xs = jnp.transpose(x_sample, (1, 0, 2)).reshape(DEC_SEQ * DEC_BATCH, D_MODEL)

    tabs_p = _rope_tables(jnp.arange(SEQ))
    tabs_s = tuple(_pad_rows(x, SUBLANES) for x in _rope_tables(PAST_LEN + jnp.arange(DEC_SEQ)))

    row1 = lambda v: v.reshape(1, -1)
    bexp_p = jnp.repeat(jnp.transpose(b_bias[0]), B_HEAD_DIM, axis=1)
    ws4 = jnp.tril(b_ws[0])[:, :DEC_SEQ, :DEC_SEQ]
    wexp_s = jnp.repeat(jnp.transpose(ws4, (1, 2, 0)).reshape(DEC_SEQ * DEC_SEQ, B_HEADS), B_HEAD_DIM, axis=1)
    bexp_s = _pad_rows(bexp_p[:DEC_SEQ], SUBLANES)
    sinkrow = jnp.repeat(a_sinks[0], DEC_SEQ).reshape(A_HEADS * DEC_SEQ, 1)

    ffn_cb = ffn_conv_b.reshape(DEPTH, 1, 2 * D_FF)

    def run(grp, x, is_prompt):
        outs = {}
        tag = "p" if is_prompt else "s"
        tm_mm = 1024 if is_prompt else grp.m
        if is_prompt:
            mix, nk, nv = _even_p(x, w_in_e_b, 0, gains, 0, grp.mods, 0, a_sinks[0], tabs_p, b_ws[0], bexp_p,
                                  row1(b_ln_g[0]), row1(b_ln_b[0]))
            outs["ak"], outs["av"] = nk, nv
        else:
            (h,) = _resnorm(grp, x, None, gains, nxt=(0, 0))
            z = _mm(h, w_in_e_b, 0, tm_mm, 896, "in_even_" + tag)
            mix, nk, nv, vb = _even_s(z.reshape(DEC_SEQ, DEC_BATCH, EVEN_IN),
                                      cache_a_k[0].reshape(DEC_BATCH * WINDOW, A_KV_WIDTH),
                                      cache_a_v[0].reshape(DEC_BATCH * WINDOW, A_KV_WIDTH),
                                      sinkrow, tabs_s, wexp_s, bexp_s, row1(b_ln_g[0]), row1(b_ln_b[0]))
            outs["ak"], outs["av"], outs["bv"] = nk, nv, vb
        x, h = _mm_res(grp, mix.reshape(grp.m, D_MODEL), w_out_e_b, 0, x, gains, (0, 1), (1, 2), "out_even_" + tag)
        x, h, outs["ff0"] = ffn(grp, x, h, 0, is_prompt, nxt=(2, 4))
        dwa, dba, dsa = d_w[0], row1(d_b[0]), row1(d_scale[0])
        cargs = (c_conv_w[0], row1(c_conv_b[0]), row1(c_ln_g[0]), row1(c_ln_b[0]), dwa, dba, dsa)
        if is_prompt:
            mix, ct, dt = _odd_p(h, w_in_o_b, 0, *cargs)
            outs["cc"] = ct[:, C_TAIL - (C_CONV_WIDTH - 1):]
            outs["dp"] = dt[:, D_TAIL - (POOL_MAX - 1):]
        else:
            z = _mm(h, w_in_o_b, 0, tm_mm, 1024, "in_odd_" + tag)
            mix, co, do = _odd_s(z.reshape(DEC_SEQ, DEC_BATCH, ODD_IN),
                                 jnp.transpose(state_c_conv[0], (1, 0, 2)),
                                 jnp.transpose(state_d_pool[0], (1, 0, 2)), *cargs)
            outs["cc"] = jnp.transpose(co, (1, 0, 2))
            outs["dp"] = jnp.transpose(do, (1, 0, 2))
        x, h = _mm_res(grp, mix.reshape(grp.m, D_MODEL), w_out_o_b, 0, x, gains, (2, 5), (3, 6), "out_odd_" + tag)
        x, _, outs["ff1"] = ffn(grp, x, h, 1, is_prompt, nxt=None)
        return x, outs

    ffn_wb = {0: (ffn_w_up[0:1].astype(BF16), ffn_w_down[0:1].astype(BF16))}

    def ffn(grp, x, h, i, is_prompt, nxt):
        nst = FFN_CONV_WIDTH - 1
        wu, wd = ffn_wb[i]
        if is_prompt:
            cast_next = (ffn_w_up, ffn_w_down, i + 1) if i + 1 < DEPTH else None
            res = _ffn_p(h, BATCH, SEQ, wu, wd, ffn_conv_w, ffn_cb, i, 0, "ffn_p", cast_next)
            out, nfg, nfv = res[:3]
            if cast_next is not None:
                ffn_wb[i + 1] = (res[3], res[4])
            nf = jnp.concatenate([nfg[:, -1, SUBLANES - nst:], nfv[:, -1, SUBLANES - nst:]], axis=-1)
        else:
            st = jnp.transpose(state_ffn_conv[i], (1, 0, 2))
            out, nsg, nsv = _ffn_s(h, st, wu, wd, ffn_conv_w, ffn_cb, i, 0, "ffn_s")
            nf = jnp.transpose(jnp.concatenate([nsg, nsv], axis=-1), (1, 0, 2))
        res = _resnorm(grp, x, out, gains, res=(2 * i + 1, 4 * i + 3), nxt=nxt)
        if nxt is None:
            return res[0], None, nf
        return res[0], res[1], nf

    yp, op = run(grp_p, xp, True)
    ys, os_ = run(grp_s, xs, False)

    kv5 = lambda a, nb: a.reshape(1, nb, WINDOW, A_KV_HEADS, A_HEAD_DIM)
    y_prompt = yp.reshape(BATCH, SEQ, D_MODEL)
    y_sample = jnp.transpose(ys.reshape(DEC_SEQ, DEC_BATCH, D_MODEL), (1, 0, 2))
    return (y_prompt, y_sample,
            kv5(op["ak"], BATCH), kv5(os_["ak"], DEC_BATCH), kv5(op["av"], BATCH), kv5(os_["av"], DEC_BATCH),
            jnp.transpose(os_["bv"], (1, 0, 2))[None],
            op["cc"][None], os_["cc"][None], op["dp"][None], os_["dp"][None],
            jnp.stack([op["ff0"], op["ff1"]]), jnp.stack([os_["ff0"], os_["ff1"]]))
```

```python
import functools

import jax
import jax.numpy as jnp
from jax import lax
from jax.experimental import pallas as pl
from jax.experimental.pallas import tpu as pltpu

D_MODEL = 2048
BATCH = 2
SEQ = 4096
DEPTH = 2
DEC_BATCH = 128
DEC_SEQ = 4
PAST_LEN = 8192
A_HEADS = 16
A_KV_HEADS = 4
A_HEAD_DIM = 64
A_Q_WIDTH = A_HEADS * A_HEAD_DIM
A_KV_WIDTH = A_KV_HEADS * A_HEAD_DIM
WINDOW = 128
ROT_DIM = A_HEAD_DIM // 4
ROPE_THETA = 500000.0
B_HEADS = 8
B_HEAD_DIM = 128
B_WIDTH = B_HEADS * B_HEAD_DIM
CHUNK = 128
C_WIDTH = 1024
C_CONV_WIDTH = 31
POOL_SIZES = (2, 4, 8, 16)
D_GROUPS = len(POOL_SIZES)
POOL_MAX = max(POOL_SIZES)
D_WIDTH = 1024
D_GROUP_WIDTH = D_WIDTH // D_GROUPS
D_FF = 5632
FFN_CONV_WIDTH = 3
EVEN_IN = A_Q_WIDTH + 2 * A_KV_WIDTH + 2 * B_WIDTH
ODD_IN = 2 * C_WIDTH + D_WIDTH
EPS = 1e-6
NEG_INF = -1e30

LANES = 128
SUBLANES = 8
VMEM_LIMIT = 56 * 1024 * 1024
EVEN_P_VMEM_LIMIT = 60 * 1024 * 1024

BF16 = jnp.bfloat16
F32 = jnp.float32

SEQ_BLOCK = 32
C_TAIL = 32
D_TAIL = 16


def _params(n_axes, vmem_limit=VMEM_LIMIT):
    return pltpu.CompilerParams(dimension_semantics=("arbitrary",) * n_axes,
                                vmem_limit_bytes=vmem_limit)


_GELU_C = 0.7978845608028654


def _gelu_tanh(x):
    hx = 0.5 * x
    return hx + hx * jnp.tanh(x * (_GELU_C + (_GELU_C * 0.044715) * (x * x)))


def _rms(x, g):
    return x * lax.rsqrt(jnp.mean(x * x, axis=-1, keepdims=True) + EPS) * g


def _layer_norm(x, g, b):
    mu = jnp.mean(x, axis=-1, keepdims=True)
    xc = x - mu
    var = jnp.mean(xc * xc, axis=-1, keepdims=True)
    return xc * lax.rsqrt(var + EPS) * g + b


def _rows(m, tm):
    r = m.shape[0]
    if r == 1 or r == tm:
        return m
    return jnp.concatenate([m] * (tm // r), axis=0)


def _rope(x, c, a, b):
    return x * c + pltpu.roll(x, LANES - ROT_DIM // 2, 1) * a + pltpu.roll(x, ROT_DIM // 2, 1) * b


def _ada_kernel(c_ref, w_ref, b_ref, o_ref):
    c = c_ref[...]
    s = (c * jax.nn.sigmoid(c)).astype(BF16)
    o_ref[...] = jnp.dot(s, w_ref[...].astype(BF16), preferred_element_type=F32) + b_ref[...]


def _ada(c_all, w, b):
    r = c_all.shape[0]
    tn = 1536
    n_sub, _, n = w.shape
    return pl.pallas_call(
        _ada_kernel,
        grid=(n_sub, n // tn),
        in_specs=[pl.BlockSpec((r, D_MODEL), lambda k, j: (0, 0)),
                  pl.BlockSpec((None, D_MODEL, tn), lambda k, j: (k, 0, j)),
                  pl.BlockSpec((None, 1, tn), lambda k, j: (k, 0, j))],
        out_specs=pl.BlockSpec((None, r, tn), lambda k, j: (k, 0, j)),
        out_shape=jax.ShapeDtypeStruct((n_sub, r, n), F32),
        compiler_params=_params(2),
        name="ada",
    )(c_all, w, b)


def _resnorm_kernel(*refs, has_res, has_next):
    it = iter(refs)
    x_ref = next(it)
    if has_res:
        o_ref, gate_ref, gpost_ref = next(it), next(it), next(it)
    if has_next:
        gpre_ref, scale_ref, shift_ref = next(it), next(it), next(it)
    if has_res:
        xo_ref = next(it)
    if has_next:
        ho_ref = next(it)
    x = x_ref[...]
    tm = x.shape[0]
    if has_res:
        x = x + _rows(gate_ref[...], tm) * _rms(o_ref[...], gpost_ref[...])
        xo_ref[...] = x
    if has_next:
        h = _rms(x, gpre_ref[...]) * (1.0 + _rows(scale_ref[...], tm)) + _rows(shift_ref[...], tm)
        ho_ref[...] = h.astype(BF16)


class _Group:
    def __init__(self, m, tm, mods, per_batch_rows):
        self.m, self.tm, self.mods = m, tm, mods
        self.per_batch_rows = per_batch_rows

    def mod_spec(self, k, col):
        if self.per_batch_rows is None:
            return pl.BlockSpec((None, DEC_BATCH, D_MODEL), lambda i: (k, 0, col))
        tpb = self.per_batch_rows // self.tm
        return pl.BlockSpec((None, None, 1, D_MODEL), lambda i: (k, i // tpb, 0, col))


def _resnorm(grp, x, out, gains, res=None, nxt=None):
    tm = grp.tm
    row = pl.BlockSpec((tm, D_MODEL), lambda i: (i, 0))
    args, specs, out_shapes, out_specs = [x], [row], [], []

    def gain_spec(idx):
        return pl.BlockSpec((None, 1, D_MODEL), lambda i: (idx, 0, 0))

    if res is not None:
        args += [out, grp.mods, gains]
        specs += [row, grp.mod_spec(res[0], 2), gain_spec(res[1])]
        out_shapes.append(jax.ShapeDtypeStruct((grp.m, D_MODEL), F32))
        out_specs.append(row)
    if nxt is not None:
        args += [gains, grp.mods, grp.mods]
        specs += [gain_spec(nxt[1]), grp.mod_spec(nxt[0], 1), grp.mod_spec(nxt[0], 0)]
        out_shapes.append(jax.ShapeDtypeStruct((grp.m, D_MODEL), BF16))
        out_specs.append(row)
    res_out = pl.pallas_call(
        functools.partial(_resnorm_kernel, has_res=res is not None, has_next=nxt is not None),
        grid=(grp.m // tm,),
        in_specs=specs, out_specs=out_specs, out_shape=out_shapes,
        compiler_params=_params(1),
        name="resnorm",
    )(*args)
    return res_out


def _mm_kernel(a_ref, w_ref, o_ref):
    o_ref[...] = jnp.dot(a_ref[...], w_ref[...], preferred_element_type=F32).astype(o_ref.dtype)


def _mm(a, w, layer, tm, tn, name, out_dtype=F32):
    m, k = a.shape
    n = w.shape[2]
    return pl.pallas_call(
        _mm_kernel,
        grid=(n // tn, m // tm),
        in_specs=[pl.BlockSpec((tm, k), lambda j, i: (i, 0)),
                  pl.BlockSpec((None, k, tn), lambda j, i: (layer, 0, j))],
        out_specs=pl.BlockSpec((tm, tn), lambda j, i: (i, j)),
        out_shape=jax.ShapeDtypeStruct((m, n), out_dtype),
        compiler_params=_params(2),
        name=name,
    )(a, w)


MM_RES_SPLIT = 4


def _mm_res_kernel(*refs, cast):
    a_ref, w_ref, x_ref, gate_ref, gpost_ref, gpre_ref, scale_ref, shift_ref = refs[:8]
    if cast:
        cin_ref, xo_ref, ho_ref, cout_ref = refs[8:]
        cout_ref[...] = cin_ref[...].astype(BF16)
    else:
        xo_ref, ho_ref = refs[8:]
    tm = a_ref.shape[0]
    rs = max(tm // MM_RES_SPLIT, gate_ref.shape[0])
    outs = [jnp.dot(a_ref[r:r + rs, :], w_ref[...], preferred_element_type=F32) for r in range(0, tm, rs)]
    for r, out in zip(range(0, tm, rs), outs):
        x = x_ref[r:r + rs, :] + _rows(gate_ref[...], rs) * _rms(out, gpost_ref[...])
        xo_ref[r:r + rs, :] = x
        h = _rms(x, gpre_ref[...]) * (1.0 + _rows(scale_ref[...], rs)) + _rows(shift_ref[...], rs)
        ho_ref[r:r + rs, :] = h.astype(BF16)


def _mm_res(grp, a, w, layer, x, gains, res, nxt, name, cast=None):
    tm = grp.tm
    k = a.shape[1]
    n = grp.m // tm
    row = pl.BlockSpec((tm, D_MODEL), lambda i: (i, 0))
    gain_spec = lambda idx: pl.BlockSpec((None, 1, D_MODEL), lambda i: (idx, 0, 0))
    in_specs = [pl.BlockSpec((tm, k), lambda i: (i, 0)),
                pl.BlockSpec((None, k, D_MODEL), lambda i: (layer, 0, 0)),
                row, grp.mod_spec(res[0], 2), gain_spec(res[1]),
                gain_spec(nxt[1]), grp.mod_spec(nxt[0], 1), grp.mod_spec(nxt[0], 0)]
    out_specs = [row, row]
    out_shape = [jax.ShapeDtypeStruct((grp.m, D_MODEL), F32), jax.ShapeDtypeStruct((grp.m, D_MODEL), BF16)]
    args = [a, w, x, grp.mods, gains, gains, grp.mods, grp.mods]
    if cast is not None:
        cw, cl = cast
        cr = cw.shape[1] // n
        in_specs.append(pl.BlockSpec((None, cr, cw.shape[2]), lambda i: (cl, i, 0)))
        out_specs.append(pl.BlockSpec((None, cr, cw.shape[2]), lambda i: (0, i, 0)))
        out_shape.append(jax.ShapeDtypeStruct((1,) + cw.shape[1:], BF16))
        args.append(cw)
    return pl.pallas_call(
        functools.partial(_mm_res_kernel, cast=cast is not None),
        grid=(n,),
        in_specs=in_specs, out_specs=out_specs, out_shape=out_shape,
        compiler_params=_params(1),
        name=name,
    )(*args)


CORE_ROWS = 512
CORE_HALF = CORE_ROWS // 2


PROJ_CHUNK = 512


def _proj_chunks(dst, lhs_ref, r0, w_ref):
    def make(c):
        def run():
            cols = slice(c, c + PROJ_CHUNK)
            dst[:, cols] = jnp.dot(lhs_ref[r0:r0 + CORE_HALF, :], w_ref[:, cols], preferred_element_type=F32)
        return run
    return iter([make(c) for c in range(0, w_ref.shape[1], PROJ_CHUNK)])


def _issue(pre, n=1):
    for _ in range(n):
        thunk = next(pre, None)
        if thunk is not None:
            thunk()


def _even_block(pre, z, r0, o0, first_lim, sink_ref, cos_ref, sa_ref, sb_ref, ws_ref, bexp_ref, lng_ref, lnb_ref,
                mix_ref, nk_ref, nv_ref, kprev, vprev):
    qb = WINDOW
    zr = slice(r0, r0 + qb)
    orow = slice(o0, o0 + qb)
    cos_t, sa_t, sb_t = cos_ref[orow, :], sa_ref[orow, :], sb_ref[orow, :]
    lane = lax.broadcasted_iota(jnp.int32, (qb, LANES), 1)
    lo = lane < A_HEAD_DIM
    lane2 = lax.broadcasted_iota(jnp.int32, (2 * qb, LANES), 1)
    lo2 = lane2 < A_HEAD_DIM
    row = lax.broadcasted_iota(jnp.int32, (qb, 2 * qb), 0)
    col = lax.broadcasted_iota(jnp.int32, (qb, 2 * qb), 1)
    mask = ((col > row) & (col < qb)) | ((col >= qb) & (col - qb <= row))
    if first_lim is not None:
        mask = mask & (col >= first_lim)
    nt = (((1,), (1,)), ((), ()))

    for jk in range(A_KV_HEADS // 2):
        _issue(pre)
        ksl = slice(A_Q_WIDTH + LANES * jk, A_Q_WIDTH + LANES * (jk + 1))
        vsl = slice(A_Q_WIDTH + A_KV_WIDTH + LANES * jk, A_Q_WIDTH + A_KV_WIDTH + LANES * (jk + 1))
        csl = slice(LANES * jk, LANES * (jk + 1))
        kc = _rope(z[zr, ksl], cos_t, sa_t, sb_t)
        vc = z[zr, vsl]
        kall = jnp.concatenate([kprev[:, csl], kc], axis=0)
        vall = jnp.concatenate([vprev[:, csl], vc], axis=0)
        krol = pltpu.roll(kall, A_HEAD_DIM, 1)
        vrol = pltpu.roll(vall, A_HEAD_DIM, 1)
        for sub in range(2):
            kvh = 2 * jk + sub
            kd = (jnp.where(lo2, kall, krol) if sub == 0 else jnp.where(lo2, krol, kall)).astype(BF16)
            vd = (jnp.where(lo2, vall, vrol) if sub == 0 else jnp.where(lo2, vrol, vall)).astype(BF16)
            for qs in range(2):
                js = 2 * kvh + qs
                qsl = _rope(z[zr, LANES * js:LANES * (js + 1)], cos_t, sa_t, sb_t) * (A_HEAD_DIM ** -0.5)
                outs = []
                for half in range(2):
                    h = 2 * js + half
                    qm = jnp.where(lo if half == 0 else jnp.logical_not(lo), qsl, 0.0).astype(BF16)
                    s = lax.dot_general(qm, kd, nt, preferred_element_type=F32)
                    s = jnp.where(mask, s, NEG_INF)
                    sink = sink_ref[h]
                    m = jnp.maximum(jnp.max(s, axis=-1, keepdims=True), sink)
                    p = jnp.exp(s - m)
                    den = jnp.sum(p, axis=-1, keepdims=True) + jnp.exp(sink - m)
                    o = jnp.dot(p.astype(BF16), vd, preferred_element_type=F32)
                    outs.append(o * (1.0 / den))
                mix_ref[orow, LANES * js:LANES * (js + 1)] = jnp.where(lo, outs[0], outs[1]).astype(BF16)
        kprev[:, csl] = kc
        vprev[:, csl] = vc
        nk_ref[:, csl] = kc
        nv_ref[:, csl] = vc

    _issue(pre, 2)
    zb0 = A_Q_WIDTH + 2 * A_KV_WIDTH
    u = _gelu_tanh(z[zr, zb0:zb0 + B_WIDTH])
    vb = _layer_norm(_gelu_tanh(z[zr, zb0 + B_WIDTH:zb0 + 2 * B_WIDTH]), lng_ref[...], lnb_ref[...])
    ri = lax.broadcasted_iota(jnp.int32, (CHUNK, CHUNK), 0)
    ci = lax.broadcasted_iota(jnp.int32, (CHUNK, CHUNK), 1)
    tri = ri >= ci
    for h in range(B_HEADS):
        hs = slice(B_HEAD_DIM * h, B_HEAD_DIM * (h + 1))
        w = jnp.where(tri, ws_ref[h], 0.0).astype(BF16)
        mixed = jnp.dot(w, vb[:, hs].astype(BF16), preferred_element_type=F32) + bexp_ref[:, hs]
        mix_ref[orow, A_Q_WIDTH + B_HEAD_DIM * h:A_Q_WIDTH + B_HEAD_DIM * (h + 1)] = (u[:, hs] * mixed).astype(BF16)


def _even_p_kernel(sink_ref, xc_ref, xn_ref, w_hbm, gpre_ref, scc_ref, shc_ref, scn_ref, shn_ref,
                   cos_ref, sa_ref, sb_ref, ws_ref, bexp_ref, lng_ref, lnb_ref,
                   mix_ref, nk_ref, nv_ref, w_ref, za, zb, hb, hn, kprev, vprev, *, layer):
    step = pl.program_id(0)
    b = step % (SEQ // CORE_ROWS)
    args = (sink_ref, cos_ref, sa_ref, sb_ref, ws_ref, bexp_ref, lng_ref, lnb_ref, mix_ref, nk_ref, nv_ref,
            kprev, vprev)

    def prenorm(x, sc_ref, sh_ref):
        return (_rms(x, gpre_ref[...]) * (1.0 + sc_ref[...]) + sh_ref[...]).astype(BF16)

    @pl.when(step == 0)
    def _():
        pltpu.sync_copy(w_hbm.at[layer], w_ref)
        hb[...] = prenorm(xc_ref[0:CORE_HALF, :], scc_ref, shc_ref)
        za[...] = jnp.dot(hb[...], w_ref[...], preferred_element_type=F32)

    @pl.when(b == 0)
    def _():
        kprev[...] = jnp.zeros_like(kprev)
        vprev[...] = jnp.zeros_like(vprev)

    hb[...] = prenorm(xc_ref[CORE_HALF:CORE_ROWS, :], scc_ref, shc_ref)
    hn[...] = prenorm(xn_ref[...], scn_ref, shn_ref)
    pre = _proj_chunks(zb, hb, 0, w_ref)
    _even_block(pre, za, 0, 0, jnp.where(b > 0, 0, WINDOW), *args)
    _even_block(pre, za, WINDOW, WINDOW, None, *args)
    _issue(pre, EVEN_IN // PROJ_CHUNK)
    pre = _proj_chunks(za, hn, 0, w_ref)
    _even_block(pre, zb, 0, CORE_HALF, None, *args)
    _even_block(pre, zb, WINDOW, CORE_HALF + WINDOW, None, *args)
    _issue(pre, EVEN_IN // PROJ_CHUNK)


def _even_p(x, w, layer, gains, gain_idx, mods, k_mod, sinks, tabs, ws, bexp, lng, lnb):
    h = x
    rows = CORE_ROWS
    steps = h.shape[0] // rows
    per_seq = SEQ // rows
    last_half = h.shape[0] // CORE_HALF - 1
    nxt_half = lambda s: jnp.minimum(2 * s + 2, last_half)
    mod_c = lambda col: pl.BlockSpec((None, None, 1, D_MODEL), lambda s: (k_mod, s // per_seq, 0, col))
    mod_n = lambda col: pl.BlockSpec((None, None, 1, D_MODEL),
                                     lambda s: (k_mod, nxt_half(s) // (2 * per_seq), 0, col))
    full = lambda shape: pl.BlockSpec(shape, lambda s: (0,) * len(shape))
    tab = pl.BlockSpec((rows, LANES), lambda s: (s % per_seq, 0))
    kv = pl.BlockSpec((None, WINDOW, A_KV_WIDTH), lambda s: (s // per_seq, 0, 0))
    return pl.pallas_call(
        functools.partial(_even_p_kernel, layer=layer),
        grid=(steps,),
        in_specs=[pl.BlockSpec(memory_space=pltpu.SMEM),
                  pl.BlockSpec((rows, D_MODEL), lambda s: (s, 0)),
                  pl.BlockSpec((CORE_HALF, D_MODEL), lambda s: (nxt_half(s), 0)),
                  pl.BlockSpec(memory_space=pl.ANY),
                  pl.BlockSpec((None, 1, D_MODEL), lambda s: (gain_idx, 0, 0)),
                  mod_c(1), mod_c(0), mod_n(1), mod_n(0),
                  tab, tab, tab,
                  full((B_HEADS, CHUNK, CHUNK)),
                  pl.BlockSpec((CHUNK, B_WIDTH), lambda s: (0, 0)),
                  full((1, B_WIDTH)), full((1, B_WIDTH))],
        out_specs=[pl.BlockSpec((rows, D_MODEL), lambda s: (s, 0)), kv, kv],
        out_shape=[jax.ShapeDtypeStruct((h.shape[0], D_MODEL), BF16),
                   jax.ShapeDtypeStruct((BATCH, WINDOW, A_KV_WIDTH), F32),
                   jax.ShapeDtypeStruct((BATCH, WINDOW, A_KV_WIDTH), F32)],
        scratch_shapes=[pltpu.VMEM((D_MODEL, EVEN_IN), BF16),
                        pltpu.VMEM((CORE_HALF, EVEN_IN), F32), pltpu.VMEM((CORE_HALF, EVEN_IN), F32),
                        pltpu.VMEM((CORE_HALF, D_MODEL), BF16), pltpu.VMEM((CORE_HALF, D_MODEL), BF16),
                        pltpu.VMEM((WINDOW, A_KV_WIDTH), F32), pltpu.VMEM((WINDOW, A_KV_WIDTH), F32)],
        compiler_params=_params(1, EVEN_P_VMEM_LIMIT),
        name="even_p",
    )(sinks, x, x, w, gains, mods, mods, mods, mods, *tabs, ws, bexp, lng, lnb)


def _even_s_kernel(sinkrow_ref, z_ref, ck_ref, cv_ref, cos_ref, sa_ref, sb_ref, wexp_ref, bexp_ref,
                   lng_ref, lnb_ref, mix_ref, ok_ref, ov_ref, vb_ref, qm, osc, kn, vn):
    s_blk = SEQ_BLOCK
    rows_per_seq = A_HEADS * DEC_SEQ
    lane256 = lax.broadcasted_iota(jnp.int32, (s_blk, A_KV_WIDTH), 1)
    kn[...] = jnp.zeros_like(kn)
    vn[...] = jnp.zeros_like(vn)
    halves = A_KV_WIDTH // LANES

    def put(ref, start, stride, val):
        for c in range(halves):
            ref[c, pl.ds(start, s_blk, stride=stride), :] = val[:, LANES * c:LANES * (c + 1)]

    def get_rows(ref, start, size):
        return jnp.concatenate([ref[c, pl.ds(start, size), :] for c in range(halves)], axis=1)

    def get_strided(ref, start, stride):
        return jnp.concatenate([ref[c, pl.ds(start, s_blk, stride=stride), :] for c in range(halves)], axis=1)

    for t in range(DEC_SEQ):
        c, a, bb = cos_ref[t:t + 1, :], sa_ref[t:t + 1, :], sb_ref[t:t + 1, :]

        def rope2(lo_lane):
            return jnp.concatenate([_rope(z_ref[t, :, lo_lane:lo_lane + LANES], c, a, bb),
                                    _rope(z_ref[t, :, lo_lane + LANES:lo_lane + 2 * LANES], c, a, bb)], axis=1)

        put(kn, t, SUBLANES, rope2(A_Q_WIDTH))
        put(vn, t, SUBLANES, z_ref[t, :, A_Q_WIDTH + A_KV_WIDTH:A_Q_WIDTH + 2 * A_KV_WIDTH])
        for j in range(A_KV_HEADS):
            qs = rope2(A_KV_WIDTH * j) * (A_HEAD_DIM ** -0.5)
            keep = (lane256 >= A_HEAD_DIM * j) & (lane256 < A_HEAD_DIM * (j + 1))
            for g in range(A_HEADS // A_KV_HEADS):
                sh = ((j - g) * A_HEAD_DIM) % A_KV_WIDTH
                qr = qs if sh == 0 else pltpu.roll(qs, sh, 1)
                r = (4 * j + g) * DEC_SEQ + t
                put(qm, r, rows_per_seq, jnp.where(keep, qr, 0.0))

    rr = lax.broadcasted_iota(jnp.int32, (rows_per_seq, WINDOW), 0) % DEC_SEQ
    cc = lax.broadcasted_iota(jnp.int32, (rows_per_seq, WINDOW), 1)
    mask_old = cc > rr
    new0 = WINDOW - DEC_SEQ
    mask_new = (cc >= new0) & (cc - new0 <= rr)
    row8 = lax.broadcasted_iota(jnp.int32, (SUBLANES, A_KV_WIDTH), 0)
    sink = sinkrow_ref[...]
    nt = (((1,), (1,)), ((), ()))

    def shifted(c_ref, new_ref, o_ref, n):
        base = pl.multiple_of(n * WINDOW, WINDOW)
        old = c_ref[pl.ds(base, WINDOW), :]
        rolled = pltpu.roll(old, WINDOW - DEC_SEQ, 0)
        fresh = get_rows(new_ref, pl.multiple_of(n * SUBLANES, SUBLANES), SUBLANES)
        tail = jnp.where(row8 >= SUBLANES - DEC_SEQ, pltpu.roll(fresh, SUBLANES - DEC_SEQ, 0),
                         rolled[WINDOW - SUBLANES:, :])
        new = jnp.concatenate([rolled[:WINDOW - SUBLANES, :], tail], axis=0)
        o_ref[pl.ds(base, WINDOW), :] = new
        return old.astype(BF16), new.astype(BF16)

    def body(n, carry):
        k_old, k_new = shifted(ck_ref, kn, ok_ref, n)
        v_old, v_new = shifted(cv_ref, vn, ov_ref, n)
        q = get_rows(qm, pl.multiple_of(n * rows_per_seq, rows_per_seq), rows_per_seq).astype(BF16)
        s_old = jnp.where(mask_old, lax.dot_general(q, k_old, nt, preferred_element_type=F32), NEG_INF)
        s_new = jnp.where(mask_new, lax.dot_general(q, k_new, nt, preferred_element_type=F32), NEG_INF)
        m = jnp.maximum(jnp.maximum(jnp.max(s_old, axis=-1, keepdims=True),
                                    jnp.max(s_new, axis=-1, keepdims=True)), sink)
        p_old = jnp.exp(s_old - m)
        p_new = jnp.exp(s_new - m)
        den = (jnp.sum(p_old, axis=-1, keepdims=True) + jnp.sum(p_new, axis=-1, keepdims=True)
               + jnp.exp(sink - m))
        o = (jnp.dot(p_old.astype(BF16), v_old, preferred_element_type=F32)
             + jnp.dot(p_new.astype(BF16), v_new, preferred_element_type=F32))
        o = o * (1.0 / den)
        for c in range(halves):
            osc[c, pl.ds(pl.multiple_of(n * rows_per_seq, rows_per_seq), rows_per_seq), :] = o[:, LANES * c:LANES * (c + 1)]
        return carry

    lax.fori_loop(0, s_blk, body, 0, unroll=8)

    for t in range(DEC_SEQ):
        for j in range(A_KV_HEADS):
            acc = jnp.zeros((s_blk, A_KV_WIDTH), F32)
            for g in range(A_HEADS // A_KV_HEADS):
                r = (4 * j + g) * DEC_SEQ + t
                ov = get_strided(osc, r, rows_per_seq)
                sh = ((g - j) * A_HEAD_DIM) % A_KV_WIDTH
                orr = ov if sh == 0 else pltpu.roll(ov, sh, 1)
                acc = jnp.where((lane256 >= A_HEAD_DIM * g) & (lane256 < A_HEAD_DIM * (g + 1)), orr, acc)
            mix_ref[t, :, A_KV_WIDTH * j:A_KV_WIDTH * (j + 1)] = acc.astype(BF16)

    zb0 = A_Q_WIDTH + 2 * A_KV_WIDTH
    us, vbs = [], []
    for t in range(DEC_SEQ):
        us.append(_gelu_tanh(z_ref[t, :, zb0:zb0 + B_WIDTH]))
        v = _layer_norm(_gelu_tanh(z_ref[t, :, zb0 + B_WIDTH:zb0 + 2 * B_WIDTH]), lng_ref[...], lnb_ref[...])
        vb_ref[t] = v
        vbs.append(v)
    for t in range(DEC_SEQ):
        mixed = bexp_ref[t:t + 1, :]
        for s in range(t + 1):
            mixed = mixed + wexp_ref[DEC_SEQ * t + s:DEC_SEQ * t + s + 1, :] * vbs[s]
        mix_ref[t, :, A_Q_WIDTH:A_Q_WIDTH + B_WIDTH] = (us[t] * mixed).astype(BF16)


def _even_s(z3, ck, cv, sinkrow, tabs, wexp, bexp, lng, lnb):
    s_blk = SEQ_BLOCK
    rows_per_seq = A_HEADS * DEC_SEQ
    full = lambda shape: pl.BlockSpec(shape, lambda s: (0,) * len(shape))
    cache = pl.BlockSpec((s_blk * WINDOW, A_KV_WIDTH), lambda s: (s, 0))
    return pl.pallas_call(
        _even_s_kernel,
        grid=(DEC_BATCH // s_blk,),
        in_specs=[full((rows_per_seq, 1)),
                  pl.BlockSpec((DEC_SEQ, s_blk, EVEN_IN), lambda s: (0, s, 0)),
                  cache, cache,
                  full((SUBLANES, LANES)), full((SUBLANES, LANES)), full((SUBLANES, LANES)),
                  full((DEC_SEQ * DEC_SEQ, B_WIDTH)), full((SUBLANES, B_WIDTH)),
                  full((1, B_WIDTH)), full((1, B_WIDTH))],
        out_specs=[pl.BlockSpec((DEC_SEQ, s_blk, D_MODEL), lambda s: (0, s, 0)),
                   cache, cache,
                   pl.BlockSpec((DEC_SEQ, s_blk, B_WIDTH), lambda s: (0, s, 0))],
        out_shape=[jax.ShapeDtypeStruct((DEC_SEQ, DEC_BATCH, D_MODEL), BF16),
                   jax.ShapeDtypeStruct((DEC_BATCH * WINDOW, A_KV_WIDTH), F32),
                   jax.ShapeDtypeStruct((DEC_BATCH * WINDOW, A_KV_WIDTH), F32),
                   jax.ShapeDtypeStruct((DEC_SEQ, DEC_BATCH, B_WIDTH), F32)],
        scratch_shapes=[pltpu.VMEM((A_KV_WIDTH // LANES, s_blk * rows_per_seq, LANES), F32),
                        pltpu.VMEM((A_KV_WIDTH // LANES, s_blk * rows_per_seq, LANES), F32),
                        pltpu.VMEM((A_KV_WIDTH // LANES, s_blk * SUBLANES, LANES), F32),
                        pltpu.VMEM((A_KV_WIDTH // LANES, s_blk * SUBLANES, LANES), F32)],
        compiler_params=_params(1),
        name="even_s",
    )(sinkrow, z3, ck, cv, *tabs, wexp, bexp, lng, lnb)


ODD_ROWS = CORE_HALF
ODD_RC = 64


def _odd_block(pre, z_ref, o0, pos0, cw_ref, cb_ref, lng_ref, lnb_ref, dw_ref, db_ref, dsc_ref,
               mix_ref, ctail_ref, dtail_ref, gext, dext, cbuf, dbuf, shbuf):
    tr = ODD_ROWS
    _issue(pre, 2)
    gext[C_TAIL:C_TAIL + tr, :] = z_ref[:, 0:C_WIDTH] * jax.nn.sigmoid(z_ref[:, C_WIDTH:2 * C_WIDTH])
    dext[D_TAIL:D_TAIL + tr, :] = z_ref[:, 2 * C_WIDTH:2 * C_WIDTH + D_WIDTH]

    gw = D_GROUP_WIDTH
    lead = C_TAIL - (C_CONV_WIDTH - 1)
    sh_rows = C_TAIL + tr - SUBLANES
    for s in range(1, SUBLANES):
        shbuf[s, 0:sh_rows, :] = gext[s:s + sh_rows, :]

    def tap(off, r0, ls):
        a, s = divmod(off, SUBLANES)
        src = gext if s == 0 else shbuf.at[s]
        return src[r0 + SUBLANES * a:r0 + SUBLANES * a + ODD_RC, ls]

    for rc in range(tr // ODD_RC):
        r0 = rc * ODD_RC
        _issue(pre)
        for c in range(C_WIDTH // gw):
            ls = slice(gw * c, gw * (c + 1))
            acc = jnp.broadcast_to(cb_ref[:, ls], (ODD_RC, gw))
            for k in range(C_CONV_WIDTH):
                acc = acc + cw_ref[k:k + 1, ls] * tap(lead + k, r0, ls)
            cbuf[r0:r0 + ODD_RC, ls] = acc
        y = _layer_norm(cbuf[r0:r0 + ODD_RC, :], lng_ref[...], lnb_ref[...])
        mix_ref[o0 + r0:o0 + r0 + ODD_RC, 0:C_WIDTH] = (y * jax.nn.sigmoid(y)).astype(BF16)

        pos1 = (pos0 + r0 + 1 + lax.broadcasted_iota(jnp.int32, (ODD_RC, gw), 0))
        for g, wsz in enumerate(POOL_SIZES):
            ls = slice(gw * g, gw * (g + 1))
            win = dext[D_TAIL + r0:D_TAIL + r0 + ODD_RC, ls]
            for jj in range(1, wsz):
                win = win + dext[D_TAIL + r0 - jj:D_TAIL + r0 - jj + ODD_RC, ls]
            cnt = jnp.minimum(pos1, wsz).astype(F32)
            diff = win / cnt - dext[D_TAIL + r0:D_TAIL + r0 + ODD_RC, ls]
            dbuf[r0:r0 + ODD_RC, ls] = diff.astype(BF16)

    for g in range(D_GROUPS):
        ls = slice(gw * g, gw * (g + 1))
        o = jnp.dot(dbuf[:, ls], dw_ref[g].astype(BF16), preferred_element_type=F32) + db_ref[:, ls]
        mix_ref[o0:o0 + tr, C_WIDTH + gw * g:C_WIDTH + gw * (g + 1)] = (o * dsc_ref[:, ls]).astype(BF16)

    gext[0:C_TAIL, :] = gext[tr:tr + C_TAIL, :]
    dext[0:D_TAIL, :] = dext[tr:tr + D_TAIL, :]
    ctail_ref[...] = gext[0:C_TAIL, :]
    dtail_ref[...] = dext[0:D_TAIL, :]


def _odd_p_kernel(hc_ref, hn_ref, w_ref, cw_ref, cb_ref, lng_ref, lnb_ref, dw_ref, db_ref, dsc_ref,
                  mix_ref, ctail_ref, dtail_ref, za, zb, gext, dext, cbuf, dbuf, shbuf):
    step = pl.program_id(0)
    b = step % (SEQ // CORE_ROWS)
    args = (cw_ref, cb_ref, lng_ref, lnb_ref, dw_ref, db_ref, dsc_ref, mix_ref, ctail_ref, dtail_ref,
            gext, dext, cbuf, dbuf, shbuf)

    @pl.when(step == 0)
    def _():
        za[...] = jnp.dot(hc_ref[0:CORE_HALF, :], w_ref[...], preferred_element_type=F32)

    @pl.when(b == 0)
    def _():
        gext[0:C_TAIL, :] = jnp.zeros((C_TAIL, C_WIDTH), F32)
        dext[0:D_TAIL, :] = jnp.zeros((D_TAIL, D_WIDTH), F32)

    pre = _proj_chunks(zb, hc_ref, CORE_HALF, w_ref)
    _odd_block(pre, za, 0, b * CORE_ROWS, *args)
    _issue(pre, ODD_IN // PROJ_CHUNK)
    pre = _proj_chunks(za, hn_ref, 0, w_ref)
    _odd_block(pre, zb, CORE_HALF, b * CORE_ROWS + CORE_HALF, *args)
    _issue(pre, ODD_IN // PROJ_CHUNK)


def _odd_p(h, w, layer, cw, cb, lng, lnb, dw, db, dsc):
    rows = CORE_ROWS
    steps = h.shape[0] // rows
    per_seq = SEQ // rows
    last_half = h.shape[0] // CORE_HALF - 1
    tr = ODD_ROWS
    full = lambda shape: pl.BlockSpec(shape, lambda s: (0,) * len(shape))
    return pl.pallas_call(
        _odd_p_kernel,
        grid=(steps,),
        in_specs=[pl.BlockSpec((rows, D_MODEL), lambda s: (s, 0)),
                  pl.BlockSpec((CORE_HALF, D_MODEL), lambda s: (jnp.minimum(2 * s + 2, last_half), 0)),
                  pl.BlockSpec((None, D_MODEL, ODD_IN), lambda s: (layer, 0, 0)),
                  full((C_CONV_WIDTH, C_WIDTH)), full((1, C_WIDTH)), full((1, C_WIDTH)), full((1, C_WIDTH)),
                  full((D_GROUPS, D_GROUP_WIDTH, D_GROUP_WIDTH)), full((1, D_WIDTH)), full((1, D_WIDTH))],
        out_specs=[pl.BlockSpec((rows, D_MODEL), lambda s: (s, 0)),
                   pl.BlockSpec((None, C_TAIL, C_WIDTH), lambda s: (s // per_seq, 0, 0)),
                   pl.BlockSpec((None, D_TAIL, D_WIDTH), lambda s: (s // per_seq, 0, 0))],
        out_shape=[jax.ShapeDtypeStruct((h.shape[0], D_MODEL), BF16),
                   jax.ShapeDtypeStruct((BATCH, C_TAIL, C_WIDTH), F32),
                   jax.ShapeDtypeStruct((BATCH, D_TAIL, D_WIDTH), F32)],
        scratch_shapes=[pltpu.VMEM((CORE_HALF, ODD_IN), F32), pltpu.VMEM((CORE_HALF, ODD_IN), F32),
                        pltpu.VMEM((C_TAIL + tr, C_WIDTH), F32), pltpu.VMEM((D_TAIL + tr, D_WIDTH), F32),
                        pltpu.VMEM((tr, C_WIDTH), F32), pltpu.VMEM((tr, D_WIDTH), BF16),
                        pltpu.VMEM((SUBLANES, C_TAIL + tr, C_WIDTH), F32)],
        compiler_params=_params(1),
        name="odd_p",
    )(h, h, w, cw, cb, lng, lnb, dw, db, dsc)


def _odd_s_kernel(z_ref, cs_ref, ds_ref, cw_ref, cb_ref, lng_ref, lnb_ref, dw_ref, db_ref, dsc_ref,
                  mix_ref, co_ref, do_ref, cbuf, dbuf):
    s_blk = SEQ_BLOCK
    gw = D_GROUP_WIDTH
    nc, nd = C_CONV_WIDTH - 1, POOL_MAX - 1
    for c in range(C_WIDTH // gw):
        ls = slice(gw * c, gw * (c + 1))
        accs = [jnp.broadcast_to(cb_ref[:, ls], (s_blk, gw)) for _ in range(DEC_SEQ)]
        for j in range(nc + DEC_SEQ):
            if j < nc:
                e = cs_ref[j, :, ls]
            else:
                t = j - nc
                e = z_ref[t, :, gw * c:gw * (c + 1)] * jax.nn.sigmoid(
                    z_ref[t, :, C_WIDTH + gw * c:C_WIDTH + gw * (c + 1)])
            if j >= DEC_SEQ:
                co_ref[j - DEC_SEQ, :, ls] = e
            for t in range(DEC_SEQ):
                k = j - t
                if 0 <= k < C_CONV_WIDTH:
                    accs[t] = accs[t] + cw_ref[k:k + 1, ls] * e
        for t in range(DEC_SEQ):
            cbuf[t, :, ls] = accs[t]
    for t in range(DEC_SEQ):
        y = _layer_norm(cbuf[t], lng_ref[...], lnb_ref[...])
        mix_ref[t, :, 0:C_WIDTH] = (y * jax.nn.sigmoid(y)).astype(BF16)

    for g, wsz in enumerate(POOL_SIZES):
        ls = slice(gw * g, gw * (g + 1))
        ext = []
        for j in range(nd + DEC_SEQ):
            if j < nd:
                e = ds_ref[j, :, ls]
            else:
                e = z_ref[j - nd, :, 2 * C_WIDTH + gw * g:2 * C_WIDTH + gw * (g + 1)]
            if j >= DEC_SEQ:
                do_ref[j - DEC_SEQ, :, ls] = e
            ext.append(e)
        for t in range(DEC_SEQ):
            win = ext[nd + t]
            for jj in range(1, wsz):
                win = win + ext[nd + t - jj]
            cnt = float(min(PAST_LEN + t + 1, wsz))
            dbuf[s_blk * t:s_blk * (t + 1), ls] = (win / cnt - ext[nd + t]).astype(BF16)
    for g in range(D_GROUPS):
        ls = slice(gw * g, gw * (g + 1))
        o = jnp.dot(dbuf[:, ls], dw_ref[g].astype(BF16), preferred_element_type=F32) + db_ref[:, ls]
        o = (o * dsc_ref[:, ls]).astype(BF16)
        for t in range(DEC_SEQ):
            mix_ref[t, :, C_WIDTH + gw * g:C_WIDTH + gw * (g + 1)] = o[s_blk * t:s_blk * (t + 1), :]


def _odd_s(z3, cs, ds, cw, cb, lng, lnb, dw, db, dsc):
    s_blk = SEQ_BLOCK
    nc, nd = C_CONV_WIDTH - 1, POOL_MAX - 1
    full = lambda shape: pl.BlockSpec(shape, lambda s: (0,) * len(shape))
    cst = pl.BlockSpec((nc, s_blk, C_WIDTH), lambda s: (0, s, 0))
    dst = pl.BlockSpec((nd, s_blk, D_WIDTH), lambda s: (0, s, 0))
    return pl.pallas_call(
        _odd_s_kernel,
        grid=(DEC_BATCH // s_blk,),
        in_specs=[pl.BlockSpec((DEC_SEQ, s_blk, ODD_IN), lambda s: (0, s, 0)), cst, dst,
                  full((C_CONV_WIDTH, C_WIDTH)), full((1, C_WIDTH)), full((1, C_WIDTH)), full((1, C_WIDTH)),
                  full((D_GROUPS, D_GROUP_WIDTH, D_GROUP_WIDTH)), full((1, D_WIDTH)), full((1, D_WIDTH))],
        out_specs=[pl.BlockSpec((DEC_SEQ, s_blk, D_MODEL), lambda s: (0, s, 0)), cst, dst],
        out_shape=[jax.ShapeDtypeStruct((DEC_SEQ, DEC_BATCH, D_MODEL), BF16),
                   jax.ShapeDtypeStruct((nc, DEC_BATCH, C_WIDTH), F32),
                   jax.ShapeDtypeStruct((nd, DEC_BATCH, D_WIDTH), F32)],
        scratch_shapes=[pltpu.VMEM((DEC_SEQ, s_blk, C_WIDTH), F32),
                        pltpu.VMEM((DEC_SEQ * s_blk, D_WIDTH), BF16)],
        compiler_params=_params(1),
        name="odd_s",
    )(z3, cs, ds, cw, cb, lng, lnb, dw, db, dsc)


FFN_TF = 512
FFN_TM = 1024
FFN_SUB_ROWS = (256, 256, 256, 256)
assert sum(FFN_SUB_ROWS) == FFN_TM


def _ffn_p_kernel(*refs, cast_next):
    h_ref, wg_ref, wv_ref, wd_ref, cwg_ref, cwv_ref, cbg_ref, cbv_ref = refs[:8]
    if cast_next:
        nwu_ref, nwd_ref, o_ref, nfg_ref, nfv_ref, nwu_out, nwd_out, carry_g, carry_v = refs[8:]
        nwu_out[...] = nwu_ref[...].astype(BF16)
        nwd_out[...] = nwd_ref[...].astype(BF16)
    else:
        o_ref, nfg_ref, nfv_ref, carry_g, carry_v = refs[8:]
    i, f = pl.program_id(1), pl.program_id(2)
    starts = [sum(FFN_SUB_ROWS[:s]) for s in range(len(FFN_SUB_ROWS))]

    @pl.when(i == 0)
    def _():
        carry_g[f] = jnp.zeros((SUBLANES, FFN_TF), F32)
        carry_v[f] = jnp.zeros((SUBLANES, FFN_TF), F32)

    @pl.when(f == 0)
    def _():
        o_ref[...] = jnp.zeros_like(o_ref)

    def conv(up, prev, cw_ref, cb_ref):
        w0, w1, w2, cb = cw_ref[0:1, :], cw_ref[1:2, :], cw_ref[2:3, :], cb_ref[...]
        body = w2 * up + w1 * pltpu.roll(up, 1, 0) + w0 * pltpu.roll(up, 2, 0) + cb
        both = jnp.concatenate([prev, up[0:SUBLANES, :]], axis=0)
        m1 = pltpu.roll(both, 1, 0)[SUBLANES:, :]
        m2 = pltpu.roll(both, 2, 0)[SUBLANES:, :]
        head = w2 * up[0:SUBLANES, :] + w1 * m1 + w0 * m2 + cb
        return jnp.concatenate([head, body[SUBLANES:, :]], axis=0)

    ups = []
    for r0, rh in zip(starts, FFN_SUB_ROWS):
        hs = h_ref[r0:r0 + rh, :]
        ups.append((jnp.dot(hs, wg_ref[...], preferred_element_type=F32),
                    jnp.dot(hs, wv_ref[...], preferred_element_type=F32)))
    prev_g, prev_v = carry_g[f], carry_v[f]
    for (ug, uv), r0, rh in zip(ups, starts, FFN_SUB_ROWS):
        g = conv(ug, prev_g, cwg_ref, cbg_ref)
        v = conv(uv, prev_v, cwv_ref, cbv_ref)
        prev_g, prev_v = ug[rh - SUBLANES:, :], uv[rh - SUBLANES:, :]
        act = (_gelu_tanh(g) * v).astype(BF16)
        o_ref[r0:r0 + rh, :] += jnp.dot(act, wd_ref[...], preferred_element_type=F32)
    carry_g[f] = prev_g
    carry_v[f] = prev_v
    nfg_ref[...] = prev_g
    nfv_ref[...] = prev_v


def _ffn_p(h, n, t, w_up, w_down, cw, cb, layer, w_layer, name, cast_next=None):
    tm, tf = FFN_TM, FFN_TF
    nf = D_FF // tf
    tpb = t // tm
    steps = n * tpb * nf
    in_specs = [pl.BlockSpec((tm, D_MODEL), lambda b, i, f: (b * tpb + i, 0)),
                pl.BlockSpec((None, D_MODEL, tf), lambda b, i, f: (w_layer, 0, f)),
                pl.BlockSpec((None, D_MODEL, tf), lambda b, i, f: (w_layer, 0, nf + f)),
                pl.BlockSpec((None, tf, D_MODEL), lambda b, i, f: (w_layer, f, 0)),
                pl.BlockSpec((None, FFN_CONV_WIDTH, tf), lambda b, i, f: (layer, 0, f)),
                pl.BlockSpec((None, FFN_CONV_WIDTH, tf), lambda b, i, f: (layer, 0, nf + f)),
                pl.BlockSpec((None, 1, tf), lambda b, i, f: (layer, 0, f)),
                pl.BlockSpec((None, 1, tf), lambda b, i, f: (layer, 0, nf + f))]
    out_specs = [pl.BlockSpec((tm, D_MODEL), lambda b, i, f: (b * tpb + i, 0)),
                 pl.BlockSpec((None, None, SUBLANES, tf), lambda b, i, f: (b, i, 0, f)),
                 pl.BlockSpec((None, None, SUBLANES, tf), lambda b, i, f: (b, i, 0, f))]
    out_shape = [jax.ShapeDtypeStruct((n * t, D_MODEL), F32),
                 jax.ShapeDtypeStruct((n, tpb, SUBLANES, D_FF), F32),
                 jax.ShapeDtypeStruct((n, tpb, SUBLANES, D_FF), F32)]
    args = [h, w_up, w_up, w_down, cw, cw, cb, cb]
    if cast_next is not None:
        nwu, nwd, nl = cast_next
        cu, cd = 2 * D_FF // steps, D_FF // steps
        step = lambda b, i, f: (b * tpb + i) * nf + f
        in_specs += [pl.BlockSpec((None, D_MODEL, cu), lambda b, i, f: (nl, 0, step(b, i, f))),
                     pl.BlockSpec((None, cd, D_MODEL), lambda b, i, f: (nl, step(b, i, f), 0))]
        out_specs += [pl.BlockSpec((None, D_MODEL, cu), lambda b, i, f: (0, 0, step(b, i, f))),
                      pl.BlockSpec((None, cd, D_MODEL), lambda b, i, f: (0, step(b, i, f), 0))]
        out_shape += [jax.ShapeDtypeStruct((1, D_MODEL, 2 * D_FF), BF16),
                      jax.ShapeDtypeStruct((1, D_FF, D_MODEL), BF16)]
        args += [nwu, nwd]
    return pl.pallas_call(
        functools.partial(_ffn_p_kernel, cast_next=cast_next is not None),
        grid=(n, tpb, nf),
        in_specs=in_specs, out_specs=out_specs, out_shape=out_shape,
        scratch_shapes=[pltpu.VMEM((nf, SUBLANES, tf), F32), pltpu.VMEM((nf, SUBLANES, tf), F32)],
        compiler_params=_params(3),
        name=name,
    )(*args)


def _ffn_s_kernel(h_ref, wg_ref, wv_ref, wd_ref, cwg_ref, cwv_ref, cbg_ref, cbv_ref, sg_ref, sv_ref,
                  o_ref, nsg_ref, nsv_ref):
    f = pl.program_id(0)
    nb = DEC_BATCH
    h = h_ref[...]
    nst = FFN_CONV_WIDTH - 1

    def conv(w_ref, cw_ref, cb_ref, st_ref, ns_ref):
        up = jnp.dot(h, w_ref[...], preferred_element_type=F32)
        ext = [st_ref[j] for j in range(nst)]
        ext += [up[nb * t:nb * (t + 1), :] for t in range(DEC_SEQ)]
        for j in range(nst):
            ns_ref[j] = ext[DEC_SEQ + j]
        return [cw_ref[0:1, :] * ext[t] + cw_ref[1:2, :] * ext[t + 1] + cw_ref[2:3, :] * ext[t + 2] + cb_ref[...]
                for t in range(DEC_SEQ)]

    @pl.when(f == 0)
    def _():
        o_ref[...] = jnp.zeros_like(o_ref)

    g = conv(wg_ref, cwg_ref, cbg_ref, sg_ref, nsg_ref)
    v = conv(wv_ref, cwv_ref, cbv_ref, sv_ref, nsv_ref)
    pair = 2
    for t0 in range(0, DEC_SEQ, pair):
        act = jnp.concatenate([(_gelu_tanh(g[t]) * v[t]).astype(BF16) for t in range(t0, t0 + pair)], axis=0)
        o_ref[nb * t0:nb * (t0 + pair), :] += jnp.dot(act, wd_ref[...], preferred_element_type=F32)


FFN_S_TF = 512


def _ffn_s(h, state, w_up, w_down, cw, cb, layer, w_layer, name):
    tf = FFN_S_TF
    nf = D_FF // tf
    m = h.shape[0]
    nst = FFN_CONV_WIDTH - 1
    return pl.pallas_call(
        _ffn_s_kernel,
        grid=(nf,),
        in_specs=[pl.BlockSpec((m, D_MODEL), lambda f: (0, 0)),
                  pl.BlockSpec((None, D_MODEL, tf), lambda f: (w_layer, 0, f)),
                  pl.BlockSpec((None, D_MODEL, tf), lambda f: (w_layer, 0, nf + f)),
                  pl.BlockSpec((None, tf, D_MODEL), lambda f: (w_layer, f, 0)),
                  pl.BlockSpec((None, FFN_CONV_WIDTH, tf), lambda f: (layer, 0, f)),
                  pl.BlockSpec((None, FFN_CONV_WIDTH, tf), lambda f: (layer, 0, nf + f)),
                  pl.BlockSpec((None, 1, tf), lambda f: (layer, 0, f)),
                  pl.BlockSpec((None, 1, tf), lambda f: (layer, 0, nf + f)),
                  pl.BlockSpec((nst, DEC_BATCH, tf), lambda f: (0, 0, f)),
                  pl.BlockSpec((nst, DEC_BATCH, tf), lambda f: (0, 0, nf + f))],
        out_specs=[pl.BlockSpec((m, D_MODEL), lambda f: (0, 0)),
                   pl.BlockSpec((nst, DEC_BATCH, tf), lambda f: (0, 0, f)),
                   pl.BlockSpec((nst, DEC_BATCH, tf), lambda f: (0, 0, f))],
        out_shape=[jax.ShapeDtypeStruct((m, D_MODEL), F32),
                   jax.ShapeDtypeStruct((nst, DEC_BATCH, D_FF), F32),
                   jax.ShapeDtypeStruct((nst, DEC_BATCH, D_FF), F32)],
        compiler_params=_params(1),
        name=name,
    )(h, w_up, w_up, w_down, cw, cw, cb, cb, state, state)


def _rope_tables(pos):
    half = ROT_DIM // 2
    inv = ROPE_THETA ** (-jnp.arange(half, dtype=F32) * 2.0 / ROT_DIM)
    ang = pos.astype(F32)[:, None] * inv[None, :]
    cos, sin = jnp.cos(ang), jnp.sin(ang)
    t = pos.shape[0]
    rest = A_HEAD_DIM - ROT_DIM
    ch = jnp.concatenate([cos, cos, jnp.ones((t, rest), F32)], axis=1)
    ah = jnp.concatenate([-sin, jnp.zeros((t, half + rest), F32)], axis=1)
    bh = jnp.concatenate([jnp.zeros((t, half), F32), sin, jnp.zeros((t, rest), F32)], axis=1)
    rep = LANES // A_HEAD_DIM
    return tuple(jnp.tile(x, (1, rep)) for x in (ch, ah, bh))


def _pad_rows(x, rows):
    return jnp.concatenate([x, jnp.zeros((rows - x.shape[0],) + x.shape[1:], x.dtype)], axis=0)


def kernel(x_prompt, x_sample, cache_a_k, cache_a_v, state_c_conv, state_d_pool, state_ffn_conv, c_prompt, c_sample, norm_g, ada_w, ada_b, w_in_e, w_out_e, a_sinks, b_ln_g, b_ln_b, b_ws, b_bias, w_in_o, w_out_o, c_conv_w, c_conv_b, c_ln_g, c_ln_b, d_w, d_b, d_scale, ffn_w_up, ffn_conv_w, ffn_conv_b, ffn_w_down):
    assert DEPTH == 2 and x_prompt.shape == (BATCH, SEQ, D_MODEL) and x_sample.shape == (DEC_BATCH, DEC_SEQ, D_MODEL)
    w_in_e_b, w_out_e_b = w_in_e.astype(BF16), w_out_e.astype(BF16)
    w_in_o_b, w_out_o_b = w_in_o.astype(BF16), w_out_o.astype(BF16)

    pad_rows = DEC_BATCH + SUBLANES
    c_all = _pad_rows(jnp.concatenate([c_sample, c_prompt], axis=0), pad_rows)
    mods = _ada(c_all, ada_w.reshape(2 * DEPTH, D_MODEL, 3 * D_MODEL), ada_b.reshape(2 * DEPTH, 1, 3 * D_MODEL))
    mods_p = mods[:, DEC_BATCH:DEC_BATCH + BATCH].reshape(2 * DEPTH, BATCH, 1, 3 * D_MODEL)
    gains = norm_g.reshape(4 * DEPTH, 1, D_MODEL)

    grp_p = _Group(BATCH * SEQ, 512, mods_p, SEQ)
    grp_s = _Group(DEC_BATCH * DEC_SEQ, DEC_BATCH * DEC_SEQ, mods, None)
    xp = x_prompt.reshape(BATCH * SEQ, D_MODEL)
    xs = jnp.transpose(x_sample, (1, 0, 2)).reshape(DEC_SEQ * DEC_BATCH, D_MODEL)

    tabs_p = _rope_tables(jnp.arange(SEQ))
    tabs_s = tuple(_pad_rows(x, SUBLANES) for x in _rope_tables(PAST_LEN + jnp.arange(DEC_SEQ)))

    row1 = lambda v: v.reshape(1, -1)
    bexp_p = jnp.repeat(jnp.transpose(b_bias[0]), B_HEAD_DIM, axis=1)
    ws4 = jnp.tril(b_ws[0])[:, :DEC_SEQ, :DEC_SEQ]
    wexp_s = jnp.repeat(jnp.transpose(ws4, (1, 2, 0)).reshape(DEC_SEQ * DEC_SEQ, B_HEADS), B_HEAD_DIM, axis=1)
    bexp_s = _pad_rows(bexp_p[:DEC_SEQ], SUBLANES)
    sinkrow = jnp.repeat(a_sinks[0], DEC_SEQ).reshape(A_HEADS * DEC_SEQ, 1)

    ffn_cb = ffn_conv_b.reshape(DEPTH, 1, 2 * D_FF)

    def run(grp, x, is_prompt):
        outs = {}
        tag = "p" if is_prompt else "s"
        tm_mm = 1024 if is_prompt else grp.m
        if is_prompt:
            mix, nk, nv = _even_p(x, w_in_e_b, 0, gains, 0, grp.mods, 0, a_sinks[0], tabs_p, b_ws[0], bexp_p,
                                  row1(b_ln_g[0]), row1(b_ln_b[0]))
            outs["ak"], outs["av"] = nk, nv
        else:
            (h,) = _resnorm(grp, x, None, gains, nxt=(0, 0))
            z = _mm(h, w_in_e_b, 0, tm_mm, 896, "in_even_" + tag)
            mix, nk, nv, vb = _even_s(z.reshape(DEC_SEQ, DEC_BATCH, EVEN_IN),
                                      cache_a_k[0].reshape(DEC_BATCH * WINDOW, A_KV_WIDTH),
                                      cache_a_v[0].reshape(DEC_BATCH * WINDOW, A_KV_WIDTH),
                                      sinkrow, tabs_s, wexp_s, bexp_s, row1(b_ln_g[0]), row1(b_ln_b[0]))
            outs["ak"], outs["av"], outs["bv"] = nk, nv, vb
        if is_prompt:
            x, h, wd0 = _mm_res(grp, mix, w_out_e_b, 0, x, gains, (0, 1), (1, 2), "out_even_p", cast=(ffn_w_down, 0))
            ffn_wb[0] = (ffn_w_up[0:1].astype(BF16), wd0)
        else:
            x, h = _mm_res(grp, mix.reshape(grp.m, D_MODEL), w_out_e_b, 0, x, gains, (0, 1), (1, 2), "out_even_s")
        x, h, outs["ff0"] = ffn(grp, x, h, 0, is_prompt, nxt=(2, 4))
        dwa, dba, dsa = d_w[0], row1(d_b[0]), row1(d_scale[0])
        cargs = (c_conv_w[0], row1(c_conv_b[0]), row1(c_ln_g[0]), row1(c_ln_b[0]), dwa, dba, dsa)
        if is_prompt:
            mix, ct, dt = _odd_p(h, w_in_o_b, 0, *cargs)
            outs["cc"] = ct[:, C_TAIL - (C_CONV_WIDTH - 1):]
            outs["dp"] = dt[:, D_TAIL - (POOL_MAX - 1):]
        else:
            z = _mm(h, w_in_o_b, 0, tm_mm, 1024, "in_odd_" + tag)
            mix, co, do = _odd_s(z.reshape(DEC_SEQ, DEC_BATCH, ODD_IN),
                                 jnp.transpose(state_c_conv[0], (1, 0, 2)),
                                 jnp.transpose(state_d_pool[0], (1, 0, 2)), *cargs)
            outs["cc"] = jnp.transpose(co, (1, 0, 2))
            outs["dp"] = jnp.transpose(do, (1, 0, 2))
        x, h = _mm_res(grp, mix.reshape(grp.m, D_MODEL), w_out_o_b, 0, x, gains, (2, 5), (3, 6), "out_odd_" + tag)
        x, _, outs["ff1"] = ffn(grp, x, h, 1, is_prompt, nxt=None)
        return x, outs

    ffn_wb = {}

    def ffn(grp, x, h, i, is_prompt, nxt):
        nst = FFN_CONV_WIDTH - 1
        wu, wd = ffn_wb[i]
        if is_prompt:
            cast_next = (ffn_w_up, ffn_w_down, i + 1) if i + 1 < DEPTH else None
            res = _ffn_p(h, BATCH, SEQ, wu, wd, ffn_conv_w, ffn_cb, i, 0, "ffn_p", cast_next)
            out, nfg, nfv = res[:3]
            if cast_next is not None:
                ffn_wb[i + 1] = (res[3], res[4])
            nf = jnp.concatenate([nfg[:, -1, SUBLANES - nst:], nfv[:, -1, SUBLANES - nst:]], axis=-1)
        else:
            st = jnp.transpose(state_ffn_conv[i], (1, 0, 2))
            out, nsg, nsv = _ffn_s(h, st, wu, wd, ffn_conv_w, ffn_cb, i, 0, "ffn_s")
            nf = jnp.transpose(jnp.concatenate([nsg, nsv], axis=-1), (1, 0, 2))
        res = _resnorm(grp, x, out, gains, res=(2 * i + 1, 4 * i + 3), nxt=nxt)
        if nxt is None:
            return res[0], None, nf
        return res[0], res[1], nf

    yp, op = run(grp_p, xp, True)
    ys, os_ = run(grp_s, xs, False)

    kv5 = lambda a, nb: a.reshape(1, nb, WINDOW, A_KV_HEADS, A_HEAD_DIM)
    y_prompt = yp.reshape(BATCH, SEQ, D_MODEL)
    y_sample = jnp.transpose(ys.reshape(DEC_SEQ, DEC_BATCH, D_MODEL), (1, 0, 2))
    return (y_prompt, y_sample,
            kv5(op["ak"], BATCH), kv5(os_["ak"], DEC_BATCH), kv5(op["av"], BATCH), kv5(os_["av"], DEC_BATCH),
            jnp.transpose(os_["bv"], (1, 0, 2))[None],
            op["cc"][None], os_["cc"][None], op["dp"][None], os_["dp"][None],
            jnp.stack([op["ff0"], op["ff1"]]), jnp.stack([os_["ff0"], os_["ff1"]]))
```

```python
import functools

import jax
import jax.numpy as jnp
from jax import lax
from jax.experimental import pallas as pl
from jax.experimental.pallas import tpu as pltpu

D_MODEL = 2048
BATCH = 2
SEQ = 4096
DEPTH = 2
DEC_BATCH = 128
DEC_SEQ = 4
PAST_LEN = 8192
A_HEADS = 16
A_KV_HEADS = 4
A_HEAD_DIM = 64
A_Q_WIDTH = A_HEADS * A_HEAD_DIM
A_KV_WIDTH = A_KV_HEADS * A_HEAD_DIM
WINDOW = 128
ROT_DIM = A_HEAD_DIM // 4
ROPE_THETA = 500000.0
B_HEADS = 8
B_HEAD_DIM = 128
B_WIDTH = B_HEADS * B_HEAD_DIM
CHUNK = 128
C_WIDTH = 1024
C_CONV_WIDTH = 31
POOL_SIZES = (2, 4, 8, 16)
D_GROUPS = len(POOL_SIZES)
POOL_MAX = max(POOL_SIZES)
D_WIDTH = 1024
D_GROUP_WIDTH = D_WIDTH // D_GROUPS
D_FF = 5632
FFN_CONV_WIDTH = 3
EVEN_IN = A_Q_WIDTH + 2 * A_KV_WIDTH + 2 * B_WIDTH
ODD_IN = 2 * C_WIDTH + D_WIDTH
EPS = 1e-6
NEG_INF = -1e30

LANES = 128
SUBLANES = 8
VMEM_LIMIT = 56 * 1024 * 1024
EVEN_P_VMEM_LIMIT = 60 * 1024 * 1024
FFN_RES_VMEM_LIMIT = 60 * 1024 * 1024

BF16 = jnp.bfloat16
F32 = jnp.float32

SEQ_BLOCK = 32
C_TAIL = 32
D_TAIL = 16


def _params(n_axes, vmem_limit=VMEM_LIMIT):
    return pltpu.CompilerParams(dimension_semantics=("arbitrary",) * n_axes,
                                vmem_limit_bytes=vmem_limit)


_GELU_C = 0.7978845608028654


def _gelu_tanh(x):
    hx = 0.5 * x
    return hx + hx * jnp.tanh(x * (_GELU_C + (_GELU_C * 0.044715) * (x * x)))


def _rms(x, g):
    return x * lax.rsqrt(jnp.mean(x * x, axis=-1, keepdims=True) + EPS) * g


def _layer_norm(x, g, b):
    mu = jnp.mean(x, axis=-1, keepdims=True)
    xc = x - mu
    var = jnp.mean(xc * xc, axis=-1, keepdims=True)
    return xc * lax.rsqrt(var + EPS) * g + b


def _rows(m, tm):
    r = m.shape[0]
    if r == 1 or r == tm:
        return m
    return jnp.concatenate([m] * (tm // r), axis=0)


def _rope(x, c, a, b):
    return x * c + pltpu.roll(x, LANES - ROT_DIM // 2, 1) * a + pltpu.roll(x, ROT_DIM // 2, 1) * b


def _ada_kernel(c_ref, w_ref, b_ref, o_ref):
    c = c_ref[...]
    s = (c * jax.nn.sigmoid(c)).astype(BF16)
    o_ref[...] = jnp.dot(s, w_ref[...].astype(BF16), preferred_element_type=F32) + b_ref[...]


def _ada(c_all, w, b):
    r = c_all.shape[0]
    tn = 1536
    n_sub, _, n = w.shape
    return pl.pallas_call(
        _ada_kernel,
        grid=(n_sub, n // tn),
        in_specs=[pl.BlockSpec((r, D_MODEL), lambda k, j: (0, 0)),
                  pl.BlockSpec((None, D_MODEL, tn), lambda k, j: (k, 0, j)),
                  pl.BlockSpec((None, 1, tn), lambda k, j: (k, 0, j))],
        out_specs=pl.BlockSpec((None, r, tn), lambda k, j: (k, 0, j)),
        out_shape=jax.ShapeDtypeStruct((n_sub, r, n), F32),
        compiler_params=_params(2),
        name="ada",
    )(c_all, w, b)


def _resnorm_kernel(*refs, has_res, has_next):
    it = iter(refs)
    x_ref = next(it)
    if has_res:
        o_ref, gate_ref, gpost_ref = next(it), next(it), next(it)
    if has_next:
        gpre_ref, scale_ref, shift_ref = next(it), next(it), next(it)
    if has_res:
        xo_ref = next(it)
    if has_next:
        ho_ref = next(it)
    x = x_ref[...]
    tm = x.shape[0]
    if has_res:
        x = x + _rows(gate_ref[...], tm) * _rms(o_ref[...], gpost_ref[...])
        xo_ref[...] = x
    if has_next:
        h = _rms(x, gpre_ref[...]) * (1.0 + _rows(scale_ref[...], tm)) + _rows(shift_ref[...], tm)
        ho_ref[...] = h.astype(BF16)


class _Group:
    def __init__(self, m, tm, mods, per_batch_rows):
        self.m, self.tm, self.mods = m, tm, mods
        self.per_batch_rows = per_batch_rows

    def mod_spec(self, k, col):
        if self.per_batch_rows is None:
            return pl.BlockSpec((None, DEC_BATCH, D_MODEL), lambda i: (k, 0, col))
        tpb = self.per_batch_rows // self.tm
        return pl.BlockSpec((None, None, 1, D_MODEL), lambda i: (k, i // tpb, 0, col))


def _resnorm(grp, x, out, gains, res=None, nxt=None):
    tm = grp.tm
    row = pl.BlockSpec((tm, D_MODEL), lambda i: (i, 0))
    args, specs, out_shapes, out_specs = [x], [row], [], []

    def gain_spec(idx):
        return pl.BlockSpec((None, 1, D_MODEL), lambda i: (idx, 0, 0))

    if res is not None:
        args += [out, grp.mods, gains]
        specs += [row, grp.mod_spec(res[0], 2), gain_spec(res[1])]
        out_shapes.append(jax.ShapeDtypeStruct((grp.m, D_MODEL), F32))
        out_specs.append(row)
    if nxt is not None:
        args += [gains, grp.mods, grp.mods]
        specs += [gain_spec(nxt[1]), grp.mod_spec(nxt[0], 1), grp.mod_spec(nxt[0], 0)]
        out_shapes.append(jax.ShapeDtypeStruct((grp.m, D_MODEL), BF16))
        out_specs.append(row)
    res_out = pl.pallas_call(
        functools.partial(_resnorm_kernel, has_res=res is not None, has_next=nxt is not None),
        grid=(grp.m // tm,),
        in_specs=specs, out_specs=out_specs, out_shape=out_shapes,
        compiler_params=_params(1),
        name="resnorm",
    )(*args)
    return res_out


def _mm_kernel(a_ref, w_ref, o_ref):
    o_ref[...] = jnp.dot(a_ref[...], w_ref[...], preferred_element_type=F32).astype(o_ref.dtype)


def _mm(a, w, layer, tm, tn, name, out_dtype=F32):
    m, k = a.shape
    n = w.shape[2]
    return pl.pallas_call(
        _mm_kernel,
        grid=(n // tn, m // tm),
        in_specs=[pl.BlockSpec((tm, k), lambda j, i: (i, 0)),
                  pl.BlockSpec((None, k, tn), lambda j, i: (layer, 0, j))],
        out_specs=pl.BlockSpec((tm, tn), lambda j, i: (i, j)),
        out_shape=jax.ShapeDtypeStruct((m, n), out_dtype),
        compiler_params=_params(2),
        name=name,
    )(a, w)


MM_RES_SPLIT = 4


def _mm_res_kernel(*refs, cast):
    a_ref, w_ref, x_ref, gate_ref, gpost_ref, gpre_ref, scale_ref, shift_ref = refs[:8]
    if cast:
        cin_ref, xo_ref, ho_ref, cout_ref = refs[8:]
        cout_ref[...] = cin_ref[...].astype(BF16)
    else:
        xo_ref, ho_ref = refs[8:]
    tm = a_ref.shape[0]
    rs = max(tm // MM_RES_SPLIT, gate_ref.shape[0])
    outs = [jnp.dot(a_ref[r:r + rs, :], w_ref[...], preferred_element_type=F32) for r in range(0, tm, rs)]
    for r, out in zip(range(0, tm, rs), outs):
        x = x_ref[r:r + rs, :] + _rows(gate_ref[...], rs) * _rms(out, gpost_ref[...])
        xo_ref[r:r + rs, :] = x
        h = _rms(x, gpre_ref[...]) * (1.0 + _rows(scale_ref[...], rs)) + _rows(shift_ref[...], rs)
        ho_ref[r:r + rs, :] = h.astype(BF16)


def _mm_res(grp, a, w, layer, x, gains, res, nxt, name, cast=None):
    tm = grp.tm
    k = a.shape[1]
    n = grp.m // tm
    row = pl.BlockSpec((tm, D_MODEL), lambda i: (i, 0))
    gain_spec = lambda idx: pl.BlockSpec((None, 1, D_MODEL), lambda i: (idx, 0, 0))
    in_specs = [pl.BlockSpec((tm, k), lambda i: (i, 0)),
                pl.BlockSpec((None, k, D_MODEL), lambda i: (layer, 0, 0)),
                row, grp.mod_spec(res[0], 2), gain_spec(res[1]),
                gain_spec(nxt[1]), grp.mod_spec(nxt[0], 1), grp.mod_spec(nxt[0], 0)]
    out_specs = [row, row]
    out_shape = [jax.ShapeDtypeStruct((grp.m, D_MODEL), F32), jax.ShapeDtypeStruct((grp.m, D_MODEL), BF16)]
    args = [a, w, x, grp.mods, gains, gains, grp.mods, grp.mods]
    if cast is not None:
        cw, cl = cast
        cr = cw.shape[1] // n
        in_specs.append(pl.BlockSpec((None, cr, cw.shape[2]), lambda i: (cl, i, 0)))
        out_specs.append(pl.BlockSpec((None, cr, cw.shape[2]), lambda i: (0, i, 0)))
        out_shape.append(jax.ShapeDtypeStruct((1,) + cw.shape[1:], BF16))
        args.append(cw)
    return pl.pallas_call(
        functools.partial(_mm_res_kernel, cast=cast is not None),
        grid=(n,),
        in_specs=in_specs, out_specs=out_specs, out_shape=out_shape,
        compiler_params=_params(1),
        name=name,
    )(*args)


CORE_ROWS = 512
CORE_HALF = CORE_ROWS // 2


PROJ_CHUNK = 512


def _proj_chunks(dst, lhs_ref, r0, w_ref):
    def make(c):
        def run():
            cols = slice(c, c + PROJ_CHUNK)
            dst[:, cols] = jnp.dot(lhs_ref[r0:r0 + CORE_HALF, :], w_ref[:, cols], preferred_element_type=F32)
        return run
    return iter([make(c) for c in range(0, w_ref.shape[1], PROJ_CHUNK)])


def _issue(pre, n=1):
    for _ in range(n):
        thunk = next(pre, None)
        if thunk is not None:
            thunk()


def _even_block(pre, z, r0, o0, first_lim, sink_ref, cos_ref, sa_ref, sb_ref, ws_ref, bexp_ref, lng_ref, lnb_ref,
                mix_ref, nk_ref, nv_ref, kprev, vprev):
    qb = WINDOW
    zr = slice(r0, r0 + qb)
    orow = slice(o0, o0 + qb)
    cos_t, sa_t, sb_t = cos_ref[orow, :], sa_ref[orow, :], sb_ref[orow, :]
    lane = lax.broadcasted_iota(jnp.int32, (qb, LANES), 1)
    lo = lane < A_HEAD_DIM
    lane2 = lax.broadcasted_iota(jnp.int32, (2 * qb, LANES), 1)
    lo2 = lane2 < A_HEAD_DIM
    row = lax.broadcasted_iota(jnp.int32, (qb, 2 * qb), 0)
    col = lax.broadcasted_iota(jnp.int32, (qb, 2 * qb), 1)
    mask = ((col > row) & (col < qb)) | ((col >= qb) & (col - qb <= row))
    if first_lim is not None:
        mask = mask & (col >= first_lim)
    nt = (((1,), (1,)), ((), ()))

    for jk in range(A_KV_HEADS // 2):
        _issue(pre)
        ksl = slice(A_Q_WIDTH + LANES * jk, A_Q_WIDTH + LANES * (jk + 1))
        vsl = slice(A_Q_WIDTH + A_KV_WIDTH + LANES * jk, A_Q_WIDTH + A_KV_WIDTH + LANES * (jk + 1))
        csl = slice(LANES * jk, LANES * (jk + 1))
        kc = _rope(z[zr, ksl], cos_t, sa_t, sb_t)
        vc = z[zr, vsl]
        kall = jnp.concatenate([kprev[:, csl], kc], axis=0)
        vall = jnp.concatenate([vprev[:, csl], vc], axis=0)
        krol = pltpu.roll(kall, A_HEAD_DIM, 1)
        vrol = pltpu.roll(vall, A_HEAD_DIM, 1)
        for sub in range(2):
            kvh = 2 * jk + sub
            kd = (jnp.where(lo2, kall, krol) if sub == 0 else jnp.where(lo2, krol, kall)).astype(BF16)
            vd = (jnp.where(lo2, vall, vrol) if sub == 0 else jnp.where(lo2, vrol, vall)).astype(BF16)
            for qs in range(2):
                js = 2 * kvh + qs
                qsl = _rope(z[zr, LANES * js:LANES * (js + 1)], cos_t, sa_t, sb_t) * (A_HEAD_DIM ** -0.5)
                outs = []
                for half in range(2):
                    h = 2 * js + half
                    qm = jnp.where(lo if half == 0 else jnp.logical_not(lo), qsl, 0.0).astype(BF16)
                    s = lax.dot_general(qm, kd, nt, preferred_element_type=F32)
                    s = jnp.where(mask, s, NEG_INF)
                    sink = sink_ref[h]
                    m = jnp.maximum(jnp.max(s, axis=-1, keepdims=True), sink)
                    p = jnp.exp(s - m)
                    den = jnp.sum(p, axis=-1, keepdims=True) + jnp.exp(sink - m)
                    o = jnp.dot(p.astype(BF16), vd, preferred_element_type=F32)
                    outs.append(o * (1.0 / den))
                mix_ref[orow, LANES * js:LANES * (js + 1)] = jnp.where(lo, outs[0], outs[1]).astype(BF16)
        kprev[:, csl] = kc
        vprev[:, csl] = vc
        nk_ref[:, csl] = kc
        nv_ref[:, csl] = vc

    _issue(pre, 2)
    zb0 = A_Q_WIDTH + 2 * A_KV_WIDTH
    u = _gelu_tanh(z[zr, zb0:zb0 + B_WIDTH])
    vb = _layer_norm(_gelu_tanh(z[zr, zb0 + B_WIDTH:zb0 + 2 * B_WIDTH]), lng_ref[...], lnb_ref[...])
    ri = lax.broadcasted_iota(jnp.int32, (CHUNK, CHUNK), 0)
    ci = lax.broadcasted_iota(jnp.int32, (CHUNK, CHUNK), 1)
    tri = ri >= ci
    for h in range(B_HEADS):
        hs = slice(B_HEAD_DIM * h, B_HEAD_DIM * (h + 1))
        w = jnp.where(tri, ws_ref[h], 0.0).astype(BF16)
        mixed = jnp.dot(w, vb[:, hs].astype(BF16), preferred_element_type=F32) + bexp_ref[:, hs]
        mix_ref[orow, A_Q_WIDTH + B_HEAD_DIM * h:A_Q_WIDTH + B_HEAD_DIM * (h + 1)] = (u[:, hs] * mixed).astype(BF16)


def _even_p_kernel(sink_ref, xc_ref, xn_ref, w_hbm, gpre_ref, scc_ref, shc_ref, scn_ref, shn_ref,
                   cos_ref, sa_ref, sb_ref, ws_ref, bexp_ref, lng_ref, lnb_ref,
                   mix_ref, nk_ref, nv_ref, w_ref, za, zb, hb, hn, kprev, vprev, *, layer):
    step = pl.program_id(0)
    b = step % (SEQ // CORE_ROWS)
    args = (sink_ref, cos_ref, sa_ref, sb_ref, ws_ref, bexp_ref, lng_ref, lnb_ref, mix_ref, nk_ref, nv_ref,
            kprev, vprev)

    def prenorm(x, sc_ref, sh_ref):
        return (_rms(x, gpre_ref[...]) * (1.0 + sc_ref[...]) + sh_ref[...]).astype(BF16)

    @pl.when(step == 0)
    def _():
        pltpu.sync_copy(w_hbm.at[layer], w_ref)
        hb[...] = prenorm(xc_ref[0:CORE_HALF, :], scc_ref, shc_ref)
        za[...] = jnp.dot(hb[...], w_ref[...], preferred_element_type=F32)

    @pl.when(b == 0)
    def _():
        kprev[...] = jnp.zeros_like(kprev)
        vprev[...] = jnp.zeros_like(vprev)

    hb[...] = prenorm(xc_ref[CORE_HALF:CORE_ROWS, :], scc_ref, shc_ref)
    hn[...] = prenorm(xn_ref[...], scn_ref, shn_ref)
    pre = _proj_chunks(zb, hb, 0, w_ref)
    _even_block(pre, za, 0, 0, jnp.where(b > 0, 0, WINDOW), *args)
    _even_block(pre, za, WINDOW, WINDOW, None, *args)
    _issue(pre, EVEN_IN // PROJ_CHUNK)
    pre = _proj_chunks(za, hn, 0, w_ref)
    _even_block(pre, zb, 0, CORE_HALF, None, *args)
    _even_block(pre, zb, WINDOW, CORE_HALF + WINDOW, None, *args)
    _issue(pre, EVEN_IN // PROJ_CHUNK)


def _even_p(x, w, layer, gains, gain_idx, mods, k_mod, sinks, tabs, ws, bexp, lng, lnb):
    h = x
    rows = CORE_ROWS
    steps = h.shape[0] // rows
    per_seq = SEQ // rows
    last_half = h.shape[0] // CORE_HALF - 1
    nxt_half = lambda s: jnp.minimum(2 * s + 2, last_half)
    mod_c = lambda col: pl.BlockSpec((None, None, 1, D_MODEL), lambda s: (k_mod, s // per_seq, 0, col))
    mod_n = lambda col: pl.BlockSpec((None, None, 1, D_MODEL),
                                     lambda s: (k_mod, nxt_half(s) // (2 * per_seq), 0, col))
    full = lambda shape: pl.BlockSpec(shape, lambda s: (0,) * len(shape))
    tab = pl.BlockSpec((rows, LANES), lambda s: (s % per_seq, 0))
    kv = pl.BlockSpec((None, WINDOW, A_KV_WIDTH), lambda s: (s // per_seq, 0, 0))
    return pl.pallas_call(
        functools.partial(_even_p_kernel, layer=layer),
        grid=(steps,),
        in_specs=[pl.BlockSpec(memory_space=pltpu.SMEM),
                  pl.BlockSpec((rows, D_MODEL), lambda s: (s, 0)),
                  pl.BlockSpec((CORE_HALF, D_MODEL), lambda s: (nxt_half(s), 0)),
                  pl.BlockSpec(memory_space=pl.ANY),
                  pl.BlockSpec((None, 1, D_MODEL), lambda s: (gain_idx, 0, 0)),
                  mod_c(1), mod_c(0), mod_n(1), mod_n(0),
                  tab, tab, tab,
                  full((B_HEADS, CHUNK, CHUNK)),
                  pl.BlockSpec((CHUNK, B_WIDTH), lambda s: (0, 0)),
                  full((1, B_WIDTH)), full((1, B_WIDTH))],
        out_specs=[pl.BlockSpec((rows, D_MODEL), lambda s: (s, 0)), kv, kv],
        out_shape=[jax.ShapeDtypeStruct((h.shape[0], D_MODEL), BF16),
                   jax.ShapeDtypeStruct((BATCH, WINDOW, A_KV_WIDTH), F32),
                   jax.ShapeDtypeStruct((BATCH, WINDOW, A_KV_WIDTH), F32)],
        scratch_shapes=[pltpu.VMEM((D_MODEL, EVEN_IN), BF16),
                        pltpu.VMEM((CORE_HALF, EVEN_IN), F32), pltpu.VMEM((CORE_HALF, EVEN_IN), F32),
                        pltpu.VMEM((CORE_HALF, D_MODEL), BF16), pltpu.VMEM((CORE_HALF, D_MODEL), BF16),
                        pltpu.VMEM((WINDOW, A_KV_WIDTH), F32), pltpu.VMEM((WINDOW, A_KV_WIDTH), F32)],
        compiler_params=_params(1, EVEN_P_VMEM_LIMIT),
        name="even_p",
    )(sinks, x, x, w, gains, mods, mods, mods, mods, *tabs, ws, bexp, lng, lnb)


def _even_s_kernel(sinkrow_ref, z_ref, ck_ref, cv_ref, cos_ref, sa_ref, sb_ref, wexp_ref, bexp_ref,
                   lng_ref, lnb_ref, mix_ref, ok_ref, ov_ref, vb_ref, qm, osc, kn, vn):
    s_blk = SEQ_BLOCK
    rows_per_seq = A_HEADS * DEC_SEQ
    lane256 = lax.broadcasted_iota(jnp.int32, (s_blk, A_KV_WIDTH), 1)
    kn[...] = jnp.zeros_like(kn)
    vn[...] = jnp.zeros_like(vn)
    halves = A_KV_WIDTH // LANES

    def put(ref, start, stride, val):
        for c in range(halves):
            ref[c, pl.ds(start, s_blk, stride=stride), :] = val[:, LANES * c:LANES * (c + 1)]

    def get_rows(ref, start, size):
        return jnp.concatenate([ref[c, pl.ds(start, size), :] for c in range(halves)], axis=1)

    def get_strided(ref, start, stride):
        return jnp.concatenate([ref[c, pl.ds(start, s_blk, stride=stride), :] for c in range(halves)], axis=1)

    for t in range(DEC_SEQ):
        c, a, bb = cos_ref[t:t + 1, :], sa_ref[t:t + 1, :], sb_ref[t:t + 1, :]

        def rope2(lo_lane):
            return jnp.concatenate([_rope(z_ref[t, :, lo_lane:lo_lane + LANES], c, a, bb),
                                    _rope(z_ref[t, :, lo_lane + LANES:lo_lane + 2 * LANES], c, a, bb)], axis=1)

        put(kn, t, SUBLANES, rope2(A_Q_WIDTH))
        put(vn, t, SUBLANES, z_ref[t, :, A_Q_WIDTH + A_KV_WIDTH:A_Q_WIDTH + 2 * A_KV_WIDTH])
        for j in range(A_KV_HEADS):
            qs = rope2(A_KV_WIDTH * j) * (A_HEAD_DIM ** -0.5)
            keep = (lane256 >= A_HEAD_DIM * j) & (lane256 < A_HEAD_DIM * (j + 1))
            for g in range(A_HEADS // A_KV_HEADS):
                sh = ((j - g) * A_HEAD_DIM) % A_KV_WIDTH
                qr = qs if sh == 0 else pltpu.roll(qs, sh, 1)
                r = (4 * j + g) * DEC_SEQ + t
                put(qm, r, rows_per_seq, jnp.where(keep, qr, 0.0))

    rr = lax.broadcasted_iota(jnp.int32, (rows_per_seq, WINDOW), 0) % DEC_SEQ
    cc = lax.broadcasted_iota(jnp.int32, (rows_per_seq, WINDOW), 1)
    mask_old = cc > rr
    new0 = WINDOW - DEC_SEQ
    mask_new = (cc >= new0) & (cc - new0 <= rr)
    row8 = lax.broadcasted_iota(jnp.int32, (SUBLANES, A_KV_WIDTH), 0)
    sink = sinkrow_ref[...]
    nt = (((1,), (1,)), ((), ()))

    def shifted(c_ref, new_ref, o_ref, n):
        base = pl.multiple_of(n * WINDOW, WINDOW)
        old = c_ref[pl.ds(base, WINDOW), :]
        rolled = pltpu.roll(old, WINDOW - DEC_SEQ, 0)
        fresh = get_rows(new_ref, pl.multiple_of(n * SUBLANES, SUBLANES), SUBLANES)
        tail = jnp.where(row8 >= SUBLANES - DEC_SEQ, pltpu.roll(fresh, SUBLANES - DEC_SEQ, 0),
                         rolled[WINDOW - SUBLANES:, :])
        new = jnp.concatenate([rolled[:WINDOW - SUBLANES, :], tail], axis=0)
        o_ref[pl.ds(base, WINDOW), :] = new
        return old.astype(BF16), new.astype(BF16)

    def body(n, carry):
        k_old, k_new = shifted(ck_ref, kn, ok_ref, n)
        v_old, v_new = shifted(cv_ref, vn, ov_ref, n)
        q = get_rows(qm, pl.multiple_of(n * rows_per_seq, rows_per_seq), rows_per_seq).astype(BF16)
        s_old = jnp.where(mask_old, lax.dot_general(q, k_old, nt, preferred_element_type=F32), NEG_INF)
        s_new = jnp.where(mask_new, lax.dot_general(q, k_new, nt, preferred_element_type=F32), NEG_INF)
        m = jnp.maximum(jnp.maximum(jnp.max(s_old, axis=-1, keepdims=True),
                                    jnp.max(s_new, axis=-1, keepdims=True)), sink)
        p_old = jnp.exp(s_old - m)
        p_new = jnp.exp(s_new - m)
        den = (jnp.sum(p_old, axis=-1, keepdims=True) + jnp.sum(p_new, axis=-1, keepdims=True)
               + jnp.exp(sink - m))
        o = (jnp.dot(p_old.astype(BF16), v_old, preferred_element_type=F32)
             + jnp.dot(p_new.astype(BF16), v_new, preferred_element_type=F32))
        o = o * (1.0 / den)
        for c in range(halves):
            osc[c, pl.ds(pl.multiple_of(n * rows_per_seq, rows_per_seq), rows_per_seq), :] = o[:, LANES * c:LANES * (c + 1)]
        return carry

    lax.fori_loop(0, s_blk, body, 0, unroll=8)

    for t in range(DEC_SEQ):
        for j in range(A_KV_HEADS):
            acc = jnp.zeros((s_blk, A_KV_WIDTH), F32)
            for g in range(A_HEADS // A_KV_HEADS):
                r = (4 * j + g) * DEC_SEQ + t
                ov = get_strided(osc, r, rows_per_seq)
                sh = ((g - j) * A_HEAD_DIM) % A_KV_WIDTH
                orr = ov if sh == 0 else pltpu.roll(ov, sh, 1)
                acc = jnp.where((lane256 >= A_HEAD_DIM * g) & (lane256 < A_HEAD_DIM * (g + 1)), orr, acc)
            mix_ref[t, :, A_KV_WIDTH * j:A_KV_WIDTH * (j + 1)] = acc.astype(BF16)

    zb0 = A_Q_WIDTH + 2 * A_KV_WIDTH
    us, vbs = [], []
    for t in range(DEC_SEQ):
        us.append(_gelu_tanh(z_ref[t, :, zb0:zb0 + B_WIDTH]))
        v = _layer_norm(_gelu_tanh(z_ref[t, :, zb0 + B_WIDTH:zb0 + 2 * B_WIDTH]), lng_ref[...], lnb_ref[...])
        vb_ref[t] = v
        vbs.append(v)
    for t in range(DEC_SEQ):
        mixed = bexp_ref[t:t + 1, :]
        for s in range(t + 1):
            mixed = mixed + wexp_ref[DEC_SEQ * t + s:DEC_SEQ * t + s + 1, :] * vbs[s]
        mix_ref[t, :, A_Q_WIDTH:A_Q_WIDTH + B_WIDTH] = (us[t] * mixed).astype(BF16)


def _even_s(z3, ck, cv, sinkrow, tabs, wexp, bexp, lng, lnb):
    s_blk = SEQ_BLOCK
    rows_per_seq = A_HEADS * DEC_SEQ
    full = lambda shape: pl.BlockSpec(shape, lambda s: (0,) * len(shape))
    cache = pl.BlockSpec((s_blk * WINDOW, A_KV_WIDTH), lambda s: (s, 0))
    return pl.pallas_call(
        _even_s_kernel,
        grid=(DEC_BATCH // s_blk,),
        in_specs=[full((rows_per_seq, 1)),
                  pl.BlockSpec((DEC_SEQ, s_blk, EVEN_IN), lambda s: (0, s, 0)),
                  cache, cache,
                  full((SUBLANES, LANES)), full((SUBLANES, LANES)), full((SUBLANES, LANES)),
                  full((DEC_SEQ * DEC_SEQ, B_WIDTH)), full((SUBLANES, B_WIDTH)),
                  full((1, B_WIDTH)), full((1, B_WIDTH))],
        out_specs=[pl.BlockSpec((DEC_SEQ, s_blk, D_MODEL), lambda s: (0, s, 0)),
                   cache, cache,
                   pl.BlockSpec((DEC_SEQ, s_blk, B_WIDTH), lambda s: (0, s, 0))],
        out_shape=[jax.ShapeDtypeStruct((DEC_SEQ, DEC_BATCH, D_MODEL), BF16),
                   jax.ShapeDtypeStruct((DEC_BATCH * WINDOW, A_KV_WIDTH), F32),
                   jax.ShapeDtypeStruct((DEC_BATCH * WINDOW, A_KV_WIDTH), F32),
                   jax.ShapeDtypeStruct((DEC_SEQ, DEC_BATCH, B_WIDTH), F32)],
        scratch_shapes=[pltpu.VMEM((A_KV_WIDTH // LANES, s_blk * rows_per_seq, LANES), F32),
                        pltpu.VMEM((A_KV_WIDTH // LANES, s_blk * rows_per_seq, LANES), F32),
                        pltpu.VMEM((A_KV_WIDTH // LANES, s_blk * SUBLANES, LANES), F32),
                        pltpu.VMEM((A_KV_WIDTH // LANES, s_blk * SUBLANES, LANES), F32)],
        compiler_params=_params(1),
        name="even_s",
    )(sinkrow, z3, ck, cv, *tabs, wexp, bexp, lng, lnb)


ODD_ROWS = CORE_HALF
ODD_RC = 64


def _odd_block(pre, z_ref, o0, pos0, cw_ref, cb_ref, lng_ref, lnb_ref, dw_ref, db_ref, dsc_ref,
               mix_ref, ctail_ref, dtail_ref, gext, dext, cbuf, dbuf, shbuf):
    tr = ODD_ROWS
    _issue(pre, 2)
    gext[C_TAIL:C_TAIL + tr, :] = z_ref[:, 0:C_WIDTH] * jax.nn.sigmoid(z_ref[:, C_WIDTH:2 * C_WIDTH])
    dext[D_TAIL:D_TAIL + tr, :] = z_ref[:, 2 * C_WIDTH:2 * C_WIDTH + D_WIDTH]

    gw = D_GROUP_WIDTH
    lead = C_TAIL - (C_CONV_WIDTH - 1)
    sh_rows = C_TAIL + tr - SUBLANES
    for s in range(1, SUBLANES):
        shbuf[s, 0:sh_rows, :] = gext[s:s + sh_rows, :]

    def tap(off, r0, ls):
        a, s = divmod(off, SUBLANES)
        src = gext if s == 0 else shbuf.at[s]
        return src[r0 + SUBLANES * a:r0 + SUBLANES * a + ODD_RC, ls]

    for rc in range(tr // ODD_RC):
        r0 = rc * ODD_RC
        _issue(pre)
        for c in range(C_WIDTH // gw):
            ls = slice(gw * c, gw * (c + 1))
            acc = jnp.broadcast_to(cb_ref[:, ls], (ODD_RC, gw))
            for k in range(C_CONV_WIDTH):
                acc = acc + cw_ref[k:k + 1, ls] * tap(lead + k, r0, ls)
            cbuf[r0:r0 + ODD_RC, ls] = acc
        y = _layer_norm(cbuf[r0:r0 + ODD_RC, :], lng_ref[...], lnb_ref[...])
        mix_ref[o0 + r0:o0 + r0 + ODD_RC, 0:C_WIDTH] = (y * jax.nn.sigmoid(y)).astype(BF16)

        pos1 = (pos0 + r0 + 1 + lax.broadcasted_iota(jnp.int32, (ODD_RC, gw), 0))
        for g, wsz in enumerate(POOL_SIZES):
            ls = slice(gw * g, gw * (g + 1))
            win = dext[D_TAIL + r0:D_TAIL + r0 + ODD_RC, ls]
            for jj in range(1, wsz):
                win = win + dext[D_TAIL + r0 - jj:D_TAIL + r0 - jj + ODD_RC, ls]
            cnt = jnp.minimum(pos1, wsz).astype(F32)
            diff = win / cnt - dext[D_TAIL + r0:D_TAIL + r0 + ODD_RC, ls]
            dbuf[r0:r0 + ODD_RC, ls] = diff.astype(BF16)

    for g in range(D_GROUPS):
        ls = slice(gw * g, gw * (g + 1))
        o = jnp.dot(dbuf[:, ls], dw_ref[g].astype(BF16), preferred_element_type=F32) + db_ref[:, ls]
        mix_ref[o0:o0 + tr, C_WIDTH + gw * g:C_WIDTH + gw * (g + 1)] = (o * dsc_ref[:, ls]).astype(BF16)

    gext[0:C_TAIL, :] = gext[tr:tr + C_TAIL, :]
    dext[0:D_TAIL, :] = dext[tr:tr + D_TAIL, :]
    ctail_ref[...] = gext[0:C_TAIL, :]
    dtail_ref[...] = dext[0:D_TAIL, :]


def _odd_p_kernel(hc_ref, hn_ref, w_ref, cw_ref, cb_ref, lng_ref, lnb_ref, dw_ref, db_ref, dsc_ref,
                  mix_ref, ctail_ref, dtail_ref, za, zb, gext, dext, cbuf, dbuf, shbuf):
    step = pl.program_id(0)
    b = step % (SEQ // CORE_ROWS)
    args = (cw_ref, cb_ref, lng_ref, lnb_ref, dw_ref, db_ref, dsc_ref, mix_ref, ctail_ref, dtail_ref,
            gext, dext, cbuf, dbuf, shbuf)

    @pl.when(step == 0)
    def _():
        za[...] = jnp.dot(hc_ref[0:CORE_HALF, :], w_ref[...], preferred_element_type=F32)

    @pl.when(b == 0)
    def _():
        gext[0:C_TAIL, :] = jnp.zeros((C_TAIL, C_WIDTH), F32)
        dext[0:D_TAIL, :] = jnp.zeros((D_TAIL, D_WIDTH), F32)

    pre = _proj_chunks(zb, hc_ref, CORE_HALF, w_ref)
    _odd_block(pre, za, 0, b * CORE_ROWS, *args)
    _issue(pre, ODD_IN // PROJ_CHUNK)
    pre = _proj_chunks(za, hn_ref, 0, w_ref)
    _odd_block(pre, zb, CORE_HALF, b * CORE_ROWS + CORE_HALF, *args)
    _issue(pre, ODD_IN // PROJ_CHUNK)


def _odd_p(h, w, layer, cw, cb, lng, lnb, dw, db, dsc):
    rows = CORE_ROWS
    steps = h.shape[0] // rows
    per_seq = SEQ // rows
    last_half = h.shape[0] // CORE_HALF - 1
    tr = ODD_ROWS
    full = lambda shape: pl.BlockSpec(shape, lambda s: (0,) * len(shape))
    return pl.pallas_call(
        _odd_p_kernel,
        grid=(steps,),
        in_specs=[pl.BlockSpec((rows, D_MODEL), lambda s: (s, 0)),
                  pl.BlockSpec((CORE_HALF, D_MODEL), lambda s: (jnp.minimum(2 * s + 2, last_half), 0)),
                  pl.BlockSpec((None, D_MODEL, ODD_IN), lambda s: (layer, 0, 0)),
                  full((C_CONV_WIDTH, C_WIDTH)), full((1, C_WIDTH)), full((1, C_WIDTH)), full((1, C_WIDTH)),
                  full((D_GROUPS, D_GROUP_WIDTH, D_GROUP_WIDTH)), full((1, D_WIDTH)), full((1, D_WIDTH))],
        out_specs=[pl.BlockSpec((rows, D_MODEL), lambda s: (s, 0)),
                   pl.BlockSpec((None, C_TAIL, C_WIDTH), lambda s: (s // per_seq, 0, 0)),
                   pl.BlockSpec((None, D_TAIL, D_WIDTH), lambda s: (s // per_seq, 0, 0))],
        out_shape=[jax.ShapeDtypeStruct((h.shape[0], D_MODEL), BF16),
                   jax.ShapeDtypeStruct((BATCH, C_TAIL, C_WIDTH), F32),
                   jax.ShapeDtypeStruct((BATCH, D_TAIL, D_WIDTH), F32)],
        scratch_shapes=[pltpu.VMEM((CORE_HALF, ODD_IN), F32), pltpu.VMEM((CORE_HALF, ODD_IN), F32),
                        pltpu.VMEM((C_TAIL + tr, C_WIDTH), F32), pltpu.VMEM((D_TAIL + tr, D_WIDTH), F32),
                        pltpu.VMEM((tr, C_WIDTH), F32), pltpu.VMEM((tr, D_WIDTH), BF16),
                        pltpu.VMEM((SUBLANES, C_TAIL + tr, C_WIDTH), F32)],
        compiler_params=_params(1),
        name="odd_p",
    )(h, h, w, cw, cb, lng, lnb, dw, db, dsc)


def _odd_s_kernel(z_ref, cs_ref, ds_ref, cw_ref, cb_ref, lng_ref, lnb_ref, dw_ref, db_ref, dsc_ref,
                  mix_ref, co_ref, do_ref, cbuf, dbuf):
    s_blk = SEQ_BLOCK
    gw = D_GROUP_WIDTH
    nc, nd = C_CONV_WIDTH - 1, POOL_MAX - 1
    for c in range(C_WIDTH // gw):
        ls = slice(gw * c, gw * (c + 1))
        accs = [jnp.broadcast_to(cb_ref[:, ls], (s_blk, gw)) for _ in range(DEC_SEQ)]
        for j in range(nc + DEC_SEQ):
            if j < nc:
                e = cs_ref[j, :, ls]
            else:
                t = j - nc
                e = z_ref[t, :, gw * c:gw * (c + 1)] * jax.nn.sigmoid(
                    z_ref[t, :, C_WIDTH + gw * c:C_WIDTH + gw * (c + 1)])
            if j >= DEC_SEQ:
                co_ref[j - DEC_SEQ, :, ls] = e
            for t in range(DEC_SEQ):
                k = j - t
                if 0 <= k < C_CONV_WIDTH:
                    accs[t] = accs[t] + cw_ref[k:k + 1, ls] * e
        for t in range(DEC_SEQ):
            cbuf[t, :, ls] = accs[t]
    for t in range(DEC_SEQ):
        y = _layer_norm(cbuf[t], lng_ref[...], lnb_ref[...])
        mix_ref[t, :, 0:C_WIDTH] = (y * jax.nn.sigmoid(y)).astype(BF16)

    for g, wsz in enumerate(POOL_SIZES):
        ls = slice(gw * g, gw * (g + 1))
        ext = []
        for j in range(nd + DEC_SEQ):
            if j < nd:
                e = ds_ref[j, :, ls]
            else:
                e = z_ref[j - nd, :, 2 * C_WIDTH + gw * g:2 * C_WIDTH + gw * (g + 1)]
            if j >= DEC_SEQ:
                do_ref[j - DEC_SEQ, :, ls] = e
            ext.append(e)
        for t in range(DEC_SEQ):
            win = ext[nd + t]
            for jj in range(1, wsz):
                win = win + ext[nd + t - jj]
            cnt = float(min(PAST_LEN + t + 1, wsz))
            dbuf[s_blk * t:s_blk * (t + 1), ls] = (win / cnt - ext[nd + t]).astype(BF16)
    for g in range(D_GROUPS):
        ls = slice(gw * g, gw * (g + 1))
        o = jnp.dot(dbuf[:, ls], dw_ref[g].astype(BF16), preferred_element_type=F32) + db_ref[:, ls]
        o = (o * dsc_ref[:, ls]).astype(BF16)
        for t in range(DEC_SEQ):
            mix_ref[t, :, C_WIDTH + gw * g:C_WIDTH + gw * (g + 1)] = o[s_blk * t:s_blk * (t + 1), :]


def _odd_s(z3, cs, ds, cw, cb, lng, lnb, dw, db, dsc):
    s_blk = SEQ_BLOCK
    nc, nd = C_CONV_WIDTH - 1, POOL_MAX - 1
    full = lambda shape: pl.BlockSpec(shape, lambda s: (0,) * len(shape))
    cst = pl.BlockSpec((nc, s_blk, C_WIDTH), lambda s: (0, s, 0))
    dst = pl.BlockSpec((nd, s_blk, D_WIDTH), lambda s: (0, s, 0))
    return pl.pallas_call(
        _odd_s_kernel,
        grid=(DEC_BATCH // s_blk,),
        in_specs=[pl.BlockSpec((DEC_SEQ, s_blk, ODD_IN), lambda s: (0, s, 0)), cst, dst,
                  full((C_CONV_WIDTH, C_WIDTH)), full((1, C_WIDTH)), full((1, C_WIDTH)), full((1, C_WIDTH)),
                  full((D_GROUPS, D_GROUP_WIDTH, D_GROUP_WIDTH)), full((1, D_WIDTH)), full((1, D_WIDTH))],
        out_specs=[pl.BlockSpec((DEC_SEQ, s_blk, D_MODEL), lambda s: (0, s, 0)), cst, dst],
        out_shape=[jax.ShapeDtypeStruct((DEC_SEQ, DEC_BATCH, D_MODEL), BF16),
                   jax.ShapeDtypeStruct((nc, DEC_BATCH, C_WIDTH), F32),
                   jax.ShapeDtypeStruct((nd, DEC_BATCH, D_WIDTH), F32)],
        scratch_shapes=[pltpu.VMEM((DEC_SEQ, s_blk, C_WIDTH), F32),
                        pltpu.VMEM((DEC_SEQ * s_blk, D_WIDTH), BF16)],
        compiler_params=_params(1),
        name="odd_s",
    )(z3, cs, ds, cw, cb, lng, lnb, dw, db, dsc)


FFN_TF = 512
FFN_TM = 1024
FFN_SUB_ROWS = (256, 256, 256, 256)
assert sum(FFN_SUB_ROWS) == FFN_TM


FFN_RES_ROWS = 128


def _ffn_p_kernel(*refs, cast_next, final_res):
    h_ref, wg_ref, wv_ref, wd_ref, cwg_ref, cwv_ref, cbg_ref, cbv_ref = refs[:8]
    if cast_next:
        nwu_ref, nwd_ref, o_ref, nfg_ref, nfv_ref, nwu_out, nwd_out, carry_g, carry_v = refs[8:]
        nwu_out[...] = nwu_ref[...].astype(BF16)
        nwd_out[...] = nwd_ref[...].astype(BF16)
    elif final_res:
        x_ref, gate_ref, gpost_ref, o_ref, nfg_ref, nfv_ref, carry_g, carry_v = refs[8:]
    else:
        o_ref, nfg_ref, nfv_ref, carry_g, carry_v = refs[8:]
    i, f = pl.program_id(1), pl.program_id(2)
    starts = [sum(FFN_SUB_ROWS[:s]) for s in range(len(FFN_SUB_ROWS))]

    @pl.when(i == 0)
    def _():
        carry_g[f] = jnp.zeros((SUBLANES, FFN_TF), F32)
        carry_v[f] = jnp.zeros((SUBLANES, FFN_TF), F32)

    @pl.when(f == 0)
    def _():
        o_ref[...] = jnp.zeros_like(o_ref)

    def conv(up, prev, cw_ref, cb_ref):
        w0, w1, w2, cb = cw_ref[0:1, :], cw_ref[1:2, :], cw_ref[2:3, :], cb_ref[...]
        body = w2 * up + w1 * pltpu.roll(up, 1, 0) + w0 * pltpu.roll(up, 2, 0) + cb
        both = jnp.concatenate([prev, up[0:SUBLANES, :]], axis=0)
        m1 = pltpu.roll(both, 1, 0)[SUBLANES:, :]
        m2 = pltpu.roll(both, 2, 0)[SUBLANES:, :]
        head = w2 * up[0:SUBLANES, :] + w1 * m1 + w0 * m2 + cb
        return jnp.concatenate([head, body[SUBLANES:, :]], axis=0)

    ups = []
    for r0, rh in zip(starts, FFN_SUB_ROWS):
        hs = h_ref[r0:r0 + rh, :]
        ups.append((jnp.dot(hs, wg_ref[...], preferred_element_type=F32),
                    jnp.dot(hs, wv_ref[...], preferred_element_type=F32)))
    prev_g, prev_v = carry_g[f], carry_v[f]
    for (ug, uv), r0, rh in zip(ups, starts, FFN_SUB_ROWS):
        g = conv(ug, prev_g, cwg_ref, cbg_ref)
        v = conv(uv, prev_v, cwv_ref, cbv_ref)
        prev_g, prev_v = ug[rh - SUBLANES:, :], uv[rh - SUBLANES:, :]
        act = (_gelu_tanh(g) * v).astype(BF16)
        o_ref[r0:r0 + rh, :] += jnp.dot(act, wd_ref[...], preferred_element_type=F32)
    carry_g[f] = prev_g
    carry_v[f] = prev_v
    nfg_ref[...] = prev_g
    nfv_ref[...] = prev_v

    if final_res:
        @pl.when(f == pl.num_programs(2) - 1)
        def _():
            for r in range(0, FFN_TM, FFN_RES_ROWS):
                rows = slice(r, r + FFN_RES_ROWS)
                o_ref[rows, :] = x_ref[rows, :] + gate_ref[...] * _rms(o_ref[rows, :], gpost_ref[...])


def _ffn_p(h, n, t, w_up, w_down, cw, cb, layer, w_layer, name, cast_next=None, final_res=None):
    assert cast_next is None or final_res is None
    tm, tf = FFN_TM, FFN_TF
    nf = D_FF // tf
    tpb = t // tm
    steps = n * tpb * nf
    in_specs = [pl.BlockSpec((tm, D_MODEL), lambda b, i, f: (b * tpb + i, 0)),
                pl.BlockSpec((None, D_MODEL, tf), lambda b, i, f: (w_layer, 0, f)),
                pl.BlockSpec((None, D_MODEL, tf), lambda b, i, f: (w_layer, 0, nf + f)),
                pl.BlockSpec((None, tf, D_MODEL), lambda b, i, f: (w_layer, f, 0)),
                pl.BlockSpec((None, FFN_CONV_WIDTH, tf), lambda b, i, f: (layer, 0, f)),
                pl.BlockSpec((None, FFN_CONV_WIDTH, tf), lambda b, i, f: (layer, 0, nf + f)),
                pl.BlockSpec((None, 1, tf), lambda b, i, f: (layer, 0, f)),
                pl.BlockSpec((None, 1, tf), lambda b, i, f: (layer, 0, nf + f))]
    out_specs = [pl.BlockSpec((tm, D_MODEL), lambda b, i, f: (b * tpb + i, 0)),
                 pl.BlockSpec((None, None, SUBLANES, tf), lambda b, i, f: (b, i, 0, f)),
                 pl.BlockSpec((None, None, SUBLANES, tf), lambda b, i, f: (b, i, 0, f))]
    out_shape = [jax.ShapeDtypeStruct((n * t, D_MODEL), F32),
                 jax.ShapeDtypeStruct((n, tpb, SUBLANES, D_FF), F32),
                 jax.ShapeDtypeStruct((n, tpb, SUBLANES, D_FF), F32)]
    args = [h, w_up, w_up, w_down, cw, cw, cb, cb]
    if cast_next is not None:
        nwu, nwd, nl = cast_next
        cu, cd = 2 * D_FF // steps, D_FF // steps
        step = lambda b, i, f: (b * tpb + i) * nf + f
        in_specs += [pl.BlockSpec((None, D_MODEL, cu), lambda b, i, f: (nl, 0, step(b, i, f))),
                     pl.BlockSpec((None, cd, D_MODEL), lambda b, i, f: (nl, step(b, i, f), 0))]
        out_specs += [pl.BlockSpec((None, D_MODEL, cu), lambda b, i, f: (0, 0, step(b, i, f))),
                      pl.BlockSpec((None, cd, D_MODEL), lambda b, i, f: (0, step(b, i, f), 0))]
        out_shape += [jax.ShapeDtypeStruct((1, D_MODEL, 2 * D_FF), BF16),
                      jax.ShapeDtypeStruct((1, D_FF, D_MODEL), BF16)]
        args += [nwu, nwd]
    vmem_limit = VMEM_LIMIT
    if final_res is not None:
        xres, mods, k_mod, gains, gain_idx = final_res
        in_specs += [pl.BlockSpec((tm, D_MODEL), lambda b, i, f: (b * tpb + i, 0)),
                     pl.BlockSpec((None, None, 1, D_MODEL), lambda b, i, f: (k_mod, b, 0, 2)),
                     pl.BlockSpec((None, 1, D_MODEL), lambda b, i, f: (gain_idx, 0, 0))]
        args += [xres, mods, gains]
        vmem_limit = FFN_RES_VMEM_LIMIT
    return pl.pallas_call(
        functools.partial(_ffn_p_kernel, cast_next=cast_next is not None, final_res=final_res is not None),
        grid=(n, tpb, nf),
        in_specs=in_specs, out_specs=out_specs, out_shape=out_shape,
        scratch_shapes=[pltpu.VMEM((nf, SUBLANES, tf), F32), pltpu.VMEM((nf, SUBLANES, tf), F32)],
        compiler_params=_params(3, vmem_limit),
        name=name,
    )(*args)


def _ffn_s_kernel(h_ref, wg_ref, wv_ref, wd_ref, cwg_ref, cwv_ref, cbg_ref, cbv_ref, sg_ref, sv_ref,
                  o_ref, nsg_ref, nsv_ref):
    f = pl.program_id(0)
    nb = DEC_BATCH
    h = h_ref[...]
    nst = FFN_CONV_WIDTH - 1

    def conv(w_ref, cw_ref, cb_ref, st_ref, ns_ref):
        up = jnp.dot(h, w_ref[...], preferred_element_type=F32)
        ext = [st_ref[j] for j in range(nst)]
        ext += [up[nb * t:nb * (t + 1), :] for t in range(DEC_SEQ)]
        for j in range(nst):
            ns_ref[j] = ext[DEC_SEQ + j]
        return [cw_ref[0:1, :] * ext[t] + cw_ref[1:2, :] * ext[t + 1] + cw_ref[2:3, :] * ext[t + 2] + cb_ref[...]
                for t in range(DEC_SEQ)]

    @pl.when(f == 0)
    def _():
        o_ref[...] = jnp.zeros_like(o_ref)

    g = conv(wg_ref, cwg_ref, cbg_ref, sg_ref, nsg_ref)
    v = conv(wv_ref, cwv_ref, cbv_ref, sv_ref, nsv_ref)
    pair = 2
    for t0 in range(0, DEC_SEQ, pair):
        act = jnp.concatenate([(_gelu_tanh(g[t]) * v[t]).astype(BF16) for t in range(t0, t0 + pair)], axis=0)
        o_ref[nb * t0:nb * (t0 + pair), :] += jnp.dot(act, wd_ref[...], preferred_element_type=F32)


FFN_S_TF = 512


def _ffn_s(h, state, w_up, w_down, cw, cb, layer, w_layer, name):
    tf = FFN_S_TF
    nf = D_FF // tf
    m = h.shape[0]
    nst = FFN_CONV_WIDTH - 1
    return pl.pallas_call(
        _ffn_s_kernel,
        grid=(nf,),
        in_specs=[pl.BlockSpec((m, D_MODEL), lambda f: (0, 0)),
                  pl.BlockSpec((None, D_MODEL, tf), lambda f: (w_layer, 0, f)),
                  pl.BlockSpec((None, D_MODEL, tf), lambda f: (w_layer, 0, nf + f)),
                  pl.BlockSpec((None, tf, D_MODEL), lambda f: (w_layer, f, 0)),
                  pl.BlockSpec((None, FFN_CONV_WIDTH, tf), lambda f: (layer, 0, f)),
                  pl.BlockSpec((None, FFN_CONV_WIDTH, tf), lambda f: (layer, 0, nf + f)),
                  pl.BlockSpec((None, 1, tf), lambda f: (layer, 0, f)),
                  pl.BlockSpec((None, 1, tf), lambda f: (layer, 0, nf + f)),
                  pl.BlockSpec((nst, DEC_BATCH, tf), lambda f: (0, 0, f)),
                  pl.BlockSpec((nst, DEC_BATCH, tf), lambda f: (0, 0, nf + f))],
        out_specs=[pl.BlockSpec((m, D_MODEL), lambda f: (0, 0)),
                   pl.BlockSpec((nst, DEC_BATCH, tf), lambda f: (0, 0, f)),
                   pl.BlockSpec((nst, DEC_BATCH, tf), lambda f: (0, 0, f))],
        out_shape=[jax.ShapeDtypeStruct((m, D_MODEL), F32),
                   jax.ShapeDtypeStruct((nst, DEC_BATCH, D_FF), F32),
                   jax.ShapeDtypeStruct((nst, DEC_BATCH, D_FF), F32)],
        compiler_params=_params(1),
        name=name,
    )(h, w_up, w_up, w_down, cw, cw, cb, cb, state, state)


def _rope_tables(pos):
    half = ROT_DIM // 2
    inv = ROPE_THETA ** (-jnp.arange(half, dtype=F32) * 2.0 / ROT_DIM)
    ang = pos.astype(F32)[:, None] * inv[None, :]
    cos, sin = jnp.cos(ang), jnp.sin(ang)
    t = pos.shape[0]
    rest = A_HEAD_DIM - ROT_DIM
    ch = jnp.concatenate([cos, cos, jnp.ones((t, rest), F32)], axis=1)
    ah = jnp.concatenate([-sin, jnp.zeros((t, half + rest), F32)], axis=1)
    bh = jnp.concatenate([jnp.zeros((t, half), F32), sin, jnp.zeros((t, rest), F32)], axis=1)
    rep = LANES // A_HEAD_DIM
    return tuple(jnp.tile(x, (1, rep)) for x in (ch, ah, bh))


def _pad_rows(x, rows):
    return jnp.concatenate([x, jnp.zeros((rows - x.shape[0],) + x.shape[1:], x.dtype)], axis=0)


def kernel(x_prompt, x_sample, cache_a_k, cache_a_v, state_c_conv, state_d_pool, state_ffn_conv, c_prompt, c_sample, norm_g, ada_w, ada_b, w_in_e, w_out_e, a_sinks, b_ln_g, b_ln_b, b_ws, b_bias, w_in_o, w_out_o, c_conv_w, c_conv_b, c_ln_g, c_ln_b, d_w, d_b, d_scale, ffn_w_up, ffn_conv_w, ffn_conv_b, ffn_w_down):
    assert DEPTH == 2 and x_prompt.shape == (BATCH, SEQ, D_MODEL) and x_sample.shape == (DEC_BATCH, DEC_SEQ, D_MODEL)
    w_in_e_b, w_out_e_b = w_in_e.astype(BF16), w_out_e.astype(BF16)
    w_in_o_b, w_out_o_b = w_in_o.astype(BF16), w_out_o.astype(BF16)

    pad_rows = DEC_BATCH + SUBLANES
    c_all = _pad_rows(jnp.concatenate([c_sample, c_prompt], axis=0), pad_rows)
    mods = _ada(c_all, ada_w.reshape(2 * DEPTH, D_MODEL, 3 * D_MODEL), ada_b.reshape(2 * DEPTH, 1, 3 * D_MODEL))
    mods_p = mods[:, DEC_BATCH:DEC_BATCH + BATCH].reshape(2 * DEPTH, BATCH, 1, 3 * D_MODEL)
    gains = norm_g.reshape(4 * DEPTH, 1, D_MODEL)

    grp_p = _Group(BATCH * SEQ, 512, mods_p, SEQ)
    grp_s = _Group(DEC_BATCH * DEC_SEQ, DEC_BATCH * DEC_SEQ, mods, None)
    xp = x_prompt.reshape(BATCH * SEQ, D_MODEL)
    xs = jnp.transpose(x_sample, (1, 0, 2)).reshape(DEC_SEQ * DEC_BATCH, D_MODEL)

    tabs_p = _rope_tables(jnp.arange(SEQ))
    tabs_s = tuple(_pad_rows(x, SUBLANES) for x in _rope_tables(PAST_LEN + jnp.arange(DEC_SEQ)))

    row1 = lambda v: v.reshape(1, -1)
    bexp_p = jnp.repeat(jnp.transpose(b_bias[0]), B_HEAD_DIM, axis=1)
    ws4 = jnp.tril(b_ws[0])[:, :DEC_SEQ, :DEC_SEQ]
    wexp_s = jnp.repeat(jnp.transpose(ws4, (1, 2, 0)).reshape(DEC_SEQ * DEC_SEQ, B_HEADS), B_HEAD_DIM, axis=1)
    bexp_s = _pad_rows(bexp_p[:DEC_SEQ], SUBLANES)
    sinkrow = jnp.repeat(a_sinks[0], DEC_SEQ).reshape(A_HEADS * DEC_SEQ, 1)

    ffn_cb = ffn_conv_b.reshape(DEPTH, 1, 2 * D_FF)

    def run(grp, x, is_prompt):
        outs = {}
        tag = "p" if is_prompt else "s"
        tm_mm = 1024 if is_prompt else grp.m
        if is_prompt:
            mix, nk, nv = _even_p(x, w_in_e_b, 0, gains, 0, grp.mods, 0, a_sinks[0], tabs_p, b_ws[0], bexp_p,
                                  row1(b_ln_g[0]), row1(b_ln_b[0]))
            outs["ak"], outs["av"] = nk, nv
        else:
            (h,) = _resnorm(grp, x, None, gains, nxt=(0, 0))
            z = _mm(h, w_in_e_b, 0, tm_mm, 896, "in_even_" + tag)
            mix, nk, nv, vb = _even_s(z.reshape(DEC_SEQ, DEC_BATCH, EVEN_IN),
                                      cache_a_k[0].reshape(DEC_BATCH * WINDOW, A_KV_WIDTH),
                                      cache_a_v[0].reshape(DEC_BATCH * WINDOW, A_KV_WIDTH),
                                      sinkrow, tabs_s, wexp_s, bexp_s, row1(b_ln_g[0]), row1(b_ln_b[0]))
            outs["ak"], outs["av"], outs["bv"] = nk, nv, vb
        if is_prompt:
            x, h, wd0 = _mm_res(grp, mix, w_out_e_b, 0, x, gains, (0, 1), (1, 2), "out_even_p", cast=(ffn_w_down, 0))
            ffn_wb[0] = (ffn_w_up[0:1].astype(BF16), wd0)
        else:
            x, h = _mm_res(grp, mix.reshape(grp.m, D_MODEL), w_out_e_b, 0, x, gains, (0, 1), (1, 2), "out_even_s")
        x, h, outs["ff0"] = ffn(grp, x, h, 0, is_prompt, nxt=(2, 4))
        dwa, dba, dsa = d_w[0], row1(d_b[0]), row1(d_scale[0])
        cargs = (c_conv_w[0], row1(c_conv_b[0]), row1(c_ln_g[0]), row1(c_ln_b[0]), dwa, dba, dsa)
        if is_prompt:
            mix, ct, dt = _odd_p(h, w_in_o_b, 0, *cargs)
            outs["cc"] = ct[:, C_TAIL - (C_CONV_WIDTH - 1):]
            outs["dp"] = dt[:, D_TAIL - (POOL_MAX - 1):]
        else:
            z = _mm(h, w_in_o_b, 0, tm_mm, 1024, "in_odd_" + tag)
            mix, co, do = _odd_s(z.reshape(DEC_SEQ, DEC_BATCH, ODD_IN),
                                 jnp.transpose(state_c_conv[0], (1, 0, 2)),
                                 jnp.transpose(state_d_pool[0], (1, 0, 2)), *cargs)
            outs["cc"] = jnp.transpose(co, (1, 0, 2))
            outs["dp"] = jnp.transpose(do, (1, 0, 2))
        x, h = _mm_res(grp, mix.reshape(grp.m, D_MODEL), w_out_o_b, 0, x, gains, (2, 5), (3, 6), "out_odd_" + tag)
        x, _, outs["ff1"] = ffn(grp, x, h, 1, is_prompt, nxt=None)
        return x, outs

    ffn_wb = {}

    def ffn(grp, x, h, i, is_prompt, nxt):
        nst = FFN_CONV_WIDTH - 1
        wu, wd = ffn_wb[i]
        if is_prompt:
            cast_next = (ffn_w_up, ffn_w_down, i + 1) if i + 1 < DEPTH else None
            final_res = (x, grp.mods, 2 * i + 1, gains, 4 * i + 3) if nxt is None else None
            res = _ffn_p(h, BATCH, SEQ, wu, wd, ffn_conv_w, ffn_cb, i, 0, "ffn_p", cast_next, final_res)
            out, nfg, nfv = res[:3]
            if cast_next is not None:
                ffn_wb[i + 1] = (res[3], res[4])
            nf = jnp.concatenate([nfg[:, -1, SUBLANES - nst:], nfv[:, -1, SUBLANES - nst:]], axis=-1)
            if final_res is not None:
                return out, None, nf
        else:
            st = jnp.transpose(state_ffn_conv[i], (1, 0, 2))
            out, nsg, nsv = _ffn_s(h, st, wu, wd, ffn_conv_w, ffn_cb, i, 0, "ffn_s")
            nf = jnp.transpose(jnp.concatenate([nsg, nsv], axis=-1), (1, 0, 2))
        res = _resnorm(grp, x, out, gains, res=(2 * i + 1, 4 * i + 3), nxt=nxt)
        if nxt is None:
            return res[0], None, nf
        return res[0], res[1], nf

    yp, op = run(grp_p, xp, True)
    ys, os_ = run(grp_s, xs, False)

    kv5 = lambda a, nb: a.reshape(1, nb, WINDOW, A_KV_HEADS, A_HEAD_DIM)
    y_prompt = yp.reshape(BATCH, SEQ, D_MODEL)
    y_sample = jnp.transpose(ys.reshape(DEC_SEQ, DEC_BATCH, D_MODEL), (1, 0, 2))
    return (y_prompt, y_sample,
            kv5(op["ak"], BATCH), kv5(os_["ak"], DEC_BATCH), kv5(op["av"], BATCH), kv5(os_["av"], DEC_BATCH),
            jnp.transpose(os_["bv"], (1, 0, 2))[None],
            op["cc"][None], os_["cc"][None], op["dp"][None], os_["dp"][None],
            jnp.stack([op["ff0"], op["ff1"]]), jnp.stack([os_["ff0"], os_["ff1"]]))
```

```python
import functools

import jax
import jax.numpy as jnp
from jax import lax
from jax.experimental import pallas as pl
from jax.experimental.pallas import tpu as pltpu

D_MODEL = 2048
BATCH = 2
SEQ = 4096
DEPTH = 2
DEC_BATCH = 128
DEC_SEQ = 4
PAST_LEN = 8192
A_HEADS = 16
A_KV_HEADS = 4
A_HEAD_DIM = 64
A_Q_WIDTH = A_HEADS * A_HEAD_DIM
A_KV_WIDTH = A_KV_HEADS * A_HEAD_DIM
WINDOW = 128
ROT_DIM = A_HEAD_DIM // 4
ROPE_THETA = 500000.0
B_HEADS = 8
B_HEAD_DIM = 128
B_WIDTH = B_HEADS * B_HEAD_DIM
CHUNK = 128
C_WIDTH = 1024
C_CONV_WIDTH = 31
POOL_SIZES = (2, 4, 8, 16)
D_GROUPS = len(POOL_SIZES)
POOL_MAX = max(POOL_SIZES)
D_WIDTH = 1024
D_GROUP_WIDTH = D_WIDTH // D_GROUPS
D_FF = 5632
FFN_CONV_WIDTH = 3
EVEN_IN = A_Q_WIDTH + 2 * A_KV_WIDTH + 2 * B_WIDTH
ODD_IN = 2 * C_WIDTH + D_WIDTH
EPS = 1e-6
NEG_INF = -1e30

LANES = 128
SUBLANES = 8
VMEM_LIMIT = 56 * 1024 * 1024
EVEN_P_VMEM_LIMIT = 60 * 1024 * 1024
FFN_RES_VMEM_LIMIT = 60 * 1024 * 1024

BF16 = jnp.bfloat16
F32 = jnp.float32

SEQ_BLOCK = 32
C_TAIL = 32
D_TAIL = 16


def _params(n_axes, vmem_limit=VMEM_LIMIT):
    return pltpu.CompilerParams(dimension_semantics=("arbitrary",) * n_axes,
                                vmem_limit_bytes=vmem_limit)


_GELU_C = 0.7978845608028654


def _gelu_tanh(x):
    hx = 0.5 * x
    return hx + hx * jnp.tanh(x * (_GELU_C + (_GELU_C * 0.044715) * (x * x)))


def _rms(x, g):
    return x * lax.rsqrt(jnp.mean(x * x, axis=-1, keepdims=True) + EPS) * g


def _layer_norm(x, g, b):
    mu = jnp.mean(x, axis=-1, keepdims=True)
    xc = x - mu
    var = jnp.mean(xc * xc, axis=-1, keepdims=True)
    return xc * lax.rsqrt(var + EPS) * g + b


def _rows(m, tm):
    r = m.shape[0]
    if r == 1 or r == tm:
        return m
    return jnp.concatenate([m] * (tm // r), axis=0)


def _rope(x, c, a, b):
    return x * c + pltpu.roll(x, LANES - ROT_DIM // 2, 1) * a + pltpu.roll(x, ROT_DIM // 2, 1) * b


def _ada_kernel(c_ref, w_ref, b_ref, o_ref):
    c = c_ref[...]
    s = (c * jax.nn.sigmoid(c)).astype(BF16)
    o_ref[...] = jnp.dot(s, w_ref[...].astype(BF16), preferred_element_type=F32) + b_ref[...]


def _ada(c_all, w, b):
    r = c_all.shape[0]
    tn = 1536
    n_sub, _, n = w.shape
    return pl.pallas_call(
        _ada_kernel,
        grid=(n_sub, n // tn),
        in_specs=[pl.BlockSpec((r, D_MODEL), lambda k, j: (0, 0)),
                  pl.BlockSpec((None, D_MODEL, tn), lambda k, j: (k, 0, j)),
                  pl.BlockSpec((None, 1, tn), lambda k, j: (k, 0, j))],
        out_specs=pl.BlockSpec((None, r, tn), lambda k, j: (k, 0, j)),
        out_shape=jax.ShapeDtypeStruct((n_sub, r, n), F32),
        compiler_params=_params(2),
        name="ada",
    )(c_all, w, b)


def _resnorm_kernel(*refs, has_res, has_next):
    it = iter(refs)
    x_ref = next(it)
    if has_res:
        o_ref, gate_ref, gpost_ref = next(it), next(it), next(it)
    if has_next:
        gpre_ref, scale_ref, shift_ref = next(it), next(it), next(it)
    if has_res:
        xo_ref = next(it)
    if has_next:
        ho_ref = next(it)
    x = x_ref[...]
    tm = x.shape[0]
    if has_res:
        x = x + _rows(gate_ref[...], tm) * _rms(o_ref[...], gpost_ref[...])
        xo_ref[...] = x
    if has_next:
        h = _rms(x, gpre_ref[...]) * (1.0 + _rows(scale_ref[...], tm)) + _rows(shift_ref[...], tm)
        ho_ref[...] = h.astype(BF16)


class _Group:
    def __init__(self, m, tm, mods, per_batch_rows):
        self.m, self.tm, self.mods = m, tm, mods
        self.per_batch_rows = per_batch_rows

    def mod_spec(self, k, col):
        if self.per_batch_rows is None:
            return pl.BlockSpec((None, DEC_BATCH, D_MODEL), lambda i: (k, 0, col))
        tpb = self.per_batch_rows // self.tm
        return pl.BlockSpec((None, None, 1, D_MODEL), lambda i: (k, i // tpb, 0, col))


def _resnorm(grp, x, out, gains, res=None, nxt=None):
    tm = grp.tm
    row = pl.BlockSpec((tm, D_MODEL), lambda i: (i, 0))
    args, specs, out_shapes, out_specs = [x], [row], [], []

    def gain_spec(idx):
        return pl.BlockSpec((None, 1, D_MODEL), lambda i: (idx, 0, 0))

    if res is not None:
        args += [out, grp.mods, gains]
        specs += [row, grp.mod_spec(res[0], 2), gain_spec(res[1])]
        out_shapes.append(jax.ShapeDtypeStruct((grp.m, D_MODEL), F32))
        out_specs.append(row)
    if nxt is not None:
        args += [gains, grp.mods, grp.mods]
        specs += [gain_spec(nxt[1]), grp.mod_spec(nxt[0], 1), grp.mod_spec(nxt[0], 0)]
        out_shapes.append(jax.ShapeDtypeStruct((grp.m, D_MODEL), BF16))
        out_specs.append(row)
    res_out = pl.pallas_call(
        functools.partial(_resnorm_kernel, has_res=res is not None, has_next=nxt is not None),
        grid=(grp.m // tm,),
        in_specs=specs, out_specs=out_specs, out_shape=out_shapes,
        compiler_params=_params(1),
        name="resnorm",
    )(*args)
    return res_out


def _mm_kernel(a_ref, w_ref, o_ref):
    o_ref[...] = jnp.dot(a_ref[...], w_ref[...], preferred_element_type=F32).astype(o_ref.dtype)


def _mm(a, w, layer, tm, tn, name, out_dtype=F32):
    m, k = a.shape
    n = w.shape[2]
    return pl.pallas_call(
        _mm_kernel,
        grid=(n // tn, m // tm),
        in_specs=[pl.BlockSpec((tm, k), lambda j, i: (i, 0)),
                  pl.BlockSpec((None, k, tn), lambda j, i: (layer, 0, j))],
        out_specs=pl.BlockSpec((tm, tn), lambda j, i: (i, j)),
        out_shape=jax.ShapeDtypeStruct((m, n), out_dtype),
        compiler_params=_params(2),
        name=name,
    )(a, w)


MM_RES_SPLIT = 4


def _mm_res_kernel(*refs, cast):
    a_ref, w_ref, x_ref, gate_ref, gpost_ref, gpre_ref, scale_ref, shift_ref = refs[:8]
    if cast:
        cin_ref, xo_ref, ho_ref, cout_ref = refs[8:]
        cout_ref[...] = cin_ref[...].astype(BF16)
    else:
        xo_ref, ho_ref = refs[8:]
    tm = a_ref.shape[0]
    rs = max(tm // MM_RES_SPLIT, gate_ref.shape[0])
    outs = [jnp.dot(a_ref[r:r + rs, :], w_ref[...], preferred_element_type=F32) for r in range(0, tm, rs)]
    for r, out in zip(range(0, tm, rs), outs):
        x = x_ref[r:r + rs, :] + _rows(gate_ref[...], rs) * _rms(out, gpost_ref[...])
        xo_ref[r:r + rs, :] = x
        h = _rms(x, gpre_ref[...]) * (1.0 + _rows(scale_ref[...], rs)) + _rows(shift_ref[...], rs)
        ho_ref[r:r + rs, :] = h.astype(BF16)


def _mm_res(grp, a, w, layer, x, gains, res, nxt, name, cast=None):
    tm = grp.tm
    k = a.shape[1]
    n = grp.m // tm
    row = pl.BlockSpec((tm, D_MODEL), lambda i: (i, 0))
    gain_spec = lambda idx: pl.BlockSpec((None, 1, D_MODEL), lambda i: (idx, 0, 0))
    in_specs = [pl.BlockSpec((tm, k), lambda i: (i, 0)),
                pl.BlockSpec((None, k, D_MODEL), lambda i: (layer, 0, 0)),
                row, grp.mod_spec(res[0], 2), gain_spec(res[1]),
                gain_spec(nxt[1]), grp.mod_spec(nxt[0], 1), grp.mod_spec(nxt[0], 0)]
    out_specs = [row, row]
    out_shape = [jax.ShapeDtypeStruct((grp.m, D_MODEL), F32), jax.ShapeDtypeStruct((grp.m, D_MODEL), BF16)]
    args = [a, w, x, grp.mods, gains, gains, grp.mods, grp.mods]
    if cast is not None:
        cw, cl = cast
        cr = cw.shape[1] // n
        in_specs.append(pl.BlockSpec((None, cr, cw.shape[2]), lambda i: (cl, i, 0)))
        out_specs.append(pl.BlockSpec((None, cr, cw.shape[2]), lambda i: (0, i, 0)))
        out_shape.append(jax.ShapeDtypeStruct((1,) + cw.shape[1:], BF16))
        args.append(cw)
    return pl.pallas_call(
        functools.partial(_mm_res_kernel, cast=cast is not None),
        grid=(n,),
        in_specs=in_specs, out_specs=out_specs, out_shape=out_shape,
        compiler_params=_params(1),
        name=name,
    )(*args)


CORE_ROWS = 512
CORE_HALF = CORE_ROWS // 2


PROJ_CHUNK = 512


def _proj_chunks(dst, lhs_ref, r0, w_ref):
    def make(c):
        def run():
            cols = slice(c, c + PROJ_CHUNK)
            dst[:, cols] = jnp.dot(lhs_ref[r0:r0 + CORE_HALF, :], w_ref[:, cols], preferred_element_type=F32)
        return run
    return iter([make(c) for c in range(0, w_ref.shape[1], PROJ_CHUNK)])


def _issue(pre, n=1):
    for _ in range(n):
        thunk = next(pre, None)
        if thunk is not None:
            thunk()


def _even_block(pre, z, r0, o0, first_lim, sink_ref, cos_ref, sa_ref, sb_ref, ws_ref, bexp_ref, lng_ref, lnb_ref,
                mix_ref, nk_ref, nv_ref, kprev, vprev):
    qb = WINDOW
    zr = slice(r0, r0 + qb)
    orow = slice(o0, o0 + qb)
    cos_t, sa_t, sb_t = cos_ref[orow, :], sa_ref[orow, :], sb_ref[orow, :]
    lane = lax.broadcasted_iota(jnp.int32, (qb, LANES), 1)
    lo = lane < A_HEAD_DIM
    lane2 = lax.broadcasted_iota(jnp.int32, (2 * qb, LANES), 1)
    lo2 = lane2 < A_HEAD_DIM
    row = lax.broadcasted_iota(jnp.int32, (qb, 2 * qb), 0)
    col = lax.broadcasted_iota(jnp.int32, (qb, 2 * qb), 1)
    mask = ((col > row) & (col < qb)) | ((col >= qb) & (col - qb <= row))
    if first_lim is not None:
        mask = mask & (col >= first_lim)
    nt = (((1,), (1,)), ((), ()))

    for jk in range(A_KV_HEADS // 2):
        _issue(pre)
        ksl = slice(A_Q_WIDTH + LANES * jk, A_Q_WIDTH + LANES * (jk + 1))
        vsl = slice(A_Q_WIDTH + A_KV_WIDTH + LANES * jk, A_Q_WIDTH + A_KV_WIDTH + LANES * (jk + 1))
        csl = slice(LANES * jk, LANES * (jk + 1))
        kc = _rope(z[zr, ksl], cos_t, sa_t, sb_t)
        vc = z[zr, vsl]
        kall = jnp.concatenate([kprev[:, csl], kc], axis=0)
        vall = jnp.concatenate([vprev[:, csl], vc], axis=0)
        krol = pltpu.roll(kall, A_HEAD_DIM, 1)
        vrol = pltpu.roll(vall, A_HEAD_DIM, 1)
        for sub in range(2):
            kvh = 2 * jk + sub
            kd = (jnp.where(lo2, kall, krol) if sub == 0 else jnp.where(lo2, krol, kall)).astype(BF16)
            vd = (jnp.where(lo2, vall, vrol) if sub == 0 else jnp.where(lo2, vrol, vall)).astype(BF16)
            for qs in range(2):
                js = 2 * kvh + qs
                qsl = _rope(z[zr, LANES * js:LANES * (js + 1)], cos_t, sa_t, sb_t) * (A_HEAD_DIM ** -0.5)
                outs = []
                for half in range(2):
                    h = 2 * js + half
                    qm = jnp.where(lo if half == 0 else jnp.logical_not(lo), qsl, 0.0).astype(BF16)
                    s = lax.dot_general(qm, kd, nt, preferred_element_type=F32)
                    s = jnp.where(mask, s, NEG_INF)
                    sink = sink_ref[h]
                    m = jnp.maximum(jnp.max(s, axis=-1, keepdims=True), sink)
                    p = jnp.exp(s - m)
                    den = jnp.sum(p, axis=-1, keepdims=True) + jnp.exp(sink - m)
                    o = jnp.dot(p.astype(BF16), vd, preferred_element_type=F32)
                    outs.append(o * (1.0 / den))
                mix_ref[orow, LANES * js:LANES * (js + 1)] = jnp.where(lo, outs[0], outs[1]).astype(BF16)
        kprev[:, csl] = kc
        vprev[:, csl] = vc
        nk_ref[:, csl] = kc
        nv_ref[:, csl] = vc

    _issue(pre, 2)
    zb0 = A_Q_WIDTH + 2 * A_KV_WIDTH
    u = _gelu_tanh(z[zr, zb0:zb0 + B_WIDTH])
    vb = _layer_norm(_gelu_tanh(z[zr, zb0 + B_WIDTH:zb0 + 2 * B_WIDTH]), lng_ref[...], lnb_ref[...])
    ri = lax.broadcasted_iota(jnp.int32, (CHUNK, CHUNK), 0)
    ci = lax.broadcasted_iota(jnp.int32, (CHUNK, CHUNK), 1)
    tri = ri >= ci
    for h in range(B_HEADS):
        hs = slice(B_HEAD_DIM * h, B_HEAD_DIM * (h + 1))
        w = jnp.where(tri, ws_ref[h], 0.0).astype(BF16)
        mixed = jnp.dot(w, vb[:, hs].astype(BF16), preferred_element_type=F32) + bexp_ref[:, hs]
        mix_ref[orow, A_Q_WIDTH + B_HEAD_DIM * h:A_Q_WIDTH + B_HEAD_DIM * (h + 1)] = (u[:, hs] * mixed).astype(BF16)


def _even_p_kernel(sink_ref, xc_ref, xn_ref, w_hbm, gpre_ref, scc_ref, shc_ref, scn_ref, shn_ref,
                   cos_ref, sa_ref, sb_ref, ws_ref, bexp_ref, lng_ref, lnb_ref,
                   mix_ref, nk_ref, nv_ref, w_ref, za, zb, hb, hn, kprev, vprev, *, layer):
    step = pl.program_id(0)
    b = step % (SEQ // CORE_ROWS)
    args = (sink_ref, cos_ref, sa_ref, sb_ref, ws_ref, bexp_ref, lng_ref, lnb_ref, mix_ref, nk_ref, nv_ref,
            kprev, vprev)

    def prenorm(x, sc_ref, sh_ref):
        return (_rms(x, gpre_ref[...]) * (1.0 + sc_ref[...]) + sh_ref[...]).astype(BF16)

    @pl.when(step == 0)
    def _():
        pltpu.sync_copy(w_hbm.at[layer], w_ref)
        hb[...] = prenorm(xc_ref[0:CORE_HALF, :], scc_ref, shc_ref)
        za[...] = jnp.dot(hb[...], w_ref[...], preferred_element_type=F32)

    @pl.when(b == 0)
    def _():
        kprev[...] = jnp.zeros_like(kprev)
        vprev[...] = jnp.zeros_like(vprev)

    hb[...] = prenorm(xc_ref[CORE_HALF:CORE_ROWS, :], scc_ref, shc_ref)
    hn[...] = prenorm(xn_ref[...], scn_ref, shn_ref)
    pre = _proj_chunks(zb, hb, 0, w_ref)
    _even_block(pre, za, 0, 0, jnp.where(b > 0, 0, WINDOW), *args)
    _even_block(pre, za, WINDOW, WINDOW, None, *args)
    _issue(pre, EVEN_IN // PROJ_CHUNK)
    pre = _proj_chunks(za, hn, 0, w_ref)
    _even_block(pre, zb, 0, CORE_HALF, None, *args)
    _even_block(pre, zb, WINDOW, CORE_HALF + WINDOW, None, *args)
    _issue(pre, EVEN_IN // PROJ_CHUNK)


def _even_p(x, w, layer, gains, gain_idx, mods, k_mod, sinks, tabs, ws, bexp, lng, lnb):
    h = x
    rows = CORE_ROWS
    steps = h.shape[0] // rows
    per_seq = SEQ // rows
    last_half = h.shape[0] // CORE_HALF - 1
    nxt_half = lambda s: jnp.minimum(2 * s + 2, last_half)
    mod_c = lambda col: pl.BlockSpec((None, None, 1, D_MODEL), lambda s: (k_mod, s // per_seq, 0, col))
    mod_n = lambda col: pl.BlockSpec((None, None, 1, D_MODEL),
                                     lambda s: (k_mod, nxt_half(s) // (2 * per_seq), 0, col))
    full = lambda shape: pl.BlockSpec(shape, lambda s: (0,) * len(shape))
    tab = pl.BlockSpec((rows, LANES), lambda s: (s % per_seq, 0))
    kv = pl.BlockSpec((None, WINDOW, A_KV_WIDTH), lambda s: (s // per_seq, 0, 0))
    return pl.pallas_call(
        functools.partial(_even_p_kernel, layer=layer),
        grid=(steps,),
        in_specs=[pl.BlockSpec(memory_space=pltpu.SMEM),
                  pl.BlockSpec((rows, D_MODEL), lambda s: (s, 0)),
                  pl.BlockSpec((CORE_HALF, D_MODEL), lambda s: (nxt_half(s), 0)),
                  pl.BlockSpec(memory_space=pl.ANY),
                  pl.BlockSpec((None, 1, D_MODEL), lambda s: (gain_idx, 0, 0)),
                  mod_c(1), mod_c(0), mod_n(1), mod_n(0),
                  tab, tab, tab,
                  full((B_HEADS, CHUNK, CHUNK)),
                  pl.BlockSpec((CHUNK, B_WIDTH), lambda s: (0, 0)),
                  full((1, B_WIDTH)), full((1, B_WIDTH))],
        out_specs=[pl.BlockSpec((rows, D_MODEL), lambda s: (s, 0)), kv, kv],
        out_shape=[jax.ShapeDtypeStruct((h.shape[0], D_MODEL), BF16),
                   jax.ShapeDtypeStruct((BATCH, WINDOW, A_KV_WIDTH), F32),
                   jax.ShapeDtypeStruct((BATCH, WINDOW, A_KV_WIDTH), F32)],
        scratch_shapes=[pltpu.VMEM((D_MODEL, EVEN_IN), BF16),
                        pltpu.VMEM((CORE_HALF, EVEN_IN), F32), pltpu.VMEM((CORE_HALF, EVEN_IN), F32),
                        pltpu.VMEM((CORE_HALF, D_MODEL), BF16), pltpu.VMEM((CORE_HALF, D_MODEL), BF16),
                        pltpu.VMEM((WINDOW, A_KV_WIDTH), F32), pltpu.VMEM((WINDOW, A_KV_WIDTH), F32)],
        compiler_params=_params(1, EVEN_P_VMEM_LIMIT),
        name="even_p",
    )(sinks, x, x, w, gains, mods, mods, mods, mods, *tabs, ws, bexp, lng, lnb)


def _even_s_kernel(sinkrow_ref, z_ref, ck_ref, cv_ref, cos_ref, sa_ref, sb_ref, wexp_ref, bexp_ref,
                   lng_ref, lnb_ref, mix_ref, ok_ref, ov_ref, vb_ref, qm, osc, kn, vn):
    s_blk = SEQ_BLOCK
    rows_per_seq = A_HEADS * DEC_SEQ
    lane256 = lax.broadcasted_iota(jnp.int32, (s_blk, A_KV_WIDTH), 1)
    kn[...] = jnp.zeros_like(kn)
    vn[...] = jnp.zeros_like(vn)
    halves = A_KV_WIDTH // LANES

    def put(ref, start, stride, val):
        for c in range(halves):
            ref[c, pl.ds(start, s_blk, stride=stride), :] = val[:, LANES * c:LANES * (c + 1)]

    def get_rows(ref, start, size):
        return jnp.concatenate([ref[c, pl.ds(start, size), :] for c in range(halves)], axis=1)

    def get_strided(ref, start, stride):
        return jnp.concatenate([ref[c, pl.ds(start, s_blk, stride=stride), :] for c in range(halves)], axis=1)

    for t in range(DEC_SEQ):
        c, a, bb = cos_ref[t:t + 1, :], sa_ref[t:t + 1, :], sb_ref[t:t + 1, :]

        def rope2(lo_lane):
            return jnp.concatenate([_rope(z_ref[t, :, lo_lane:lo_lane + LANES], c, a, bb),
                                    _rope(z_ref[t, :, lo_lane + LANES:lo_lane + 2 * LANES], c, a, bb)], axis=1)

        put(kn, t, SUBLANES, rope2(A_Q_WIDTH))
        put(vn, t, SUBLANES, z_ref[t, :, A_Q_WIDTH + A_KV_WIDTH:A_Q_WIDTH + 2 * A_KV_WIDTH])
        for j in range(A_KV_HEADS):
            qs = rope2(A_KV_WIDTH * j) * (A_HEAD_DIM ** -0.5)
            keep = (lane256 >= A_HEAD_DIM * j) & (lane256 < A_HEAD_DIM * (j + 1))
            for g in range(A_HEADS // A_KV_HEADS):
                sh = ((j - g) * A_HEAD_DIM) % A_KV_WIDTH
                qr = qs if sh == 0 else pltpu.roll(qs, sh, 1)
                r = (4 * j + g) * DEC_SEQ + t
                put(qm, r, rows_per_seq, jnp.where(keep, qr, 0.0))

    rr = lax.broadcasted_iota(jnp.int32, (rows_per_seq, WINDOW), 0) % DEC_SEQ
    cc = lax.broadcasted_iota(jnp.int32, (rows_per_seq, WINDOW), 1)
    mask_old = cc > rr
    new0 = WINDOW - DEC_SEQ
    mask_new = (cc >= new0) & (cc - new0 <= rr)
    row8 = lax.broadcasted_iota(jnp.int32, (SUBLANES, A_KV_WIDTH), 0)
    sink = sinkrow_ref[...]
    nt = (((1,), (1,)), ((), ()))

    def shifted(c_ref, new_ref, o_ref, n):
        base = pl.multiple_of(n * WINDOW, WINDOW)
        old = c_ref[pl.ds(base, WINDOW), :]
        rolled = pltpu.roll(old, WINDOW - DEC_SEQ, 0)
        fresh = get_rows(new_ref, pl.multiple_of(n * SUBLANES, SUBLANES), SUBLANES)
        tail = jnp.where(row8 >= SUBLANES - DEC_SEQ, pltpu.roll(fresh, SUBLANES - DEC_SEQ, 0),
                         rolled[WINDOW - SUBLANES:, :])
        new = jnp.concatenate([rolled[:WINDOW - SUBLANES, :], tail], axis=0)
        o_ref[pl.ds(base, WINDOW), :] = new
        return old.astype(BF16), new.astype(BF16)

    def body(n, carry):
        k_old, k_new = shifted(ck_ref, kn, ok_ref, n)
        v_old, v_new = shifted(cv_ref, vn, ov_ref, n)
        q = get_rows(qm, pl.multiple_of(n * rows_per_seq, rows_per_seq), rows_per_seq).astype(BF16)
        s_old = jnp.where(mask_old, lax.dot_general(q, k_old, nt, preferred_element_type=F32), NEG_INF)
        s_new = jnp.where(mask_new, lax.dot_general(q, k_new, nt, preferred_element_type=F32), NEG_INF)
        m = jnp.maximum(jnp.maximum(jnp.max(s_old, axis=-1, keepdims=True),
                                    jnp.max(s_new, axis=-1, keepdims=True)), sink)
        p_old = jnp.exp(s_old - m)
        p_new = jnp.exp(s_new - m)
        den = (jnp.sum(p_old, axis=-1, keepdims=True) + jnp.sum(p_new, axis=-1, keepdims=True)
               + jnp.exp(sink - m))
        o = (jnp.dot(p_old.astype(BF16), v_old, preferred_element_type=F32)
             + jnp.dot(p_new.astype(BF16), v_new, preferred_element_type=F32))
        o = o * (1.0 / den)
        for c in range(halves):
            osc[c, pl.ds(pl.multiple_of(n * rows_per_seq, rows_per_seq), rows_per_seq), :] = o[:, LANES * c:LANES * (c + 1)]
        return carry

    lax.fori_loop(0, s_blk, body, 0, unroll=8)

    for t in range(DEC_SEQ):
        for j in range(A_KV_HEADS):
            acc = jnp.zeros((s_blk, A_KV_WIDTH), F32)
            for g in range(A_HEADS // A_KV_HEADS):
                r = (4 * j + g) * DEC_SEQ + t
                ov = get_strided(osc, r, rows_per_seq)
                sh = ((g - j) * A_HEAD_DIM) % A_KV_WIDTH
                orr = ov if sh == 0 else pltpu.roll(ov, sh, 1)
                acc = jnp.where((lane256 >= A_HEAD_DIM * g) & (lane256 < A_HEAD_DIM * (g + 1)), orr, acc)
            mix_ref[t, :, A_KV_WIDTH * j:A_KV_WIDTH * (j + 1)] = acc.astype(BF16)

    zb0 = A_Q_WIDTH + 2 * A_KV_WIDTH
    us, vbs = [], []
    for t in range(DEC_SEQ):
        us.append(_gelu_tanh(z_ref[t, :, zb0:zb0 + B_WIDTH]))
        v = _layer_norm(_gelu_tanh(z_ref[t, :, zb0 + B_WIDTH:zb0 + 2 * B_WIDTH]), lng_ref[...], lnb_ref[...])
        vb_ref[t] = v
        vbs.append(v)
    for t in range(DEC_SEQ):
        mixed = bexp_ref[t:t + 1, :]
        for s in range(t + 1):
            mixed = mixed + wexp_ref[DEC_SEQ * t + s:DEC_SEQ * t + s + 1, :] * vbs[s]
        mix_ref[t, :, A_Q_WIDTH:A_Q_WIDTH + B_WIDTH] = (us[t] * mixed).astype(BF16)


def _even_s(z3, ck, cv, sinkrow, tabs, wexp, bexp, lng, lnb):
    s_blk = SEQ_BLOCK
    rows_per_seq = A_HEADS * DEC_SEQ
    full = lambda shape: pl.BlockSpec(shape, lambda s: (0,) * len(shape))
    cache = pl.BlockSpec((s_blk * WINDOW, A_KV_WIDTH), lambda s: (s, 0))
    return pl.pallas_call(
        _even_s_kernel,
        grid=(DEC_BATCH // s_blk,),
        in_specs=[full((rows_per_seq, 1)),
                  pl.BlockSpec((DEC_SEQ, s_blk, EVEN_IN), lambda s: (0, s, 0)),
                  cache, cache,
                  full((SUBLANES, LANES)), full((SUBLANES, LANES)), full((SUBLANES, LANES)),
                  full((DEC_SEQ * DEC_SEQ, B_WIDTH)), full((SUBLANES, B_WIDTH)),
                  full((1, B_WIDTH)), full((1, B_WIDTH))],
        out_specs=[pl.BlockSpec((DEC_SEQ, s_blk, D_MODEL), lambda s: (0, s, 0)),
                   cache, cache,
                   pl.BlockSpec((DEC_SEQ, s_blk, B_WIDTH), lambda s: (0, s, 0))],
        out_shape=[jax.ShapeDtypeStruct((DEC_SEQ, DEC_BATCH, D_MODEL), BF16),
                   jax.ShapeDtypeStruct((DEC_BATCH * WINDOW, A_KV_WIDTH), F32),
                   jax.ShapeDtypeStruct((DEC_BATCH * WINDOW, A_KV_WIDTH), F32),
                   jax.ShapeDtypeStruct((DEC_SEQ, DEC_BATCH, B_WIDTH), F32)],
        scratch_shapes=[pltpu.VMEM((A_KV_WIDTH // LANES, s_blk * rows_per_seq, LANES), F32),
                        pltpu.VMEM((A_KV_WIDTH // LANES, s_blk * rows_per_seq, LANES), F32),
                        pltpu.VMEM((A_KV_WIDTH // LANES, s_blk * SUBLANES, LANES), F32),
                        pltpu.VMEM((A_KV_WIDTH // LANES, s_blk * SUBLANES, LANES), F32)],
        compiler_params=_params(1),
        name="even_s",
    )(sinkrow, z3, ck, cv, *tabs, wexp, bexp, lng, lnb)


ODD_ROWS = CORE_HALF
ODD_RC = 64


def _odd_block(pre, z_ref, o0, pos0, cw_ref, cb_ref, lng_ref, lnb_ref, dw_ref, db_ref, dsc_ref,
               mix_ref, ctail_ref, dtail_ref, gext, dext, cbuf, dbuf, shbuf):
    tr = ODD_ROWS
    _issue(pre, 2)
    gext[C_TAIL:C_TAIL + tr, :] = z_ref[:, 0:C_WIDTH] * jax.nn.sigmoid(z_ref[:, C_WIDTH:2 * C_WIDTH])
    dext[D_TAIL:D_TAIL + tr, :] = z_ref[:, 2 * C_WIDTH:2 * C_WIDTH + D_WIDTH]

    gw = D_GROUP_WIDTH
    lead = C_TAIL - (C_CONV_WIDTH - 1)
    sh_rows = C_TAIL + tr - SUBLANES
    for s in range(1, SUBLANES):
        shbuf[s, 0:sh_rows, :] = gext[s:s + sh_rows, :]

    def tap(off, r0, ls):
        a, s = divmod(off, SUBLANES)
        src = gext if s == 0 else shbuf.at[s]
        return src[r0 + SUBLANES * a:r0 + SUBLANES * a + ODD_RC, ls]

    for rc in range(tr // ODD_RC):
        r0 = rc * ODD_RC
        _issue(pre)
        for c in range(C_WIDTH // gw):
            ls = slice(gw * c, gw * (c + 1))
            acc = jnp.broadcast_to(cb_ref[:, ls], (ODD_RC, gw))
            for k in range(C_CONV_WIDTH):
                acc = acc + cw_ref[k:k + 1, ls] * tap(lead + k, r0, ls)
            cbuf[r0:r0 + ODD_RC, ls] = acc
        y = _layer_norm(cbuf[r0:r0 + ODD_RC, :], lng_ref[...], lnb_ref[...])
        mix_ref[o0 + r0:o0 + r0 + ODD_RC, 0:C_WIDTH] = (y * jax.nn.sigmoid(y)).astype(BF16)

        pos1 = (pos0 + r0 + 1 + lax.broadcasted_iota(jnp.int32, (ODD_RC, gw), 0))
        for g, wsz in enumerate(POOL_SIZES):
            ls = slice(gw * g, gw * (g + 1))
            win = dext[D_TAIL + r0:D_TAIL + r0 + ODD_RC, ls]
            for jj in range(1, wsz):
                win = win + dext[D_TAIL + r0 - jj:D_TAIL + r0 - jj + ODD_RC, ls]
            cnt = jnp.minimum(pos1, wsz).astype(F32)
            diff = win / cnt - dext[D_TAIL + r0:D_TAIL + r0 + ODD_RC, ls]
            dbuf[r0:r0 + ODD_RC, ls] = diff.astype(BF16)

    for g in range(D_GROUPS):
        ls = slice(gw * g, gw * (g + 1))
        o = jnp.dot(dbuf[:, ls], dw_ref[g].astype(BF16), preferred_element_type=F32) + db_ref[:, ls]
        mix_ref[o0:o0 + tr, C_WIDTH + gw * g:C_WIDTH + gw * (g + 1)] = (o * dsc_ref[:, ls]).astype(BF16)

    gext[0:C_TAIL, :] = gext[tr:tr + C_TAIL, :]
    dext[0:D_TAIL, :] = dext[tr:tr + D_TAIL, :]
    ctail_ref[...] = gext[0:C_TAIL, :]
    dtail_ref[...] = dext[0:D_TAIL, :]


def _odd_p_kernel(hc_ref, hn_ref, w_ref, cw_ref, cb_ref, lng_ref, lnb_ref, dw_ref, db_ref, dsc_ref,
                  mix_ref, ctail_ref, dtail_ref, za, zb, gext, dext, cbuf, dbuf, shbuf):
    step = pl.program_id(0)
    b = step % (SEQ // CORE_ROWS)
    args = (cw_ref, cb_ref, lng_ref, lnb_ref, dw_ref, db_ref, dsc_ref, mix_ref, ctail_ref, dtail_ref,
            gext, dext, cbuf, dbuf, shbuf)

    @pl.when(step == 0)
    def _():
        za[...] = jnp.dot(hc_ref[0:CORE_HALF, :], w_ref[...], preferred_element_type=F32)

    @pl.when(b == 0)
    def _():
        gext[0:C_TAIL, :] = jnp.zeros((C_TAIL, C_WIDTH), F32)
        dext[0:D_TAIL, :] = jnp.zeros((D_TAIL, D_WIDTH), F32)

    pre = _proj_chunks(zb, hc_ref, CORE_HALF, w_ref)
    _odd_block(pre, za, 0, b * CORE_ROWS, *args)
    _issue(pre, ODD_IN // PROJ_CHUNK)
    pre = _proj_chunks(za, hn_ref, 0, w_ref)
    _odd_block(pre, zb, CORE_HALF, b * CORE_ROWS + CORE_HALF, *args)
    _issue(pre, ODD_IN // PROJ_CHUNK)


def _odd_p(h, w, layer, cw, cb, lng, lnb, dw, db, dsc):
    rows = CORE_ROWS
    steps = h.shape[0] // rows
    per_seq = SEQ // rows
    last_half = h.shape[0] // CORE_HALF - 1
    tr = ODD_ROWS
    full = lambda shape: pl.BlockSpec(shape, lambda s: (0,) * len(shape))
    return pl.pallas_call(
        _odd_p_kernel,
        grid=(steps,),
        in_specs=[pl.BlockSpec((rows, D_MODEL), lambda s: (s, 0)),
                  pl.BlockSpec((CORE_HALF, D_MODEL), lambda s: (jnp.minimum(2 * s + 2, last_half), 0)),
                  pl.BlockSpec((None, D_MODEL, ODD_IN), lambda s: (layer, 0, 0)),
                  full((C_CONV_WIDTH, C_WIDTH)), full((1, C_WIDTH)), full((1, C_WIDTH)), full((1, C_WIDTH)),
                  full((D_GROUPS, D_GROUP_WIDTH, D_GROUP_WIDTH)), full((1, D_WIDTH)), full((1, D_WIDTH))],
        out_specs=[pl.BlockSpec((rows, D_MODEL), lambda s: (s, 0)),
                   pl.BlockSpec((None, C_TAIL, C_WIDTH), lambda s: (s // per_seq, 0, 0)),
                   pl.BlockSpec((None, D_TAIL, D_WIDTH), lambda s: (s // per_seq, 0, 0))],
        out_shape=[jax.ShapeDtypeStruct((h.shape[0], D_MODEL), BF16),
                   jax.ShapeDtypeStruct((BATCH, C_TAIL, C_WIDTH), F32),
                   jax.ShapeDtypeStruct((BATCH, D_TAIL, D_WIDTH), F32)],
        scratch_shapes=[pltpu.VMEM((CORE_HALF, ODD_IN), F32), pltpu.VMEM((CORE_HALF, ODD_IN), F32),
                        pltpu.VMEM((C_TAIL + tr, C_WIDTH), F32), pltpu.VMEM((D_TAIL + tr, D_WIDTH), F32),
                        pltpu.VMEM((tr, C_WIDTH), F32), pltpu.VMEM((tr, D_WIDTH), BF16),
                        pltpu.VMEM((SUBLANES, C_TAIL + tr, C_WIDTH), F32)],
        compiler_params=_params(1),
        name="odd_p",
    )(h, h, w, cw, cb, lng, lnb, dw, db, dsc)


def _odd_s_kernel(z_ref, cs_ref, ds_ref, cw_ref, cb_ref, lng_ref, lnb_ref, dw_ref, db_ref, dsc_ref,
                  mix_ref, co_ref, do_ref, cbuf, dbuf):
    s_blk = SEQ_BLOCK
    gw = D_GROUP_WIDTH
    nc, nd = C_CONV_WIDTH - 1, POOL_MAX - 1
    for c in range(C_WIDTH // gw):
        ls = slice(gw * c, gw * (c + 1))
        accs = [jnp.broadcast_to(cb_ref[:, ls], (s_blk, gw)) for _ in range(DEC_SEQ)]
        for j in range(nc + DEC_SEQ):
            if j < nc:
                e = cs_ref[j, :, ls]
            else:
                t = j - nc
                e = z_ref[t, :, gw * c:gw * (c + 1)] * jax.nn.sigmoid(
                    z_ref[t, :, C_WIDTH + gw * c:C_WIDTH + gw * (c + 1)])
            if j >= DEC_SEQ:
                co_ref[j - DEC_SEQ, :, ls] = e
            for t in range(DEC_SEQ):
                k = j - t
                if 0 <= k < C_CONV_WIDTH:
                    accs[t] = accs[t] + cw_ref[k:k + 1, ls] * e
        for t in range(DEC_SEQ):
            cbuf[t, :, ls] = accs[t]
    for t in range(DEC_SEQ):
        y = _layer_norm(cbuf[t], lng_ref[...], lnb_ref[...])
        mix_ref[t, :, 0:C_WIDTH] = (y * jax.nn.sigmoid(y)).astype(BF16)

    for g, wsz in enumerate(POOL_SIZES):
        ls = slice(gw * g, gw * (g + 1))
        ext = []
        for j in range(nd + DEC_SEQ):
            if j < nd:
                e = ds_ref[j, :, ls]
            else:
                e = z_ref[j - nd, :, 2 * C_WIDTH + gw * g:2 * C_WIDTH + gw * (g + 1)]
            if j >= DEC_SEQ:
                do_ref[j - DEC_SEQ, :, ls] = e
            ext.append(e)
        for t in range(DEC_SEQ):
            win = ext[nd + t]
            for jj in range(1, wsz):
                win = win + ext[nd + t - jj]
            cnt = float(min(PAST_LEN + t + 1, wsz))
            dbuf[s_blk * t:s_blk * (t + 1), ls] = (win / cnt - ext[nd + t]).astype(BF16)
    for g in range(D_GROUPS):
        ls = slice(gw * g, gw * (g + 1))
        o = jnp.dot(dbuf[:, ls], dw_ref[g].astype(BF16), preferred_element_type=F32) + db_ref[:, ls]
        o = (o * dsc_ref[:, ls]).astype(BF16)
        for t in range(DEC_SEQ):
            mix_ref[t, :, C_WIDTH + gw * g:C_WIDTH + gw * (g + 1)] = o[s_blk * t:s_blk * (t + 1), :]


def _odd_s(z3, cs, ds, cw, cb, lng, lnb, dw, db, dsc):
    s_blk = SEQ_BLOCK
    nc, nd = C_CONV_WIDTH - 1, POOL_MAX - 1
    full = lambda shape: pl.BlockSpec(shape, lambda s: (0,) * len(shape))
    cst = pl.BlockSpec((nc, s_blk, C_WIDTH), lambda s: (0, s, 0))
    dst = pl.BlockSpec((nd, s_blk, D_WIDTH), lambda s: (0, s, 0))
    return pl.pallas_call(
        _odd_s_kernel,
        grid=(DEC_BATCH // s_blk,),
        in_specs=[pl.BlockSpec((DEC_SEQ, s_blk, ODD_IN), lambda s: (0, s, 0)), cst, dst,
                  full((C_CONV_WIDTH, C_WIDTH)), full((1, C_WIDTH)), full((1, C_WIDTH)), full((1, C_WIDTH)),
                  full((D_GROUPS, D_GROUP_WIDTH, D_GROUP_WIDTH)), full((1, D_WIDTH)), full((1, D_WIDTH))],
        out_specs=[pl.BlockSpec((DEC_SEQ, s_blk, D_MODEL), lambda s: (0, s, 0)), cst, dst],
        out_shape=[jax.ShapeDtypeStruct((DEC_SEQ, DEC_BATCH, D_MODEL), BF16),
                   jax.ShapeDtypeStruct((nc, DEC_BATCH, C_WIDTH), F32),
                   jax.ShapeDtypeStruct((nd, DEC_BATCH, D_WIDTH), F32)],
        scratch_shapes=[pltpu.VMEM((DEC_SEQ, s_blk, C_WIDTH), F32),
                        pltpu.VMEM((DEC_SEQ * s_blk, D_WIDTH), BF16)],
        compiler_params=_params(1),
        name="odd_s",
    )(z3, cs, ds, cw, cb, lng, lnb, dw, db, dsc)


FFN_TF = 512
FFN_TM = 1024
FFN_SUB_ROWS = (256, 256, 256, 256)
assert sum(FFN_SUB_ROWS) == FFN_TM


FFN_RES_ROWS = 128


def _ffn_p_kernel(*refs, cast_next, final_res):
    h_ref, wg_ref, wv_ref, wd_ref, cwg_ref, cwv_ref, cbg_ref, cbv_ref = refs[:8]
    if cast_next:
        nwu_ref, nwd_ref, o_ref, nfg_ref, nfv_ref, nwu_out, nwd_out, carry_g, carry_v = refs[8:]
        nwu_out[...] = nwu_ref[...].astype(BF16)
        nwd_out[...] = nwd_ref[...].astype(BF16)
    elif final_res:
        x_ref, gate_ref, gpost_ref, o_ref, nfg_ref, nfv_ref, carry_g, carry_v = refs[8:]
    else:
        o_ref, nfg_ref, nfv_ref, carry_g, carry_v = refs[8:]
    i, f = pl.program_id(1), pl.program_id(2)
    starts = [sum(FFN_SUB_ROWS[:s]) for s in range(len(FFN_SUB_ROWS))]

    @pl.when(i == 0)
    def _():
        carry_g[f] = jnp.zeros((SUBLANES, FFN_TF), F32)
        carry_v[f] = jnp.zeros((SUBLANES, FFN_TF), F32)

    @pl.when(f == 0)
    def _():
        o_ref[...] = jnp.zeros_like(o_ref)

    def conv(up, prev, cw_ref, cb_ref):
        w0, w1, w2, cb = cw_ref[0:1, :], cw_ref[1:2, :], cw_ref[2:3, :], cb_ref[...]
        body = w2 * up + w1 * pltpu.roll(up, 1, 0) + w0 * pltpu.roll(up, 2, 0) + cb
        both = jnp.concatenate([prev, up[0:SUBLANES, :]], axis=0)
        m1 = pltpu.roll(both, 1, 0)[SUBLANES:, :]
        m2 = pltpu.roll(both, 2, 0)[SUBLANES:, :]
        head = w2 * up[0:SUBLANES, :] + w1 * m1 + w0 * m2 + cb
        return jnp.concatenate([head, body[SUBLANES:, :]], axis=0)

    ups = []
    for r0, rh in zip(starts, FFN_SUB_ROWS):
        hs = h_ref[r0:r0 + rh, :]
        ups.append((jnp.dot(hs, wg_ref[...], preferred_element_type=F32),
                    jnp.dot(hs, wv_ref[...], preferred_element_type=F32)))
    prev_g, prev_v = carry_g[f], carry_v[f]
    for (ug, uv), r0, rh in zip(ups, starts, FFN_SUB_ROWS):
        g = conv(ug, prev_g, cwg_ref, cbg_ref)
        v = conv(uv, prev_v, cwv_ref, cbv_ref)
        prev_g, prev_v = ug[rh - SUBLANES:, :], uv[rh - SUBLANES:, :]
        act = (_gelu_tanh(g) * v).astype(BF16)
        o_ref[r0:r0 + rh, :] += jnp.dot(act, wd_ref[...], preferred_element_type=F32)
    carry_g[f] = prev_g
    carry_v[f] = prev_v
    nfg_ref[...] = prev_g
    nfv_ref[...] = prev_v

    if final_res:
        @pl.when(f == pl.num_programs(2) - 1)
        def _():
            for r in range(0, FFN_TM, FFN_RES_ROWS):
                rows = slice(r, r + FFN_RES_ROWS)
                o_ref[rows, :] = x_ref[rows, :] + gate_ref[...] * _rms(o_ref[rows, :], gpost_ref[...])


def _ffn_p(h, n, t, w_up, w_down, cw, cb, layer, w_layer, name, cast_next=None, final_res=None):
    assert cast_next is None or final_res is None
    tm, tf = FFN_TM, FFN_TF
    nf = D_FF // tf
    tpb = t // tm
    steps = n * tpb * nf
    in_specs = [pl.BlockSpec((tm, D_MODEL), lambda b, i, f: (b * tpb + i, 0)),
                pl.BlockSpec((None, D_MODEL, tf), lambda b, i, f: (w_layer, 0, f)),
                pl.BlockSpec((None, D_MODEL, tf), lambda b, i, f: (w_layer, 0, nf + f)),
                pl.BlockSpec((None, tf, D_MODEL), lambda b, i, f: (w_layer, f, 0)),
                pl.BlockSpec((None, FFN_CONV_WIDTH, tf), lambda b, i, f: (layer, 0, f)),
                pl.BlockSpec((None, FFN_CONV_WIDTH, tf), lambda b, i, f: (layer, 0, nf + f)),
                pl.BlockSpec((None, 1, tf), lambda b, i, f: (layer, 0, f)),
                pl.BlockSpec((None, 1, tf), lambda b, i, f: (layer, 0, nf + f))]
    out_specs = [pl.BlockSpec((tm, D_MODEL), lambda b, i, f: (b * tpb + i, 0)),
                 pl.BlockSpec((None, None, SUBLANES, tf), lambda b, i, f: (b, i, 0, f)),
                 pl.BlockSpec((None, None, SUBLANES, tf), lambda b, i, f: (b, i, 0, f))]
    out_shape = [jax.ShapeDtypeStruct((n * t, D_MODEL), F32),
                 jax.ShapeDtypeStruct((n, tpb, SUBLANES, D_FF), F32),
                 jax.ShapeDtypeStruct((n, tpb, SUBLANES, D_FF), F32)]
    args = [h, w_up, w_up, w_down, cw, cw, cb, cb]
    if cast_next is not None:
        nwu, nwd, nl = cast_next
        cu, cd = 2 * D_FF // steps, D_FF // steps
        step = lambda b, i, f: (b * tpb + i) * nf + f
        in_specs += [pl.BlockSpec((None, D_MODEL, cu), lambda b, i, f: (nl, 0, step(b, i, f))),
                     pl.BlockSpec((None, cd, D_MODEL), lambda b, i, f: (nl, step(b, i, f), 0))]
        out_specs += [pl.BlockSpec((None, D_MODEL, cu), lambda b, i, f: (0, 0, step(b, i, f))),
                      pl.BlockSpec((None, cd, D_MODEL), lambda b, i, f: (0, step(b, i, f), 0))]
        out_shape += [jax.ShapeDtypeStruct((1, D_MODEL, 2 * D_FF), BF16),
                      jax.ShapeDtypeStruct((1, D_FF, D_MODEL), BF16)]
        args += [nwu, nwd]
    vmem_limit = VMEM_LIMIT
    if final_res is not None:
        xres, mods, k_mod, gains, gain_idx = final_res
        in_specs += [pl.BlockSpec((tm, D_MODEL), lambda b, i, f: (b * tpb + i, 0)),
                     pl.BlockSpec((None, None, 1, D_MODEL), lambda b, i, f: (k_mod, b, 0, 2)),
                     pl.BlockSpec((None, 1, D_MODEL), lambda b, i, f: (gain_idx, 0, 0))]
        args += [xres, mods, gains]
        vmem_limit = FFN_RES_VMEM_LIMIT
    return pl.pallas_call(
        functools.partial(_ffn_p_kernel, cast_next=cast_next is not None, final_res=final_res is not None),
        grid=(n, tpb, nf),
        in_specs=in_specs, out_specs=out_specs, out_shape=out_shape,
        scratch_shapes=[pltpu.VMEM((nf, SUBLANES, tf), F32), pltpu.VMEM((nf, SUBLANES, tf), F32)],
        compiler_params=_params(3, vmem_limit),
        name=name,
    )(*args)


def _ffn_s_kernel(*refs, has_next):
    h_ref, wg_ref, wv_ref, wd_ref, cwg_ref, cwv_ref, cbg_ref, cbv_ref, sg_ref, sv_ref = refs[:10]
    x_ref, gate_ref, gpost_ref = refs[10:13]
    if has_next:
        gpre_ref, scale_ref, shift_ref, o_ref, nsg_ref, nsv_ref, ho_ref = refs[13:]
    else:
        o_ref, nsg_ref, nsv_ref = refs[13:]
    f = pl.program_id(0)
    nb = DEC_BATCH
    h = h_ref[...]
    nst = FFN_CONV_WIDTH - 1

    def conv(w_ref, cw_ref, cb_ref, st_ref, ns_ref):
        up = jnp.dot(h, w_ref[...], preferred_element_type=F32)
        ext = [st_ref[j] for j in range(nst)]
        ext += [up[nb * t:nb * (t + 1), :] for t in range(DEC_SEQ)]
        for j in range(nst):
            ns_ref[j] = ext[DEC_SEQ + j]
        return [cw_ref[0:1, :] * ext[t] + cw_ref[1:2, :] * ext[t + 1] + cw_ref[2:3, :] * ext[t + 2] + cb_ref[...]
                for t in range(DEC_SEQ)]

    @pl.when(f == 0)
    def _():
        o_ref[...] = jnp.zeros_like(o_ref)

    g = conv(wg_ref, cwg_ref, cbg_ref, sg_ref, nsg_ref)
    v = conv(wv_ref, cwv_ref, cbv_ref, sv_ref, nsv_ref)
    pair = 2
    for t0 in range(0, DEC_SEQ, pair):
        act = jnp.concatenate([(_gelu_tanh(g[t]) * v[t]).astype(BF16) for t in range(t0, t0 + pair)], axis=0)
        o_ref[nb * t0:nb * (t0 + pair), :] += jnp.dot(act, wd_ref[...], preferred_element_type=F32)

    @pl.when(f == pl.num_programs(0) - 1)
    def _():
        for t in range(DEC_SEQ):
            rows = slice(nb * t, nb * (t + 1))
            x = x_ref[rows, :] + gate_ref[...] * _rms(o_ref[rows, :], gpost_ref[...])
            o_ref[rows, :] = x
            if has_next:
                ho_ref[rows, :] = (_rms(x, gpre_ref[...]) * (1.0 + scale_ref[...]) + shift_ref[...]).astype(BF16)


FFN_S_TF = 512


def _ffn_s(grp, h, state, w_up, w_down, cw, cb, layer, w_layer, name, x, gains, res, nxt):
    tf = FFN_S_TF
    nf = D_FF // tf
    m = h.shape[0]
    nst = FFN_CONV_WIDTH - 1
    const = lambda spec: pl.BlockSpec(spec.block_shape, lambda f, _m=spec.index_map: _m(0))
    gain_spec = lambda idx: pl.BlockSpec((None, 1, D_MODEL), lambda f: (idx, 0, 0))
    extra_specs = [pl.BlockSpec((m, D_MODEL), lambda f: (0, 0)), const(grp.mod_spec(res[0], 2)), gain_spec(res[1])]
    extra_args = [x, grp.mods, gains]
    extra_out_specs, extra_out_shape = [], []
    if nxt is not None:
        extra_specs += [gain_spec(nxt[1]), const(grp.mod_spec(nxt[0], 1)), const(grp.mod_spec(nxt[0], 0))]
        extra_args += [gains, grp.mods, grp.mods]
        extra_out_specs.append(pl.BlockSpec((m, D_MODEL), lambda f: (0, 0)))
        extra_out_shape.append(jax.ShapeDtypeStruct((m, D_MODEL), BF16))
    return pl.pallas_call(
        functools.partial(_ffn_s_kernel, has_next=nxt is not None),
        grid=(nf,),
        in_specs=[pl.BlockSpec((m, D_MODEL), lambda f: (0, 0)),
                  pl.BlockSpec((None, D_MODEL, tf), lambda f: (w_layer, 0, f)),
                  pl.BlockSpec((None, D_MODEL, tf), lambda f: (w_layer, 0, nf + f)),
                  pl.BlockSpec((None, tf, D_MODEL), lambda f: (w_layer, f, 0)),
                  pl.BlockSpec((None, FFN_CONV_WIDTH, tf), lambda f: (layer, 0, f)),
                  pl.BlockSpec((None, FFN_CONV_WIDTH, tf), lambda f: (layer, 0, nf + f)),
                  pl.BlockSpec((None, 1, tf), lambda f: (layer, 0, f)),
                  pl.BlockSpec((None, 1, tf), lambda f: (layer, 0, nf + f)),
                  pl.BlockSpec((nst, DEC_BATCH, tf), lambda f: (0, 0, f)),
                  pl.BlockSpec((nst, DEC_BATCH, tf), lambda f: (0, 0, nf + f))] + extra_specs,
        out_specs=[pl.BlockSpec((m, D_MODEL), lambda f: (0, 0)),
                   pl.BlockSpec((nst, DEC_BATCH, tf), lambda f: (0, 0, f)),
                   pl.BlockSpec((nst, DEC_BATCH, tf), lambda f: (0, 0, f))] + extra_out_specs,
        out_shape=[jax.ShapeDtypeStruct((m, D_MODEL), F32),
                   jax.ShapeDtypeStruct((nst, DEC_BATCH, D_FF), F32),
                   jax.ShapeDtypeStruct((nst, DEC_BATCH, D_FF), F32)] + extra_out_shape,
        compiler_params=_params(1),
        name=name,
    )(h, w_up, w_up, w_down, cw, cw, cb, cb, state, state, *extra_args)


def _rope_tables(pos):
    half = ROT_DIM // 2
    inv = ROPE_THETA ** (-jnp.arange(half, dtype=F32) * 2.0 / ROT_DIM)
    ang = pos.astype(F32)[:, None] * inv[None, :]
    cos, sin = jnp.cos(ang), jnp.sin(ang)
    t = pos.shape[0]
    rest = A_HEAD_DIM - ROT_DIM
    ch = jnp.concatenate([cos, cos, jnp.ones((t, rest), F32)], axis=1)
    ah = jnp.concatenate([-sin, jnp.zeros((t, half + rest), F32)], axis=1)
    bh = jnp.concatenate([jnp.zeros((t, half), F32), sin, jnp.zeros((t, rest), F32)], axis=1)
    rep = LANES // A_HEAD_DIM
    return tuple(jnp.tile(x, (1, rep)) for x in (ch, ah, bh))


def _pad_rows(x, rows):
    return jnp.concatenate([x, jnp.zeros((rows - x.shape[0],) + x.shape[1:], x.dtype)], axis=0)


def kernel(x_prompt, x_sample, cache_a_k, cache_a_v, state_c_conv, state_d_pool, state_ffn_conv, c_prompt, c_sample, norm_g, ada_w, ada_b, w_in_e, w_out_e, a_sinks, b_ln_g, b_ln_b, b_ws, b_bias, w_in_o, w_out_o, c_conv_w, c_conv_b, c_ln_g, c_ln_b, d_w, d_b, d_scale, ffn_w_up, ffn_conv_w, ffn_conv_b, ffn_w_down):
    assert DEPTH == 2 and x_prompt.shape == (BATCH, SEQ, D_MODEL) and x_sample.shape == (DEC_BATCH, DEC_SEQ, D_MODEL)
    w_in_e_b, w_out_e_b = w_in_e.astype(BF16), w_out_e.astype(BF16)
    w_in_o_b, w_out_o_b = w_in_o.astype(BF16), w_out_o.astype(BF16)

    pad_rows = DEC_BATCH + SUBLANES
    c_all = _pad_rows(jnp.concatenate([c_sample, c_prompt], axis=0), pad_rows)
    mods = _ada(c_all, ada_w.reshape(2 * DEPTH, D_MODEL, 3 * D_MODEL), ada_b.reshape(2 * DEPTH, 1, 3 * D_MODEL))
    mods_p = mods[:, DEC_BATCH:DEC_BATCH + BATCH].reshape(2 * DEPTH, BATCH, 1, 3 * D_MODEL)
    gains = norm_g.reshape(4 * DEPTH, 1, D_MODEL)

    grp_p = _Group(BATCH * SEQ, 512, mods_p, SEQ)
    grp_s = _Group(DEC_BATCH * DEC_SEQ, DEC_BATCH * DEC_SEQ, mods, None)
    xp = x_prompt.reshape(BATCH * SEQ, D_MODEL)
    xs = jnp.transpose(x_sample, (1, 0, 2)).reshape(DEC_SEQ * DEC_BATCH, D_MODEL)

    tabs_p = _rope_tables(jnp.arange(SEQ))
    tabs_s = tuple(_pad_rows(x, SUBLANES) for x in _rope_tables(PAST_LEN + jnp.arange(DEC_SEQ)))

    row1 = lambda v: v.reshape(1, -1)
    bexp_p = jnp.repeat(jnp.transpose(b_bias[0]), B_HEAD_DIM, axis=1)
    ws4 = jnp.tril(b_ws[0])[:, :DEC_SEQ, :DEC_SEQ]
    wexp_s = jnp.repeat(jnp.transpose(ws4, (1, 2, 0)).reshape(DEC_SEQ * DEC_SEQ, B_HEADS), B_HEAD_DIM, axis=1)
    bexp_s = _pad_rows(bexp_p[:DEC_SEQ], SUBLANES)
    sinkrow = jnp.repeat(a_sinks[0], DEC_SEQ).reshape(A_HEADS * DEC_SEQ, 1)

    ffn_cb = ffn_conv_b.reshape(DEPTH, 1, 2 * D_FF)

    def run(grp, x, is_prompt):
        outs = {}
        tag = "p" if is_prompt else "s"
        tm_mm = 1024 if is_prompt else grp.m
        if is_prompt:
            mix, nk, nv = _even_p(x, w_in_e_b, 0, gains, 0, grp.mods, 0, a_sinks[0], tabs_p, b_ws[0], bexp_p,
                                  row1(b_ln_g[0]), row1(b_ln_b[0]))
            outs["ak"], outs["av"] = nk, nv
        else:
            (h,) = _resnorm(grp, x, None, gains, nxt=(0, 0))
            z = _mm(h, w_in_e_b, 0, tm_mm, 896, "in_even_" + tag)
            mix, nk, nv, vb = _even_s(z.reshape(DEC_SEQ, DEC_BATCH, EVEN_IN),
                                      cache_a_k[0].reshape(DEC_BATCH * WINDOW, A_KV_WIDTH),
                                      cache_a_v[0].reshape(DEC_BATCH * WINDOW, A_KV_WIDTH),
                                      sinkrow, tabs_s, wexp_s, bexp_s, row1(b_ln_g[0]), row1(b_ln_b[0]))
            outs["ak"], outs["av"], outs["bv"] = nk, nv, vb
        if is_prompt:
            x, h, wd0 = _mm_res(grp, mix, w_out_e_b, 0, x, gains, (0, 1), (1, 2), "out_even_p", cast=(ffn_w_down, 0))
            ffn_wb[0] = (ffn_w_up[0:1].astype(BF16), wd0)
        else:
            x, h = _mm_res(grp, mix.reshape(grp.m, D_MODEL), w_out_e_b, 0, x, gains, (0, 1), (1, 2), "out_even_s")
        x, h, outs["ff0"] = ffn(grp, x, h, 0, is_prompt, nxt=(2, 4))
        dwa, dba, dsa = d_w[0], row1(d_b[0]), row1(d_scale[0])
        cargs = (c_conv_w[0], row1(c_conv_b[0]), row1(c_ln_g[0]), row1(c_ln_b[0]), dwa, dba, dsa)
        if is_prompt:
            mix, ct, dt = _odd_p(h, w_in_o_b, 0, *cargs)
            outs["cc"] = ct[:, C_TAIL - (C_CONV_WIDTH - 1):]
            outs["dp"] = dt[:, D_TAIL - (POOL_MAX - 1):]
        else:
            z = _mm(h, w_in_o_b, 0, tm_mm, 1024, "in_odd_" + tag)
            mix, co, do = _odd_s(z.reshape(DEC_SEQ, DEC_BATCH, ODD_IN),
                                 jnp.transpose(state_c_conv[0], (1, 0, 2)),
                                 jnp.transpose(state_d_pool[0], (1, 0, 2)), *cargs)
            outs["cc"] = jnp.transpose(co, (1, 0, 2))
            outs["dp"] = jnp.transpose(do, (1, 0, 2))
        x, h = _mm_res(grp, mix.reshape(grp.m, D_MODEL), w_out_o_b, 0, x, gains, (2, 5), (3, 6), "out_odd_" + tag)
        x, _, outs["ff1"] = ffn(grp, x, h, 1, is_prompt, nxt=None)
        return x, outs

    ffn_wb = {}

    def ffn(grp, x, h, i, is_prompt, nxt):
        nst = FFN_CONV_WIDTH - 1
        wu, wd = ffn_wb[i]
        if is_prompt:
            cast_next = (ffn_w_up, ffn_w_down, i + 1) if i + 1 < DEPTH else None
            final_res = (x, grp.mods, 2 * i + 1, gains, 4 * i + 3) if nxt is None else None
            res = _ffn_p(h, BATCH, SEQ, wu, wd, ffn_conv_w, ffn_cb, i, 0, "ffn_p", cast_next, final_res)
            out, nfg, nfv = res[:3]
            if cast_next is not None:
                ffn_wb[i + 1] = (res[3], res[4])
            nf = jnp.concatenate([nfg[:, -1, SUBLANES - nst:], nfv[:, -1, SUBLANES - nst:]], axis=-1)
            if final_res is not None:
                return out, None, nf
        else:
            st = jnp.transpose(state_ffn_conv[i], (1, 0, 2))
            res = _ffn_s(grp, h, st, wu, wd, ffn_conv_w, ffn_cb, i, 0, "ffn_s", x, gains,
                         (2 * i + 1, 4 * i + 3), nxt)
            nf = jnp.transpose(jnp.concatenate([res[1], res[2]], axis=-1), (1, 0, 2))
            return res[0], (res[3] if nxt is not None else None), nf
        res = _resnorm(grp, x, out, gains, res=(2 * i + 1, 4 * i + 3), nxt=nxt)
        if nxt is None:
            return res[0], None, nf
        return res[0], res[1], nf

    yp, op = run(grp_p, xp, True)
    ys, os_ = run(grp_s, xs, False)

    kv5 = lambda a, nb: a.reshape(1, nb, WINDOW, A_KV_HEADS, A_HEAD_DIM)
    y_prompt = yp.reshape(BATCH, SEQ, D_MODEL)
    y_sample = jnp.transpose(ys.reshape(DEC_SEQ, DEC_BATCH, D_MODEL), (1, 0, 2))
    return (y_prompt, y_sample,
            kv5(op["ak"], BATCH), kv5(os_["ak"], DEC_BATCH), kv5(op["av"], BATCH), kv5(os_["av"], DEC_BATCH),
            jnp.transpose(os_["bv"], (1, 0, 2))[None],
            op["cc"][None], os_["cc"][None], op["dp"][None], os_["dp"][None],
            jnp.stack([op["ff0"], op["ff1"]]), jnp.stack([os_["ff0"], os_["ff1"]]))
```
